```python
import jax, jax.numpy as jnp
from jax import lax
import numpy as np

D_MODEL = 2048
BATCH = 4
SEQ = 2048
DEPTH = 1

HEAD_DIM = 64
N_HEADS_DIL = (D_MODEL // 2) // HEAD_DIM
N_HEADS_SB = (D_MODEL // 2) // HEAD_DIM
DIL_WIDTH = N_HEADS_DIL * HEAD_DIM
SB_WIDTH = N_HEADS_SB * HEAD_DIM
MIX_WIDTH = DIL_WIDTH + SB_WIDTH
IN_WIDTH = 3 * DIL_WIDTH + 3 * SB_WIDTH
DIL_PATTERNS = ((128, 1), (512, 4), (2048, 16))
BLOCK = 128
ROPE_THETA = 10000.0
N_EXPERTS = 32
TOP_K = 4
D_EXPERT = D_MODEL
EXPERT_BLOCK = 128
SWIGLU_LIMIT = 7.0
SWIGLU_ALPHA = 1.702
NORM_EPS = 1e-5

kernel_name = "hymba_dilated_stickbreaking_moe_layer"


def rms_norm(x, g):
    xf = x.astype(jnp.float32)
    y = xf * lax.rsqrt(jnp.mean(xf * xf, axis=-1, keepdims=True) + NORM_EPS)
    return (y * g.astype(jnp.float32)).astype(x.dtype)


def rope(x, pos):
    half = x.shape[-1] // 2
    inv_freq = 1.0 / (ROPE_THETA ** (jnp.arange(half, dtype=jnp.float32) * 2.0 / x.shape[-1]))
    ang = pos.astype(jnp.float32)[:, None] * inv_freq[None, :]
    c = jnp.cos(ang)[None, :, None, :]
    s = jnp.sin(ang)[None, :, None, :]
    xf = x.astype(jnp.float32)
    x1, x2 = xf[..., :half], xf[..., half:]
    return jnp.concatenate([x1 * c - x2 * s, x2 * c + x1 * s], axis=-1).astype(x.dtype)


def dilated_window_attention(q, k, v, window, dilation):
    B, S, H, Dh = q.shape
    L = S // dilation
    w = window // dilation
    nb = -(-L // BLOCK)
    Lp = nb * BLOCK
    def split(t):
        return t.reshape(B, L, dilation, H, Dh)
    qd = jnp.pad(split(q), ((0, 0), (0, Lp - L), (0, 0), (0, 0), (0, 0)))
    qd = qd.reshape(B, nb, BLOCK, dilation, H, Dh)
    pad_kv = ((0, 0), (BLOCK, Lp - L), (0, 0), (0, 0), (0, 0))
    kd = jnp.pad(split(k), pad_kv).reshape(B, nb + 1, BLOCK, dilation, H, Dh)
    vd = jnp.pad(split(v), pad_kv).reshape(B, nb + 1, BLOCK, dilation, H, Dh)
    kb = jnp.concatenate([kd[:, :-1], kd[:, 1:]], axis=2)
    vb = jnp.concatenate([vd[:, :-1], vd[:, 1:]], axis=2)
    s = jnp.einsum('bnqrhe,bnkrhe->bnrhqk', qd, kb,
                   preferred_element_type=jnp.float32) * (HEAD_DIM ** -0.5)
    qi = jnp.arange(BLOCK)
    ki = jnp.arange(2 * BLOCK)
    dist = qi[:, None] + BLOCK - ki[None, :]
    key_u = jnp.arange(nb)[:, None] * BLOCK - BLOCK + ki[None, :]
    valid = ((dist >= 0) & (dist <= w))[None] & (key_u >= 0)[:, None, :]
    valid = valid[None, :, None, None]
    s = jnp.where(valid, s, -jnp.inf)
    m = jnp.max(s, axis=-1, keepdims=True)
    p = jnp.exp(s - m)
    den = jnp.sum(p, axis=-1, keepdims=True)
    o = jnp.einsum('bnrhqk,bnkrhe->bnqrhe', p, vb.astype(jnp.float32))
    den_t = jnp.moveaxis(den[..., 0], -1, 2)
    o = o / den_t[..., None]
    lse = jnp.moveaxis((m + jnp.log(den))[..., 0], -1, 2)
    o = o.reshape(B, Lp, dilation, H, Dh)[:, :L].reshape(B, S, H, Dh)
    lse = lse.reshape(B, Lp, dilation, H)[:, :L].reshape(B, S, H)
    return o, lse


def stick_breaking_attention(q, k, v):
    B, S, H, Dh = q.shape
    nbq = S // BLOCK
    qb = q.reshape(B, nbq, BLOCK, H, Dh).swapaxes(0, 1)
    kpos = jnp.arange(S)
    v32 = v.astype(jnp.float32)
    def block(args):
        qblk, i = args
        qpos = i * BLOCK + jnp.arange(BLOCK)
        z = jnp.einsum('bqhe,bkhe->bhqk', qblk, k,
                       preferred_element_type=jnp.float32) * (HEAD_DIM ** -0.5)
        causal = (kpos[None, :] < qpos[:, None])[None, None]
        log_beta = jax.nn.log_sigmoid(z)
        log_rest = jnp.where(causal, jax.nn.log_sigmoid(-z), 0.0)
        after = lax.cumsum(log_rest, axis=3, reverse=True) - log_rest
        a = jnp.where(causal, jnp.exp(log_beta + after), 0.0)
        return jnp.einsum('bhqk,bkhe->bqhe', a, v32)
    out = lax.map(block, (qb, jnp.arange(nbq)))
    return out.swapaxes(0, 1).reshape(B, S, H, Dh)


def hybrid_mixer(h, w_in, dil_out_norm, sb_out_norm, w_out):
    B, S, _ = h.shape
    proj = h @ w_in
    cuts = [DIL_WIDTH, 2 * DIL_WIDTH, 3 * DIL_WIDTH,
            3 * DIL_WIDTH + SB_WIDTH, 3 * DIL_WIDTH + 2 * SB_WIDTH]
    q_d, k_d, v_d, q_s, k_s, v_s = jnp.split(proj, cuts, axis=-1)
    q_d = q_d.reshape(B, S, N_HEADS_DIL, HEAD_DIM)
    k_d = k_d.reshape(B, S, N_HEADS_DIL, HEAD_DIM)
    v_d = v_d.reshape(B, S, N_HEADS_DIL, HEAD_DIM)
    q_s = q_s.reshape(B, S, N_HEADS_SB, HEAD_DIM)
    k_s = k_s.reshape(B, S, N_HEADS_SB, HEAD_DIM)
    v_s = v_s.reshape(B, S, N_HEADS_SB, HEAD_DIM)
    pos = jnp.arange(S)
    q_d = rope(q_d, pos)
    k_d = rope(k_d, pos)
    outs, lses = [], []
    for window, dilation in DIL_PATTERNS:
        o, l = dilated_window_attention(q_d, k_d, v_d, window, dilation)
        outs.append(o)
        lses.append(l)
    wts = jax.nn.softmax(jnp.stack(lses, axis=0), axis=0)
    o_d = jnp.sum(wts[..., None] * jnp.stack(outs, axis=0), axis=0)
    o_s = stick_breaking_attention(q_s, k_s, v_s)
    o_d = rms_norm(o_d.reshape(B, S, DIL_WIDTH).astype(h.dtype), dil_out_norm)
    o_s = rms_norm(o_s.reshape(B, S, SB_WIDTH).astype(h.dtype), sb_out_norm)
    return jnp.concatenate([o_d, o_s], axis=-1) @ w_out


def moe_ffn(h, w_router, b_router, w_gate, b_gate, w_up, b_up, w_down, b_down):
    B, S, D = h.shape
    n_tok = B * S
    hf = h.reshape(n_tok, D)
    logits = (hf @ w_router).astype(jnp.float32) + b_router.astype(jnp.float32)
    top_vals, top_idx = lax.top_k(logits, TOP_K)
    gates = jax.nn.softmax(top_vals, axis=-1)
    n_assign = n_tok * TOP_K
    flat_e = top_idx.reshape(n_assign)
    flat_tok = jnp.arange(n_assign, dtype=jnp.int32) // TOP_K
    flat_gate = gates.reshape(n_assign)
    onehot = (flat_e[:, None] == jnp.arange(N_EXPERTS)[None, :]).astype(jnp.int32)
    rank = jnp.take_along_axis(jnp.cumsum(onehot, axis=0), flat_e[:, None], axis=1)[:, 0] - 1
    counts = jnp.sum(onehot, axis=0)
    padded = (counts + EXPERT_BLOCK - 1) // EXPERT_BLOCK * EXPERT_BLOCK
    pend = jnp.cumsum(padded)
    pstart = pend - padded
    dest = pstart[flat_e] + rank
    n_blk = -(-n_assign // EXPERT_BLOCK) + N_EXPERTS
    n_rows = n_blk * EXPERT_BLOCK
    row_tok = jnp.full((n_rows,), n_tok, dtype=jnp.int32).at[dest].set(flat_tok)
    row_gate = jnp.zeros((n_rows,), dtype=jnp.float32).at[dest].set(flat_gate)
    blk_expert = jnp.minimum(
        jnp.searchsorted(pend, jnp.arange(n_blk, dtype=jnp.int32) * EXPERT_BLOCK, side='right'),
        N_EXPERTS - 1)
    xs = jnp.concatenate([hf, jnp.zeros((1, D), hf.dtype)], axis=0)[row_tok]
    xs = xs.reshape(n_blk, EXPERT_BLOCK, D)
    def expert_block(args):
        xb, e = args
        g = xb @ w_gate[e] + b_gate[e]
        u = xb @ w_up[e] + b_up[e]
        g = jnp.minimum(g, SWIGLU_LIMIT)
        u = jnp.clip(u, -SWIGLU_LIMIT, SWIGLU_LIMIT)
        a = (u + 1.0) * (g * jax.nn.sigmoid(SWIGLU_ALPHA * g))
        return a @ w_down[e] + b_down[e]
    ys = lax.map(expert_block, (xs, blk_expert)).reshape(n_rows, D)
    y = jax.ops.segment_sum(ys * row_gate[:, None].astype(ys.dtype), row_tok,
                            num_segments=n_tok + 1)[:n_tok]
    return y.reshape(B, S, D)


def setup_inputs(seed: int = 0) -> dict:
    key = jax.random.key(seed)
    ks = jax.random.split(key, 17)
    f32 = jnp.float32
    def nrm(k, shape, scale):
        return jax.random.normal(k, shape, f32) * scale
    return {
        "x": nrm(ks[0], (BATCH, SEQ, D_MODEL), 1.0),
        "attn_norm": 1.0 + nrm(ks[1], (DEPTH, D_MODEL), 0.1),
        "w_in": nrm(ks[2], (DEPTH, D_MODEL, IN_WIDTH), D_MODEL ** -0.5),
        "dil_out_norm": 1.0 + nrm(ks[3], (DEPTH, DIL_WIDTH), 0.1),
        "sb_out_norm": 1.0 + nrm(ks[4], (DEPTH, SB_WIDTH), 0.1),
        "w_out": nrm(ks[5], (DEPTH, MIX_WIDTH, D_MODEL), MIX_WIDTH ** -0.5),
        "ffn_norm": 1.0 + nrm(ks[6], (DEPTH, D_MODEL), 0.1),
        "w_router": nrm(ks[7], (DEPTH, D_MODEL, N_EXPERTS), D_MODEL ** -0.5),
        "b_router": nrm(ks[8], (DEPTH, N_EXPERTS), 0.01),
        "w_gate": nrm(ks[9], (DEPTH, N_EXPERTS, D_MODEL, D_EXPERT), D_MODEL ** -0.5),
        "b_gate": nrm(ks[10], (DEPTH, N_EXPERTS, D_EXPERT), 0.01),
        "w_up": nrm(ks[11], (DEPTH, N_EXPERTS, D_MODEL, D_EXPERT), D_MODEL ** -0.5),
        "b_up": nrm(ks[12], (DEPTH, N_EXPERTS, D_EXPERT), 0.01),
        "w_down": nrm(ks[13], (DEPTH, N_EXPERTS, D_EXPERT, D_MODEL), D_EXPERT ** -0.5),
        "b_down": nrm(ks[14], (DEPTH, N_EXPERTS, D_MODEL), 0.01),
        "final_norm": 1.0 + nrm(ks[15], (D_MODEL,), 0.1),
    }


def reference(x, attn_norm, w_in, dil_out_norm, sb_out_norm, w_out, ffn_norm,
              w_router, b_router, w_gate, b_gate, w_up, b_up, w_down, b_down,
              final_norm):
    for l in range(DEPTH):
        h = rms_norm(x, attn_norm[l])
        x = x + hybrid_mixer(h, w_in[l], dil_out_norm[l], sb_out_norm[l], w_out[l])
        h = rms_norm(x, ffn_norm[l])
        x = x + moe_ffn(h, w_router[l], b_router[l], w_gate[l], b_gate[l],
                        w_up[l], b_up[l], w_down[l], b_down[l])
    return rms_norm(x, final_norm)
```

```python
import functools

import jax
import jax.numpy as jnp
from jax import lax
from jax.experimental import pallas as pl
from jax.experimental.pallas import tpu as pltpu

F32 = jnp.float32
BF16 = jnp.bfloat16

HEAD_DIM = 64
LANES = 128
BLOCK = 128
DIL_PATTERNS = ((128, 1), (512, 4), (2048, 16))
ROPE_THETA = 10000.0
N_EXPERTS = 32
TOP_K = 4
SWIGLU_LIMIT = 7.0
SWIGLU_ALPHA = 1.702
NORM_EPS = 1e-5
ROW_ALIGN = 128
ROW_BLOCK = 256
ITEM_ROWS = 1280
VMEM_LIMIT = 56 * 1024 * 1024


def _cparams(sem):
    return pltpu.CompilerParams(dimension_semantics=sem, vmem_limit_bytes=VMEM_LIMIT)


def _inproj_kernel(x_ref, g_ref, w_ref, cos_ref, sin_ref, o_ref, h_scr, *, tn, half_width):
    j = pl.program_id(1)

    @pl.when(j == 0)
    def _():
        x = x_ref[...]
        y = x * lax.rsqrt(jnp.mean(x * x, axis=-1, keepdims=True) + NORM_EPS)
        h_scr[...] = (y * g_ref[...]).astype(BF16)

    acc = jnp.dot(h_scr[...], w_ref[...].astype(BF16), preferred_element_type=F32)
    section = (j * tn) // half_width
    scale = jnp.where((section == 0) | (section == 3), HEAD_DIM ** -0.5, 1.0).astype(F32)

    @pl.when(section < 2)
    def _():
        lane = lax.broadcasted_iota(jnp.int32, (acc.shape[0], LANES), 1)
        first_half = (lane % HEAD_DIM) < (HEAD_DIM // 2)
        c = cos_ref[...]
        s = sin_ref[...]
        for g in range(tn // LANES):
            xg = acc[:, g * LANES:(g + 1) * LANES]
            partner = jnp.where(first_half,
                                pltpu.roll(xg, LANES - HEAD_DIM // 2, 1),
                                pltpu.roll(xg, HEAD_DIM // 2, 1))
            o_ref[:, g * LANES:(g + 1) * LANES] = ((xg * c + partner * s) * scale).astype(BF16)

    @pl.when(section >= 2)
    def _():
        o_ref[...] = (acc * scale).astype(BF16)


def _inproj(x2d, g, w_in, cos_t, sin_t, *, seq, tm, tn):
    m, d = x2d.shape
    n = w_in.shape[1]
    half_width = d // 2
    assert half_width % tn == 0 and tn % LANES == 0 and m % tm == 0 and seq % tm == 0
    pos_tiles = seq // tm
    return pl.pallas_call(
        functools.partial(_inproj_kernel, tn=tn, half_width=half_width),
        grid=(m // tm, n // tn),
        in_specs=[
            pl.BlockSpec((tm, d), lambda i, j: (i, 0)),
            pl.BlockSpec((1, d), lambda i, j: (0, 0)),
            pl.BlockSpec((d, tn), lambda i, j: (0, j)),
            pl.BlockSpec((tm, LANES), lambda i, j: (i % pos_tiles, 0)),
            pl.BlockSpec((tm, LANES), lambda i, j: (i % pos_tiles, 0)),
        ],
        out_specs=pl.BlockSpec((tm, tn), lambda i, j: (i, j)),
        out_shape=jax.ShapeDtypeStruct((m, n), BF16),
        scratch_shapes=[pltpu.VMEM((tm, d), BF16)],
        compiler_params=_cparams(("arbitrary", "arbitrary")),
        name="inproj",
    )(x2d, g, w_in, cos_t, sin_t)


def _rope_tables(seq):
    half = HEAD_DIM // 2
    inv_freq = 1.0 / (ROPE_THETA ** (jnp.arange(half, dtype=F32) * 2.0 / HEAD_DIM))
    ang = jnp.arange(seq, dtype=F32)[:, None] * inv_freq[None, :]
    cos = jnp.cos(ang)
    sin = jnp.sin(ang)
    reps = LANES // HEAD_DIM
    cos_t = jnp.tile(jnp.concatenate([cos, cos], axis=1), (1, reps))
    sin_t = jnp.tile(jnp.concatenate([-sin, sin], axis=1), (1, reps))
    return cos_t, sin_t


def _dil_kernel(q_ref, k_ref, v_ref, o_ref, f32_scr, qp, kp, vp, operm, lperm, onat, lnat, *, seq):
    n_blocks = seq // BLOCK
    lane_k = lax.broadcasted_iota(jnp.int32, (2 * BLOCK, LANES), 1)
    head_a_k = lane_k < HEAD_DIM
    lane_q = lax.broadcasted_iota(jnp.int32, (BLOCK, LANES), 1)
    head_a_q = lane_q < HEAD_DIM
    qi = lax.broadcasted_iota(jnp.int32, (BLOCK, 2 * BLOCK), 0)
    ki = lax.broadcasted_iota(jnp.int32, (BLOCK, 2 * BLOCK), 1)
    dist = qi + BLOCK - ki
    band = (dist >= 0) & (dist <= BLOCK)
    cur_only = ki >= BLOCK
    zero_b = jnp.zeros((), BF16)

    f32_scr[0] = q_ref[0].astype(F32)
    f32_scr[1] = k_ref[0].astype(F32)
    f32_scr[2] = v_ref[0].astype(F32)
    kp[0:BLOCK, :] = jnp.zeros((BLOCK, LANES), BF16)
    vp[0:BLOCK, :] = jnp.zeros((BLOCK, LANES), BF16)

    for p, (window, dil) in enumerate(DIL_PATTERNS):
        assert window // dil == BLOCK
        length = seq // dil
        per_stream = length // BLOCK
        assert length % BLOCK == 0
        if dil == 1:
            qp[...] = q_ref[0]
            kp[BLOCK:, :] = k_ref[0]
            vp[BLOCK:, :] = v_ref[0]
        else:
            for r in range(dil):
                rows = pl.ds(r, length, stride=dil)
                qp[r * length:(r + 1) * length, :] = f32_scr[0, rows, :].astype(BF16)
                kp[BLOCK + r * length:BLOCK + (r + 1) * length, :] = f32_scr[1, rows, :].astype(BF16)
                vp[BLOCK + r * length:BLOCK + (r + 1) * length, :] = f32_scr[2, rows, :].astype(BF16)
        o_dst = onat.at[p] if dil == 1 else operm
        l_dst = lnat.at[p] if dil == 1 else lperm

        def block_body(bi, carry, per_stream=per_stream, o_dst=o_dst, l_dst=l_dst):
            row0 = pl.multiple_of(bi * BLOCK, BLOCK)
            qb = qp[pl.ds(row0, BLOCK), :]
            kw = kp[pl.ds(row0, 2 * BLOCK), :]
            vw = vp[pl.ds(row0, 2 * BLOCK), :]
            has_prev = (bi % per_stream) != 0
            valid = band & (cur_only | has_prev)
            kk = jnp.concatenate([jnp.where(head_a_k, kw, zero_b), jnp.where(head_a_k, zero_b, kw)], axis=0)
            s2 = lax.dot_general(qb, kk, (((1,), (1,)), ((), ())), preferred_element_type=F32)
            ps, ms, dens = [], [], []
            for h in range(2):
                s = jnp.where(valid, s2[:, h * 2 * BLOCK:(h + 1) * 2 * BLOCK], -jnp.inf)
                m = jnp.max(s, axis=-1, keepdims=True)
                e = jnp.exp(s - m)
                ps.append(e.astype(BF16))
                ms.append(m)
                dens.append(jnp.sum(e, axis=-1, keepdims=True))
            vv = jnp.concatenate([jnp.where(head_a_k, vw, zero_b), jnp.where(head_a_k, zero_b, vw)], axis=0)
            o = jnp.dot(jnp.concatenate(ps, axis=1), vv, preferred_element_type=F32)
            den = jnp.where(head_a_q, dens[0], dens[1])
            o_dst[pl.ds(row0, BLOCK), :] = o / den
            l_dst[pl.ds(row0, BLOCK), :] = jnp.where(head_a_q, ms[0], ms[1]) + jnp.log(den)
            return carry

        lax.fori_loop(0, n_blocks, block_body, 0)
        if dil != 1:
            for r in range(dil):
                rows = pl.ds(r, length, stride=dil)
                onat[p, rows, :] = operm[r * length:(r + 1) * length, :]
                lnat[p, rows, :] = lperm[r * length:(r + 1) * length, :]

    def mix_body(c, carry):
        rows = pl.ds(pl.multiple_of(c * BLOCK, BLOCK), BLOCK)
        ls = [lnat[p, rows, :] for p in range(len(DIL_PATTERNS))]
        m = functools.reduce(jnp.maximum, ls)
        es = [jnp.exp(l - m) for l in ls]
        num = functools.reduce(lambda a, b: a + b, [e * onat[p, rows, :] for p, e in enumerate(es)])
        o_ref[0, rows, :] = num / functools.reduce(lambda a, b: a + b, es)
        return carry

    lax.fori_loop(0, n_blocks, mix_body, 0)


def _dilated_attention(proj3, *, half_width):
    b, seq, _ = proj3.shape
    pairs = half_width // LANES
    n_pat = len(DIL_PATTERNS)
    spec = lambda off: pl.BlockSpec((1, seq, LANES), lambda bi, c, off=off: (bi, 0, off + c))
    return pl.pallas_call(
        functools.partial(_dil_kernel, seq=seq),
        grid=(b, pairs),
        in_specs=[spec(0), spec(pairs), spec(2 * pairs)],
        out_specs=pl.BlockSpec((1, seq, LANES), lambda bi, c: (bi, 0, c)),
        out_shape=jax.ShapeDtypeStruct((b, seq, half_width), F32),
        scratch_shapes=[
            pltpu.VMEM((3, seq, LANES), F32),
            pltpu.VMEM((seq, LANES), BF16),
            pltpu.VMEM((seq + BLOCK, LANES), BF16),
            pltpu.VMEM((seq + BLOCK, LANES), BF16),
            pltpu.VMEM((seq, LANES), F32),
            pltpu.VMEM((seq, LANES), F32),
            pltpu.VMEM((n_pat, seq, LANES), F32),
            pltpu.VMEM((n_pat, seq, LANES), F32),
        ],
        compiler_params=_cparams(("arbitrary", "arbitrary")),
        name="dilated_attention",
    )(proj3, proj3, proj3)


def _sb_kernel(q_ref, k_ref, v_ref, o_ref, kk_scr, vv_scr, tri_scr, acc_scr, carry_scr, *, seq):
    n_blocks = seq // BLOCK
    lane_k = lax.broadcasted_iota(jnp.int32, (BLOCK, LANES), 1)
    head_a = lane_k < HEAD_DIM
    zero_b = jnp.zeros((), BF16)

    def prep(jb, carry):
        rows = pl.ds(pl.multiple_of(jb * BLOCK, BLOCK), BLOCK)
        kb = k_ref[0, rows, :]
        vb = v_ref[0, rows, :]
        kk_scr[jb, 0:BLOCK, :] = jnp.where(head_a, kb, zero_b)
        kk_scr[jb, BLOCK:, :] = jnp.where(head_a, zero_b, kb)
        vv_scr[jb, 0:BLOCK, :] = jnp.where(head_a, vb, zero_b)
        vv_scr[jb, BLOCK:, :] = jnp.where(head_a, zero_b, vb)
        return carry

    lax.fori_loop(0, n_blocks, prep, 0)

    jj = lax.broadcasted_iota(jnp.int32, (2 * BLOCK, 2 * BLOCK), 0) % BLOCK
    ss = lax.broadcasted_iota(jnp.int32, (2 * BLOCK, 2 * BLOCK), 1)
    tri_scr[...] = jnp.where((ss >= BLOCK) | (jj > ss), 1.0, 0.0).astype(BF16)

    row = lax.broadcasted_iota(jnp.int32, (BLOCK, 2 * BLOCK), 0)
    col = lax.broadcasted_iota(jnp.int32, (BLOCK, 2 * BLOCK), 1) % BLOCK
    causal = col < row

    def key_block(qb, jb, diag):
        z = lax.dot_general(qb, kk_scr[jb], (((1,), (1,)), ((), ())), preferred_element_type=F32)
        ls = jnp.minimum(z, 0.0) - jnp.log1p(jnp.exp(-jnp.abs(z)))
        lr = ls - z
        if diag:
            lr = jnp.where(causal, lr, 0.0)
        hi = lr.astype(BF16)
        lo = (lr - hi.astype(F32)).astype(BF16)
        tri = tri_scr[...]
        a_heads = []
        for h in range(2):
            sl = slice(h * BLOCK, (h + 1) * BLOCK)
            lhs = jnp.concatenate([hi[:, sl], lo[:, sl]], axis=1)
            cs_tot = jnp.dot(lhs, tri, preferred_element_type=F32)
            carry = carry_scr[:, sl]
            a = jnp.exp(ls[:, sl] + cs_tot[:, :BLOCK] + carry)
            if diag:
                a = jnp.where(causal[:, :BLOCK], a, 0.0)
            carry_scr[:, sl] = carry + cs_tot[:, BLOCK:]
            a_heads.append(a.astype(BF16))
        acc_scr[...] += jnp.dot(jnp.concatenate(a_heads, axis=1), vv_scr[jb], preferred_element_type=F32)

    def q_block(i, carry):
        rows = pl.ds(pl.multiple_of(i * BLOCK, BLOCK), BLOCK)
        qb = q_ref[0, rows, :]
        acc_scr[...] = jnp.zeros_like(acc_scr)
        carry_scr[...] = jnp.zeros_like(carry_scr)
        key_block(qb, i, True)

        def inner(t, c):
            key_block(qb, i - 1 - t, False)
            return c

        lax.fori_loop(0, i, inner, 0)
        o_ref[0, rows, :] = acc_scr[...]
        return carry

    lax.fori_loop(0, n_blocks, q_block, 0)


def _stickbreak_attention(proj3, *, half_width):
    b, seq, _ = proj3.shape
    pairs = half_width // LANES
    n_blocks = seq // BLOCK
    spec = lambda off: pl.BlockSpec((1, seq, LANES), lambda bi, c, off=off: (bi, 0, off + c))
    return pl.pallas_call(
        functools.partial(_sb_kernel, seq=seq),
        grid=(b, pairs),
        in_specs=[spec(3 * pairs), spec(4 * pairs), spec(5 * pairs)],
        out_specs=pl.BlockSpec((1, seq, LANES), lambda bi, c: (bi, 0, c)),
        out_shape=jax.ShapeDtypeStruct((b, seq, half_width), F32),
        scratch_shapes=[
            pltpu.VMEM((n_blocks, 2 * BLOCK, LANES), BF16),
            pltpu.VMEM((n_blocks, 2 * BLOCK, LANES), BF16),
            pltpu.VMEM((2 * BLOCK, 2 * BLOCK), BF16),
            pltpu.VMEM((BLOCK, LANES), F32),
            pltpu.VMEM((BLOCK, 2 * BLOCK), F32),
        ],
        compiler_params=_cparams(("arbitrary", "arbitrary")),
        name="stickbreak_attention",
    )(proj3, proj3, proj3)


def _split_bf16(x):
    hi = x.astype(BF16)
    return hi, (x - hi.astype(F32)).astype(BF16)


def _outproj_kernel(od_ref, os_ref, x_ref, w_ref, gd_ref, gs_ref, gf_ref, wr_ref, br_ref,
                    x2_ref, h2_ref, idx_ref, gate_ref, acc_d, acc_s, ssq_d, ssq_s, *, nk, half_width):
    k = pl.program_id(1)

    @pl.when(k == 0)
    def _():
        acc_d[...] = jnp.zeros_like(acc_d)
        acc_s[...] = jnp.zeros_like(acc_s)
        ssq_d[...] = jnp.zeros_like(ssq_d)
        ssq_s[...] = jnp.zeros_like(ssq_s)

    w = w_ref[...].astype(BF16)

    def accumulate(a_ref, g_ref, acc, ssq):
        a = a_ref[...]
        ssq[...] += jnp.sum(a * a, axis=-1, keepdims=True)
        acc[...] += jnp.dot((a * g_ref[...]).astype(BF16), w, preferred_element_type=F32)

    @pl.when(k < nk // 2)
    def _():
        accumulate(od_ref, gd_ref, acc_d, ssq_d)

    @pl.when(k >= nk // 2)
    def _():
        accumulate(os_ref, gs_ref, acc_s, ssq_s)

    @pl.when(k == nk - 1)
    def _():
        r_d = lax.rsqrt(ssq_d[...] / half_width + NORM_EPS)
        r_s = lax.rsqrt(ssq_s[...] / half_width + NORM_EPS)
        x2 = x_ref[...] + (acc_d[...] * r_d + acc_s[...] * r_s)
        x2_ref[...] = x2
        h2 = (x2 * lax.rsqrt(jnp.mean(x2 * x2, axis=-1, keepdims=True) + NORM_EPS)) * gf_ref[...]
        h2_ref[...] = h2
        h_hi, h_lo = _split_bf16(h2)
        w_hi, w_lo = _split_bf16(wr_ref[...])
        logits = (jnp.dot(h_hi, w_hi, preferred_element_type=F32)
                  + jnp.dot(h_hi, w_lo, preferred_element_type=F32)
                  + jnp.dot(h_lo, w_hi, preferred_element_type=F32)) + br_ref[...]
        tm = logits.shape[0]
        e_iota = lax.broadcasted_iota(jnp.int32, logits.shape, 1)
        out_lane = lax.broadcasted_iota(jnp.int32, (tm, LANES), 1)
        idx_out = jnp.zeros((tm, LANES), jnp.int32)
        val_out = jnp.full((tm, LANES), -jnp.inf, F32)
        top0 = None
        for kk in range(TOP_K):
            m = jnp.max(logits, axis=-1, keepdims=True)
            sel = jnp.min(jnp.where(logits == m, e_iota, N_EXPERTS), axis=-1, keepdims=True)
            idx_out = jnp.where(out_lane == kk, sel, idx_out)
            val_out = jnp.where(out_lane == kk, m, val_out)
            logits = jnp.where(e_iota == sel, -jnp.inf, logits)
            if kk == 0:
                top0 = m
        ex = jnp.exp(val_out - top0)
        gate_ref[...] = ex / jnp.sum(ex, axis=-1, keepdims=True)
        idx_ref[...] = idx_out


def _outproj_router(o_d, o_s, x2d, w_out, g_d, g_s, g_f, w_router, b_router, *, tm, tk):
    m, d = x2d.shape
    half_width = d // 2
    nk = d // tk
    assert half_width % tk == 0 and m % tm == 0
    kh = nk // 2
    return pl.pallas_call(
        functools.partial(_outproj_kernel, nk=nk, half_width=half_width),
        grid=(m // tm, nk),
        in_specs=[
            pl.BlockSpec((tm, tk), lambda i, k: (i, jnp.minimum(k, kh - 1))),
            pl.BlockSpec((tm, tk), lambda i, k: (i, jnp.maximum(k - kh, 0))),
            pl.BlockSpec((tm, d), lambda i, k: (i, 0)),
            pl.BlockSpec((tk, d), lambda i, k: (k, 0)),
            pl.BlockSpec((1, tk), lambda i, k: (0, jnp.minimum(k, kh - 1))),
            pl.BlockSpec((1, tk), lambda i, k: (0, jnp.maximum(k - kh, 0))),
            pl.BlockSpec((1, d), lambda i, k: (0, 0)),
            pl.BlockSpec((d, N_EXPERTS), lambda i, k: (0, 0)),
            pl.BlockSpec((1, N_EXPERTS), lambda i, k: (0, 0)),
        ],
        out_specs=[
            pl.BlockSpec((tm, d), lambda i, k: (i, 0)),
            pl.BlockSpec((tm, d), lambda i, k: (i, 0)),
            pl.BlockSpec((tm, LANES), lambda i, k: (i, 0)),
            pl.BlockSpec((tm, LANES), lambda i, k: (i, 0)),
        ],
        out_shape=[
            jax.ShapeDtypeStruct((m, d), F32),
            jax.ShapeDtypeStruct((m, d), F32),
            jax.ShapeDtypeStruct((m, LANES), jnp.int32),
            jax.ShapeDtypeStruct((m, LANES), F32),
        ],
        scratch_shapes=[
            pltpu.VMEM((tm, d), F32),
            pltpu.VMEM((tm, d), F32),
            pltpu.VMEM((tm, 1), F32),
            pltpu.VMEM((tm, 1), F32),
        ],
        compiler_params=_cparams(("arbitrary", "arbitrary")),
        name="outproj_router",
    )(o_d, o_s, x2d, w_out, g_d, g_s, g_f, w_router, b_router)


def _expert_kernel(item_e, item_r0, item_nb, n_items, row_tok,
                   h_hbm, wg_ref, bg_ref, wu_ref, bu_ref, wd_ref, bd_ref,
                   ys_hbm, xf32, xb16, acc, zbuf, sem_in, sem_out, *, nj):
    i = pl.program_id(0)
    j = pl.program_id(1)
    active = i < n_items[0]
    r0 = item_r0[i]
    n_align = item_nb[i]
    n_rows = n_align * ROW_ALIGN
    n_mm = (n_rows + ROW_BLOCK - 1) // ROW_BLOCK

    def row_copy(r):
        tok = row_tok[r0 + r]
        return pltpu.make_async_copy(h_hbm.at[pl.ds(tok, 1)], xf32.at[pl.ds(r, 1)], sem_in)

    @pl.when((i == 0) & (j == 0))
    def _():
        xb16[...] = jnp.zeros_like(xb16)

    @pl.when(active & (j == 0))
    def _():
        def issue(r, c):
            row_copy(r).start()
            return c

        lax.fori_loop(0, n_rows, issue, 0)

        def wait(r, c):
            row_copy(r).wait()
            return c

        lax.fori_loop(0, n_rows, wait, 0)

        def cast(rb, c):
            rows = pl.ds(pl.multiple_of(rb * ROW_ALIGN, ROW_ALIGN), ROW_ALIGN)
            xb16[rows, :] = xf32[rows, :].astype(BF16)
            return c

        lax.fori_loop(0, n_align, cast, 0)

    @pl.when(active)
    def _():
        wg = wg_ref[0].astype(BF16)
        wu = wu_ref[0].astype(BF16)
        wd = wd_ref[0].astype(BF16)
        bg = bg_ref[0]
        bu = bu_ref[0]

        def mm(rb, c):
            rows = pl.ds(pl.multiple_of(rb * ROW_BLOCK, ROW_BLOCK), ROW_BLOCK)
            xb = xb16[rows, :]
            g = jnp.dot(xb, wg, preferred_element_type=F32) + bg
            u = jnp.dot(xb, wu, preferred_element_type=F32) + bu
            g = jnp.minimum(g, SWIGLU_LIMIT)
            u = jnp.clip(u, -SWIGLU_LIMIT, SWIGLU_LIMIT)
            a = (u + 1.0) * (g * jax.nn.sigmoid(SWIGLU_ALPHA * g))
            y = jnp.dot(a.astype(BF16), wd, preferred_element_type=F32)

            @pl.when(j == 0)
            def _():
                acc[rows, :] = y + bd_ref[0]

            @pl.when(j > 0)
            def _():
                acc[rows, :] += y

            return c

        lax.fori_loop(0, n_mm, mm, 0)

    @pl.when(active & (j == nj - 1))
    def _():
        def out_copy(rb):
            src = acc.at[pl.ds(pl.multiple_of(rb * ROW_ALIGN, ROW_ALIGN), ROW_ALIGN)]
            dst = ys_hbm.at[pl.ds(pl.multiple_of(r0 + rb * ROW_ALIGN, ROW_ALIGN), ROW_ALIGN)]
            return pltpu.make_async_copy(src, dst, sem_out)

        def issue(rb, c):
            out_copy(rb).start()
            return c

        lax.fori_loop(0, n_align, issue, 0)

        def wait(rb, c):
            out_copy(rb).wait()
            return c

        lax.fori_loop(0, n_align, wait, 0)

    @pl.when((i == pl.num_programs(0) - 1) & (j == nj - 1))
    def _():
        zbuf[...] = jnp.zeros_like(zbuf)
        used = n_items[1]

        def tail_copy(t):
            dst = ys_hbm.at[pl.ds(pl.multiple_of((used + t) * ROW_ALIGN, ROW_ALIGN), ROW_ALIGN)]
            return pltpu.make_async_copy(zbuf, dst, sem_out)

        def issue(t, c):
            tail_copy(t).start()
            return c

        n_tail = ys_hbm.shape[0] // ROW_ALIGN - used
        lax.fori_loop(0, n_tail, issue, 0)

        def wait(t, c):
            tail_copy(t).wait()
            return c

        lax.fori_loop(0, n_tail, wait, 0)


def _experts(h2, w_gate, b_gate, w_up, b_up, w_down, b_down, meta, *, n_rows, th):
    item_e, item_r0, item_nb, n_items, row_tok = meta
    n_exp, d, dh = w_gate.shape
    nj = dh // th
    n_item_slots = item_e.shape[0]

    def w_col(i, j, item_e, item_r0, item_nb, n_items, row_tok):
        return (item_e[i], 0, jnp.where(i < n_items[0], j, nj - 1))

    def w_row(i, j, item_e, item_r0, item_nb, n_items, row_tok):
        return (item_e[i], jnp.where(i < n_items[0], j, nj - 1), 0)

    def b_all(i, j, item_e, item_r0, item_nb, n_items, row_tok):
        return (item_e[i], 0, 0)

    grid_spec = pltpu.PrefetchScalarGridSpec(
        num_scalar_prefetch=5,
        grid=(n_item_slots, nj),
        in_specs=[
            pl.BlockSpec(memory_space=pl.ANY),
            pl.BlockSpec((1, d, th), w_col),
            pl.BlockSpec((1, 1, th), w_col),
            pl.BlockSpec((1, d, th), w_col),
            pl.BlockSpec((1, 1, th), w_col),
            pl.BlockSpec((1, th, d), w_row),
            pl.BlockSpec((1, 1, d), b_all),
        ],
        out_specs=pl.BlockSpec(memory_space=pl.ANY),
        scratch_shapes=[
            pltpu.VMEM((ITEM_ROWS, d), F32),
            pltpu.VMEM((ITEM_ROWS, d), BF16),
            pltpu.VMEM((ITEM_ROWS, d), F32),
            pltpu.VMEM((ROW_ALIGN, d), F32),
            pltpu.SemaphoreType.DMA,
            pltpu.SemaphoreType.DMA,
        ],
    )
    return pl.pallas_call(
        functools.partial(_expert_kernel, nj=nj),
        grid_spec=grid_spec,
        out_shape=jax.ShapeDtypeStruct((n_rows, d), F32),
        compiler_params=_cparams(("arbitrary", "arbitrary")),
        name="experts",
    )(item_e, item_r0, item_nb, n_items, row_tok,
      h2, w_gate, b_gate.reshape(n_exp, 1, dh), w_up, b_up.reshape(n_exp, 1, dh), w_down,
      b_down.reshape(n_exp, 1, d))


def _routing_metadata(top_idx, n_tok):
    n_assign = n_tok * TOP_K
    flat_e = top_idx.reshape(n_assign)
    onehot = (flat_e[:, None] == jnp.arange(N_EXPERTS, dtype=jnp.int32)[None, :]).astype(jnp.int32)
    rank = jnp.take_along_axis(jnp.cumsum(onehot, axis=0), flat_e[:, None], axis=1)[:, 0] - 1
    counts = jnp.sum(onehot, axis=0)
    padded = (counts + ROW_ALIGN - 1) // ROW_ALIGN * ROW_ALIGN
    pend = jnp.cumsum(padded)
    pstart = pend - padded
    dest = (pstart[flat_e] + rank).astype(jnp.int32)
    n_rows = (n_assign // ROW_ALIGN + N_EXPERTS) * ROW_ALIGN
    flat_tok = jnp.arange(n_assign, dtype=jnp.int32) // TOP_K
    row_tok = jnp.zeros((n_rows,), jnp.int32).at[dest].set(flat_tok)
    n_chunks = (padded + ITEM_ROWS - 1) // ITEM_ROWS
    cend = jnp.cumsum(n_chunks)
    cstart = cend - n_chunks
    n_slots = N_EXPERTS + n_assign // ITEM_ROWS + 1
    slot = jnp.arange(n_slots, dtype=jnp.int32)
    n_items = cend[-1]
    last_e = jnp.max(jnp.where(n_chunks > 0, jnp.arange(N_EXPERTS), 0))
    e_of = jnp.minimum(jnp.searchsorted(cend, slot, side='right'), N_EXPERTS - 1)
    live = slot < n_items
    item_e = jnp.where(live, e_of, last_e).astype(jnp.int32)
    chunk = slot - cstart[e_of]
    item_r0 = jnp.where(live, pstart[e_of] + chunk * ITEM_ROWS, 0).astype(jnp.int32)
    item_rows = jnp.clip(padded[e_of] - chunk * ITEM_ROWS, 0, ITEM_ROWS)
    item_nb = jnp.where(live, item_rows // ROW_ALIGN, 0).astype(jnp.int32)
    counts_blk = jnp.stack([n_items, pend[-1] // ROW_ALIGN]).astype(jnp.int32)
    meta = (item_e, item_r0, item_nb, counts_blk, row_tok)
    return meta, dest, n_rows


def _combine_kernel(dest, ys_hbm, x2_ref, gate_ref, g_ref, o_ref, buf, sem, *, tm):
    i = pl.program_id(0)

    def row_copy(t, kk):
        r = dest[(i * tm + t) * TOP_K + kk]
        return pltpu.make_async_copy(ys_hbm.at[pl.ds(r, 1)], buf.at[kk, pl.ds(t, 1)], sem)

    def issue(t, c):
        for kk in range(TOP_K):
            row_copy(t, kk).start()
        return c

    lax.fori_loop(0, tm, issue, 0)

    def wait(t, c):
        for kk in range(TOP_K):
            row_copy(t, kk).wait()
        return c

    lax.fori_loop(0, tm, wait, 0)

    gates = gate_ref[...]
    y = x2_ref[...]
    for kk in range(TOP_K):
        y = y + gates[:, kk:kk + 1] * buf[kk]
    o_ref[...] = (y * lax.rsqrt(jnp.mean(y * y, axis=-1, keepdims=True) + NORM_EPS)) * g_ref[...]


def _combine(dest, ys, x2, gates, g_final, *, tm):
    m, d = x2.shape
    grid_spec = pltpu.PrefetchScalarGridSpec(
        num_scalar_prefetch=1,
        grid=(m // tm,),
        in_specs=[
            pl.BlockSpec(memory_space=pl.ANY),
            pl.BlockSpec((tm, d), lambda i, dest: (i, 0)),
            pl.BlockSpec((tm, LANES), lambda i, dest: (i, 0)),
            pl.BlockSpec((1, d), lambda i, dest: (0, 0)),
        ],
        out_specs=pl.BlockSpec((tm, d), lambda i, dest: (i, 0)),
        scratch_shapes=[pltpu.VMEM((TOP_K, tm, d), F32), pltpu.SemaphoreType.DMA],
    )
    return pl.pallas_call(
        functools.partial(_combine_kernel, tm=tm),
        grid_spec=grid_spec,
        out_shape=jax.ShapeDtypeStruct((m, d), F32),
        compiler_params=_cparams(("arbitrary",)),
        name="combine",
    )(dest, ys, x2, gates, g_final)


def _layer(x, attn_norm, w_in, dil_out_norm, sb_out_norm, w_out, ffn_norm,
           w_router, b_router, w_gate, b_gate, w_up, b_up, w_down, b_down, *, tiles):
    b, seq, d = x.shape
    m = b * seq
    half_width = d // 2
    x2d = x.reshape(m, d)
    cos_t, sin_t = _rope_tables(seq)
    proj = _inproj(x2d, attn_norm.reshape(1, d), w_in, cos_t, sin_t,
                   seq=seq, tm=tiles["in_tm"], tn=tiles["in_tn"])
    proj3 = proj.reshape(b, seq, 3 * d)
    o_d = _dilated_attention(proj3, half_width=half_width).reshape(m, half_width)
    o_s = _stickbreak_attention(proj3, half_width=half_width).reshape(m, half_width)
    x2, h2, idx_l, gate_l = _outproj_router(
        o_d, o_s, x2d, w_out, dil_out_norm.reshape(1, half_width), sb_out_norm.reshape(1, half_width),
        ffn_norm.reshape(1, d), w_router, b_router.reshape(1, N_EXPERTS),
        tm=tiles["out_tm"], tk=tiles["out_tk"])
    meta, dest, n_rows = _routing_metadata(idx_l[:, :TOP_K], m)
    ys = _experts(h2, w_gate, b_gate, w_up, b_up, w_down, b_down, meta, n_rows=n_rows, th=tiles["moe_th"])
    return x2, ys, dest, gate_l


_TILES = dict(in_tm=1024, in_tn=512, out_tm=512, out_tk=512, moe_th=256, cmb_tm=128)


def kernel(x, attn_norm, w_in, dil_out_norm, sb_out_norm, w_out, ffn_norm, w_router, b_router,
           w_gate, b_gate, w_up, b_up, w_down, b_down, final_norm):
    depth = attn_norm.shape[0]
    b, seq, d = x.shape
    for l in range(depth):
        x2, ys, dest, gate_l = _layer(
            x, attn_norm[l], w_in[l], dil_out_norm[l], sb_out_norm[l], w_out[l], ffn_norm[l],
            w_router[l], b_router[l], w_gate[l], b_gate[l], w_up[l], b_up[l], w_down[l], b_down[l],
            tiles=_TILES)
        assert depth == 1
        x = _combine(dest, ys, x2, gate_l, final_norm.reshape(1, d), tm=_TILES["cmb_tm"])
    return x.reshape(b, seq, d)
```

```python
import functools
import math

import jax
import jax.numpy as jnp
from jax import lax
from jax.experimental import pallas as pl
from jax.experimental.pallas import tpu as pltpu

F32 = jnp.float32
BF16 = jnp.bfloat16

HEAD_DIM = 64
LANES = 128
BLOCK = 128
DIL_PATTERNS = ((128, 1), (512, 4), (2048, 16))
ROPE_THETA = 10000.0
N_EXPERTS = 32
TOP_K = 4
SWIGLU_LIMIT = 7.0
SWIGLU_ALPHA = 1.702
NORM_EPS = 1e-5
Q_SCALE = HEAD_DIM ** -0.5 * math.log2(math.e)
DIL_UNROLL = 4
ROW_ALIGN = 128
ROW_BLOCK = 256
ITEM_ROWS = 1280
VMEM_LIMIT = 56 * 1024 * 1024


def _cparams(sem):
    return pltpu.CompilerParams(dimension_semantics=sem, vmem_limit_bytes=VMEM_LIMIT)


def _inproj_kernel(x_ref, g_ref, w_ref, cos_ref, sin_ref, o_ref, h_scr, *, tn, half_width):
    j = pl.program_id(1)

    @pl.when(j == 0)
    def _():
        x = x_ref[...]
        y = x * lax.rsqrt(jnp.mean(x * x, axis=-1, keepdims=True) + NORM_EPS)
        h_scr[...] = (y * g_ref[...]).astype(BF16)

    acc = jnp.dot(h_scr[...], w_ref[...].astype(BF16), preferred_element_type=F32)
    section = (j * tn) // half_width
    scale = jnp.where((section == 0) | (section == 3), Q_SCALE, 1.0).astype(F32)

    @pl.when(section < 2)
    def _():
        lane = lax.broadcasted_iota(jnp.int32, (acc.shape[0], LANES), 1)
        first_half = (lane % HEAD_DIM) < (HEAD_DIM // 2)
        c = cos_ref[...]
        s = sin_ref[...]
        for g in range(tn // LANES):
            xg = acc[:, g * LANES:(g + 1) * LANES]
            partner = jnp.where(first_half,
                                pltpu.roll(xg, LANES - HEAD_DIM // 2, 1),
                                pltpu.roll(xg, HEAD_DIM // 2, 1))
            o_ref[:, g * LANES:(g + 1) * LANES] = ((xg * c + partner * s) * scale).astype(BF16)

    @pl.when(section >= 2)
    def _():
        o_ref[...] = (acc * scale).astype(BF16)


def _inproj(x2d, g, w_in, cos_t, sin_t, *, seq, tm, tn):
    m, d = x2d.shape
    n = w_in.shape[1]
    half_width = d // 2
    assert half_width % tn == 0 and tn % LANES == 0 and m % tm == 0 and seq % tm == 0
    pos_tiles = seq // tm
    return pl.pallas_call(
        functools.partial(_inproj_kernel, tn=tn, half_width=half_width),
        grid=(m // tm, n // tn),
        in_specs=[
            pl.BlockSpec((tm, d), lambda i, j: (i, 0)),
            pl.BlockSpec((1, d), lambda i, j: (0, 0)),
            pl.BlockSpec((d, tn), lambda i, j: (0, j)),
            pl.BlockSpec((tm, LANES), lambda i, j: (i % pos_tiles, 0)),
            pl.BlockSpec((tm, LANES), lambda i, j: (i % pos_tiles, 0)),
        ],
        out_specs=pl.BlockSpec((tm, tn), lambda i, j: (i, j)),
        out_shape=jax.ShapeDtypeStruct((m, n), BF16),
        scratch_shapes=[pltpu.VMEM((tm, d), BF16)],
        compiler_params=_cparams(("arbitrary", "arbitrary")),
        name="inproj",
    )(x2d, g, w_in, cos_t, sin_t)


def _rope_tables(seq):
    half = HEAD_DIM // 2
    inv_freq = 1.0 / (ROPE_THETA ** (jnp.arange(half, dtype=F32) * 2.0 / HEAD_DIM))
    ang = jnp.arange(seq, dtype=F32)[:, None] * inv_freq[None, :]
    cos = jnp.cos(ang)
    sin = jnp.sin(ang)
    reps = LANES // HEAD_DIM
    cos_t = jnp.tile(jnp.concatenate([cos, cos], axis=1), (1, reps))
    sin_t = jnp.tile(jnp.concatenate([-sin, sin], axis=1), (1, reps))
    return cos_t, sin_t


def _dil_kernel(q_ref, k_ref, v_ref, o_ref, f32_scr, qp, kp, vp, operm, lperm, onat, lnat, *, seq):
    n_blocks = seq // BLOCK
    lane_k = lax.broadcasted_iota(jnp.int32, (2 * BLOCK, LANES), 1)
    head_a_k = lane_k < HEAD_DIM
    lane_q = lax.broadcasted_iota(jnp.int32, (BLOCK, LANES), 1)
    head_a_q = lane_q < HEAD_DIM
    qi = lax.broadcasted_iota(jnp.int32, (BLOCK, 2 * BLOCK), 0)
    ki = lax.broadcasted_iota(jnp.int32, (BLOCK, 2 * BLOCK), 1)
    dist = qi + BLOCK - ki
    band = (dist >= 0) & (dist <= BLOCK)
    cur_only = ki >= BLOCK
    zero_b = jnp.zeros((), BF16)

    f32_scr[0] = q_ref[0].astype(F32)
    f32_scr[1] = k_ref[0].astype(F32)
    f32_scr[2] = v_ref[0].astype(F32)
    kp[0:BLOCK, :] = jnp.zeros((BLOCK, LANES), BF16)
    vp[0:BLOCK, :] = jnp.zeros((BLOCK, LANES), BF16)

    for p, (window, dil) in enumerate(DIL_PATTERNS):
        assert window // dil == BLOCK
        length = seq // dil
        per_stream = length // BLOCK
        assert length % BLOCK == 0
        if dil == 1:
            qp[...] = q_ref[0]
            kp[BLOCK:, :] = k_ref[0]
            vp[BLOCK:, :] = v_ref[0]
        else:
            for r in range(dil):
                rows = pl.ds(r, length, stride=dil)
                qp[r * length:(r + 1) * length, :] = f32_scr[0, rows, :].astype(BF16)
                kp[BLOCK + r * length:BLOCK + (r + 1) * length, :] = f32_scr[1, rows, :].astype(BF16)
                vp[BLOCK + r * length:BLOCK + (r + 1) * length, :] = f32_scr[2, rows, :].astype(BF16)
        o_dst = onat.at[p] if dil == 1 else operm
        l_dst = lnat.at[p] if dil == 1 else lperm

        def block_body(bi, carry, per_stream=per_stream, o_dst=o_dst, l_dst=l_dst):
            row0 = pl.multiple_of(bi * BLOCK, BLOCK)
            qb = qp[pl.ds(row0, BLOCK), :]
            kw = kp[pl.ds(row0, 2 * BLOCK), :]
            vw = vp[pl.ds(row0, 2 * BLOCK), :]
            has_prev = (bi % per_stream) != 0
            valid = band & (cur_only | has_prev)
            kk = jnp.concatenate([jnp.where(head_a_k, kw, zero_b), jnp.where(head_a_k, zero_b, kw)], axis=0)
            s2 = lax.dot_general(qb, kk, (((1,), (1,)), ((), ())), preferred_element_type=F32)
            ps, ms, dens = [], [], []
            for h in range(2):
                s = jnp.where(valid, s2[:, h * 2 * BLOCK:(h + 1) * 2 * BLOCK], -jnp.inf)
                m = jnp.max(s, axis=-1, keepdims=True)
                e = jnp.exp2(s - m)
                ps.append(e.astype(BF16))
                ms.append(m)
                dens.append(jnp.sum(e, axis=-1, keepdims=True))
            vv = jnp.concatenate([jnp.where(head_a_k, vw, zero_b), jnp.where(head_a_k, zero_b, vw)], axis=0)
            o = jnp.dot(jnp.concatenate(ps, axis=1), vv, preferred_element_type=F32)
            den = jnp.where(head_a_q, dens[0], dens[1])
            o_dst[pl.ds(row0, BLOCK), :] = o / den
            l_dst[pl.ds(row0, BLOCK), :] = jnp.where(head_a_q, ms[0], ms[1]) + jnp.log2(den)
            return carry

        lax.fori_loop(0, n_blocks, block_body, 0, unroll=DIL_UNROLL)
        if dil != 1:
            for r in range(dil):
                rows = pl.ds(r, length, stride=dil)
                onat[p, rows, :] = operm[r * length:(r + 1) * length, :]
                lnat[p, rows, :] = lperm[r * length:(r + 1) * length, :]

    def mix_body(c, carry):
        rows = pl.ds(pl.multiple_of(c * BLOCK, BLOCK), BLOCK)
        ls = [lnat[p, rows, :] for p in range(len(DIL_PATTERNS))]
        m = functools.reduce(jnp.maximum, ls)
        es = [jnp.exp2(l - m) for l in ls]
        num = functools.reduce(lambda a, b: a + b, [e * onat[p, rows, :] for p, e in enumerate(es)])
        o_ref[0, rows, :] = num / functools.reduce(lambda a, b: a + b, es)
        return carry

    lax.fori_loop(0, n_blocks, mix_body, 0)


def _dilated_attention(proj3, *, half_width):
    b, seq, _ = proj3.shape
    pairs = half_width // LANES
    n_pat = len(DIL_PATTERNS)
    spec = lambda off: pl.BlockSpec((1, seq, LANES), lambda bi, c, off=off: (bi, 0, off + c))
    return pl.pallas_call(
        functools.partial(_dil_kernel, seq=seq),
        grid=(b, pairs),
        in_specs=[spec(0), spec(pairs), spec(2 * pairs)],
        out_specs=pl.BlockSpec((1, seq, LANES), lambda bi, c: (bi, 0, c)),
        out_shape=jax.ShapeDtypeStruct((b, seq, half_width), F32),
        scratch_shapes=[
            pltpu.VMEM((3, seq, LANES), F32),
            pltpu.VMEM((seq, LANES), BF16),
            pltpu.VMEM((seq + BLOCK, LANES), BF16),
            pltpu.VMEM((seq + BLOCK, LANES), BF16),
            pltpu.VMEM((seq, LANES), F32),
            pltpu.VMEM((seq, LANES), F32),
            pltpu.VMEM((n_pat, seq, LANES), F32),
            pltpu.VMEM((n_pat, seq, LANES), F32),
        ],
        compiler_params=_cparams(("arbitrary", "arbitrary")),
        name="dilated_attention",
    )(proj3, proj3, proj3)


def _sb_kernel(q_ref, k_ref, v_ref, o_ref, kk_scr, vv_scr, tri_scr, acc_scr, carry_scr, *, seq, qt):
    n_blocks = seq // BLOCK
    per_tile = qt // BLOCK
    lane_k = lax.broadcasted_iota(jnp.int32, (BLOCK, LANES), 1)
    head_a = lane_k < HEAD_DIM
    zero_b = jnp.zeros((), BF16)

    def prep(jb, carry):
        rows = pl.ds(pl.multiple_of(jb * BLOCK, BLOCK), BLOCK)
        kb = k_ref[0, rows, :]
        vb = v_ref[0, rows, :]
        kk_scr[jb, 0:BLOCK, :] = jnp.where(head_a, kb, zero_b)
        kk_scr[jb, BLOCK:, :] = jnp.where(head_a, zero_b, kb)
        vv_scr[jb, 0:BLOCK, :] = jnp.where(head_a, vb, zero_b)
        vv_scr[jb, BLOCK:, :] = jnp.where(head_a, zero_b, vb)
        return carry

    lax.fori_loop(0, n_blocks, prep, 0)

    jj = lax.broadcasted_iota(jnp.int32, (2 * BLOCK, 2 * BLOCK), 0) % BLOCK
    ss = lax.broadcasted_iota(jnp.int32, (2 * BLOCK, 2 * BLOCK), 1)
    tri_scr[...] = jnp.where((ss >= BLOCK) | (jj > ss), 1.0, 0.0).astype(BF16)

    row = lax.broadcasted_iota(jnp.int32, (qt, 2 * BLOCK), 0)
    col = lax.broadcasted_iota(jnp.int32, (qt, 2 * BLOCK), 1) % BLOCK

    def key_block(qb, jb, key_off):
        z = lax.dot_general(qb, kk_scr[jb], (((1,), (1,)), ((), ())), preferred_element_type=F32)
        neg_abs = lax.bitcast_convert_type(lax.bitcast_convert_type(z, jnp.uint32) | jnp.uint32(0x80000000), F32)
        ls = jnp.minimum(z, 0.0) - jnp.log2(1.0 + jnp.exp2(neg_abs))
        lr = ls - z
        if key_off is not None:
            causal = (col + key_off) < row
            lr = jnp.where(causal, lr, 0.0)
        hi = lr.astype(BF16)
        lo = (lr - hi.astype(F32)).astype(BF16)
        tri = tri_scr[...]
        a_heads = []
        for h in range(2):
            sl = slice(h * BLOCK, (h + 1) * BLOCK)
            lhs = jnp.concatenate([hi[:, sl], lo[:, sl]], axis=1)
            cs_tot = jnp.dot(lhs, tri, preferred_element_type=F32)
            carry = carry_scr[:, sl]
            a = jnp.exp2(ls[:, sl] + cs_tot[:, :BLOCK] + carry)
            if key_off is not None:
                a = jnp.where(causal[:, :BLOCK], a, 0.0)
            carry_scr[:, sl] = carry + cs_tot[:, BLOCK:]
            a_heads.append(a.astype(BF16))
        acc_scr[...] += jnp.dot(jnp.concatenate(a_heads, axis=1), vv_scr[jb], preferred_element_type=F32)

    def q_tile(i, carry):
        rows = pl.ds(pl.multiple_of(i * qt, qt), qt)
        qb = q_ref[0, rows, :]
        acc_scr[...] = jnp.zeros_like(acc_scr)
        carry_scr[...] = jnp.zeros_like(carry_scr)
        for t in reversed(range(per_tile)):
            key_block(qb, i * per_tile + t, t * BLOCK)

        def inner(t, c):
            key_block(qb, i * per_tile - 1 - t, None)
            return c

        lax.fori_loop(0, i * per_tile, inner, 0)
        o_ref[0, rows, :] = acc_scr[...]
        return carry

    lax.fori_loop(0, seq // qt, q_tile, 0)


def _stickbreak_attention(proj3, *, half_width, qt):
    b, seq, _ = proj3.shape
    pairs = half_width // LANES
    n_blocks = seq // BLOCK
    assert seq % qt == 0 and qt % BLOCK == 0
    spec = lambda off: pl.BlockSpec((1, seq, LANES), lambda bi, c, off=off: (bi, 0, off + c))
    return pl.pallas_call(
        functools.partial(_sb_kernel, seq=seq, qt=qt),
        grid=(b, pairs),
        in_specs=[spec(3 * pairs), spec(4 * pairs), spec(5 * pairs)],
        out_specs=pl.BlockSpec((1, seq, LANES), lambda bi, c: (bi, 0, c)),
        out_shape=jax.ShapeDtypeStruct((b, seq, half_width), F32),
        scratch_shapes=[
            pltpu.VMEM((n_blocks, 2 * BLOCK, LANES), BF16),
            pltpu.VMEM((n_blocks, 2 * BLOCK, LANES), BF16),
            pltpu.VMEM((2 * BLOCK, 2 * BLOCK), BF16),
            pltpu.VMEM((qt, LANES), F32),
            pltpu.VMEM((qt, 2 * BLOCK), F32),
        ],
        compiler_params=_cparams(("arbitrary", "arbitrary")),
        name="stickbreak_attention",
    )(proj3, proj3, proj3)


def _split_bf16(x):
    hi = x.astype(BF16)
    return hi, (x - hi.astype(F32)).astype(BF16)


def _outproj_kernel(od_ref, os_ref, x_ref, w_ref, gd_ref, gs_ref, gf_ref, wr_ref, br_ref,
                    x2_ref, h2_ref, idx_ref, gate_ref, acc_d, acc_s, ssq_d, ssq_s, *, nk, half_width):
    k = pl.program_id(1)

    @pl.when(k == 0)
    def _():
        acc_d[...] = jnp.zeros_like(acc_d)
        acc_s[...] = jnp.zeros_like(acc_s)
        ssq_d[...] = jnp.zeros_like(ssq_d)
        ssq_s[...] = jnp.zeros_like(ssq_s)

    w = w_ref[...].astype(BF16)

    def accumulate(a_ref, g_ref, acc, ssq):
        a = a_ref[...]
        ssq[...] += jnp.sum(a * a, axis=-1, keepdims=True)
        acc[...] += jnp.dot((a * g_ref[...]).astype(BF16), w, preferred_element_type=F32)

    @pl.when(k < nk // 2)
    def _():
        accumulate(od_ref, gd_ref, acc_d, ssq_d)

    @pl.when(k >= nk // 2)
    def _():
        accumulate(os_ref, gs_ref, acc_s, ssq_s)

    @pl.when(k == nk - 1)
    def _():
        r_d = lax.rsqrt(ssq_d[...] / half_width + NORM_EPS)
        r_s = lax.rsqrt(ssq_s[...] / half_width + NORM_EPS)
        x2 = x_ref[...] + (acc_d[...] * r_d + acc_s[...] * r_s)
        x2_ref[...] = x2
        h2 = (x2 * lax.rsqrt(jnp.mean(x2 * x2, axis=-1, keepdims=True) + NORM_EPS)) * gf_ref[...]
        h2_ref[...] = h2
        h_hi, h_lo = _split_bf16(h2)
        w_hi, w_lo = _split_bf16(wr_ref[...])
        logits = (jnp.dot(h_hi, w_hi, preferred_element_type=F32)
                  + jnp.dot(h_hi, w_lo, preferred_element_type=F32)
                  + jnp.dot(h_lo, w_hi, preferred_element_type=F32)) + br_ref[...]
        tm = logits.shape[0]
        e_iota = lax.broadcasted_iota(jnp.int32, logits.shape, 1)
        out_lane = lax.broadcasted_iota(jnp.int32, (tm, LANES), 1)
        idx_out = jnp.zeros((tm, LANES), jnp.int32)
        val_out = jnp.full((tm, LANES), -jnp.inf, F32)
        top0 = None
        for kk in range(TOP_K):
            m = jnp.max(logits, axis=-1, keepdims=True)
            sel = jnp.min(jnp.where(logits == m, e_iota, N_EXPERTS), axis=-1, keepdims=True)
            idx_out = jnp.where(out_lane == kk, sel, idx_out)
            val_out = jnp.where(out_lane == kk, m, val_out)
            logits = jnp.where(e_iota == sel, -jnp.inf, logits)
            if kk == 0:
                top0 = m
        ex = jnp.exp(val_out - top0)
        gate_ref[...] = ex / jnp.sum(ex, axis=-1, keepdims=True)
        idx_ref[...] = idx_out


def _outproj_router(o_d, o_s, x2d, w_out, g_d, g_s, g_f, w_router, b_router, *, tm, tk):
    m, d = x2d.shape
    half_width = d // 2
    nk = d // tk
    assert half_width % tk == 0 and m % tm == 0
    kh = nk // 2
    return pl.pallas_call(
        functools.partial(_outproj_kernel, nk=nk, half_width=half_width),
        grid=(m // tm, nk),
        in_specs=[
            pl.BlockSpec((tm, tk), lambda i, k: (i, jnp.minimum(k, kh - 1))),
            pl.BlockSpec((tm, tk), lambda i, k: (i, jnp.maximum(k - kh, 0))),
            pl.BlockSpec((tm, d), lambda i, k: (i, 0)),
            pl.BlockSpec((tk, d), lambda i, k: (k, 0)),
            pl.BlockSpec((1, tk), lambda i, k: (0, jnp.minimum(k, kh - 1))),
            pl.BlockSpec((1, tk), lambda i, k: (0, jnp.maximum(k - kh, 0))),
            pl.BlockSpec((1, d), lambda i, k: (0, 0)),
            pl.BlockSpec((d, N_EXPERTS), lambda i, k: (0, 0)),
            pl.BlockSpec((1, N_EXPERTS), lambda i, k: (0, 0)),
        ],
        out_specs=[
            pl.BlockSpec((tm, d), lambda i, k: (i, 0)),
            pl.BlockSpec((tm, d), lambda i, k: (i, 0)),
            pl.BlockSpec((tm, LANES), lambda i, k: (i, 0)),
            pl.BlockSpec((tm, LANES), lambda i, k: (i, 0)),
        ],
        out_shape=[
            jax.ShapeDtypeStruct((m, d), F32),
            jax.ShapeDtypeStruct((m, d), F32),
            jax.ShapeDtypeStruct((m, LANES), jnp.int32),
            jax.ShapeDtypeStruct((m, LANES), F32),
        ],
        scratch_shapes=[
            pltpu.VMEM((tm, d), F32),
            pltpu.VMEM((tm, d), F32),
            pltpu.VMEM((tm, 1), F32),
            pltpu.VMEM((tm, 1), F32),
        ],
        compiler_params=_cparams(("arbitrary", "arbitrary")),
        name="outproj_router",
    )(o_d, o_s, x2d, w_out, g_d, g_s, g_f, w_router, b_router)


def _expert_kernel(item_e, item_r0, item_nb, n_items, row_tok,
                   h_hbm, wg_ref, bg_ref, wu_ref, bu_ref, wd_ref, bd_ref,
                   ys_hbm, xf32, xb16, acc, zbuf, sem_in, sem_out, *, nj):
    i = pl.program_id(0)
    j = pl.program_id(1)
    active = i < n_items[0]
    r0 = item_r0[i]
    n_align = item_nb[i]
    n_rows = n_align * ROW_ALIGN
    n_mm = (n_rows + ROW_BLOCK - 1) // ROW_BLOCK

    def row_copy(r):
        tok = row_tok[r0 + r]
        return pltpu.make_async_copy(h_hbm.at[pl.ds(tok, 1)], xf32.at[pl.ds(r, 1)], sem_in)

    @pl.when((i == 0) & (j == 0))
    def _():
        xb16[...] = jnp.zeros_like(xb16)

    @pl.when(active & (j == 0))
    def _():
        def issue(r, c):
            row_copy(r).start()
            return c

        lax.fori_loop(0, n_rows, issue, 0)

        def wait(r, c):
            row_copy(r).wait()
            return c

        lax.fori_loop(0, n_rows, wait, 0)

        def cast(rb, c):
            rows = pl.ds(pl.multiple_of(rb * ROW_ALIGN, ROW_ALIGN), ROW_ALIGN)
            xb16[rows, :] = xf32[rows, :].astype(BF16)
            return c

        lax.fori_loop(0, n_align, cast, 0)

    @pl.when(active)
    def _():
        wg = wg_ref[0].astype(BF16)
        wu = wu_ref[0].astype(BF16)
        wd = wd_ref[0].astype(BF16)
        bg = bg_ref[0]
        bu = bu_ref[0]

        def mm(rb, c):
            rows = pl.ds(pl.multiple_of(rb * ROW_BLOCK, ROW_BLOCK), ROW_BLOCK)
            xb = xb16[rows, :]
            g = jnp.dot(xb, wg, preferred_element_type=F32) + bg
            u = jnp.dot(xb, wu, preferred_element_type=F32) + bu
            g = jnp.minimum(g, SWIGLU_LIMIT)
            u = jnp.clip(u, -SWIGLU_LIMIT, SWIGLU_LIMIT)
            a = (u + 1.0) * (g * jax.nn.sigmoid(SWIGLU_ALPHA * g))
            y = jnp.dot(a.astype(BF16), wd, preferred_element_type=F32)

            @pl.when(j == 0)
            def _():
                acc[rows, :] = y + bd_ref[0]

            @pl.when(j > 0)
            def _():
                acc[rows, :] += y

            return c

        lax.fori_loop(0, n_mm, mm, 0)

    @pl.when(active & (j == nj - 1))
    def _():
        def out_copy(rb):
            src = acc.at[pl.ds(pl.multiple_of(rb * ROW_ALIGN, ROW_ALIGN), ROW_ALIGN)]
            dst = ys_hbm.at[pl.ds(pl.multiple_of(r0 + rb * ROW_ALIGN, ROW_ALIGN), ROW_ALIGN)]
            return pltpu.make_async_copy(src, dst, sem_out)

        def issue(rb, c):
            out_copy(rb).start()
            return c

        lax.fori_loop(0, n_align, issue, 0)

        def wait(rb, c):
            out_copy(rb).wait()
            return c

        lax.fori_loop(0, n_align, wait, 0)

    @pl.when((i == pl.num_programs(0) - 1) & (j == nj - 1))
    def _():
        zbuf[...] = jnp.zeros_like(zbuf)
        used = n_items[1]

        def tail_copy(t):
            dst = ys_hbm.at[pl.ds(pl.multiple_of((used + t) * ROW_ALIGN, ROW_ALIGN), ROW_ALIGN)]
            return pltpu.make_async_copy(zbuf, dst, sem_out)

        def issue(t, c):
            tail_copy(t).start()
            return c

        n_tail = ys_hbm.shape[0] // ROW_ALIGN - used
        lax.fori_loop(0, n_tail, issue, 0)

        def wait(t, c):
            tail_copy(t).wait()
            return c

        lax.fori_loop(0, n_tail, wait, 0)


def _experts(h2, w_gate, b_gate, w_up, b_up, w_down, b_down, meta, *, n_rows, th):
    item_e, item_r0, item_nb, n_items, row_tok = meta
    n_exp, d, dh = w_gate.shape
    nj = dh // th
    n_item_slots = item_e.shape[0]

    def w_col(i, j, item_e, item_r0, item_nb, n_items, row_tok):
        return (item_e[i], 0, jnp.where(i < n_items[0], j, nj - 1))

    def w_row(i, j, item_e, item_r0, item_nb, n_items, row_tok):
        return (item_e[i], jnp.where(i < n_items[0], j, nj - 1), 0)

    def b_all(i, j, item_e, item_r0, item_nb, n_items, row_tok):
        return (item_e[i], 0, 0)

    grid_spec = pltpu.PrefetchScalarGridSpec(
        num_scalar_prefetch=5,
        grid=(n_item_slots, nj),
        in_specs=[
            pl.BlockSpec(memory_space=pl.ANY),
            pl.BlockSpec((1, d, th), w_col),
            pl.BlockSpec((1, 1, th), w_col),
            pl.BlockSpec((1, d, th), w_col),
            pl.BlockSpec((1, 1, th), w_col),
            pl.BlockSpec((1, th, d), w_row),
            pl.BlockSpec((1, 1, d), b_all),
        ],
        out_specs=pl.BlockSpec(memory_space=pl.ANY),
        scratch_shapes=[
            pltpu.VMEM((ITEM_ROWS, d), F32),
            pltpu.VMEM((ITEM_ROWS, d), BF16),
            pltpu.VMEM((ITEM_ROWS, d), F32),
            pltpu.VMEM((ROW_ALIGN, d), F32),
            pltpu.SemaphoreType.DMA,
            pltpu.SemaphoreType.DMA,
        ],
    )
    return pl.pallas_call(
        functools.partial(_expert_kernel, nj=nj),
        grid_spec=grid_spec,
        out_shape=jax.ShapeDtypeStruct((n_rows, d), F32),
        compiler_params=_cparams(("arbitrary", "arbitrary")),
        name="experts",
    )(item_e, item_r0, item_nb, n_items, row_tok,
      h2, w_gate, b_gate.reshape(n_exp, 1, dh), w_up, b_up.reshape(n_exp, 1, dh), w_down,
      b_down.reshape(n_exp, 1, d))


def _routing_metadata(top_idx, n_tok):
    n_assign = n_tok * TOP_K
    flat_e = top_idx.reshape(n_assign)
    onehot = (flat_e[:, None] == jnp.arange(N_EXPERTS, dtype=jnp.int32)[None, :]).astype(jnp.int32)
    rank = jnp.take_along_axis(jnp.cumsum(onehot, axis=0), flat_e[:, None], axis=1)[:, 0] - 1
    counts = jnp.sum(onehot, axis=0)
    padded = (counts + ROW_ALIGN - 1) // ROW_ALIGN * ROW_ALIGN
    pend = jnp.cumsum(padded)
    pstart = pend - padded
    dest = (pstart[flat_e] + rank).astype(jnp.int32)
    n_rows = (n_assign // ROW_ALIGN + N_EXPERTS) * ROW_ALIGN
    flat_tok = jnp.arange(n_assign, dtype=jnp.int32) // TOP_K
    row_tok = jnp.zeros((n_rows,), jnp.int32).at[dest].set(flat_tok)
    n_chunks = (padded + ITEM_ROWS - 1) // ITEM_ROWS
    cend = jnp.cumsum(n_chunks)
    cstart = cend - n_chunks
    n_slots = N_EXPERTS + n_assign // ITEM_ROWS + 1
    slot = jnp.arange(n_slots, dtype=jnp.int32)
    n_items = cend[-1]
    last_e = jnp.max(jnp.where(n_chunks > 0, jnp.arange(N_EXPERTS), 0))
    e_of = jnp.minimum(jnp.searchsorted(cend, slot, side='right'), N_EXPERTS - 1)
    live = slot < n_items
    item_e = jnp.where(live, e_of, last_e).astype(jnp.int32)
    chunk = slot - cstart[e_of]
    item_r0 = jnp.where(live, pstart[e_of] + chunk * ITEM_ROWS, 0).astype(jnp.int32)
    item_rows = jnp.clip(padded[e_of] - chunk * ITEM_ROWS, 0, ITEM_ROWS)
    item_nb = jnp.where(live, item_rows // ROW_ALIGN, 0).astype(jnp.int32)
    counts_blk = jnp.stack([n_items, pend[-1] // ROW_ALIGN]).astype(jnp.int32)
    meta = (item_e, item_r0, item_nb, counts_blk, row_tok)
    return meta, dest, n_rows


def _combine_kernel(dest, ys_hbm, x2_ref, gate_ref, g_ref, o_ref, buf, sem, *, tm):
    i = pl.program_id(0)

    def row_copy(t, kk):
        r = dest[(i * tm + t) * TOP_K + kk]
        return pltpu.make_async_copy(ys_hbm.at[pl.ds(r, 1)], buf.at[kk, pl.ds(t, 1)], sem)

    def issue(t, c):
        for kk in range(TOP_K):
            row_copy(t, kk).start()
        return c

    lax.fori_loop(0, tm, issue, 0)

    def wait(t, c):
        for kk in range(TOP_K):
            row_copy(t, kk).wait()
        return c

    lax.fori_loop(0, tm, wait, 0)

    gates = gate_ref[...]
    y = x2_ref[...]
    for kk in range(TOP_K):
        y = y + gates[:, kk:kk + 1] * buf[kk]
    o_ref[...] = (y * lax.rsqrt(jnp.mean(y * y, axis=-1, keepdims=True) + NORM_EPS)) * g_ref[...]


def _combine(dest, ys, x2, gates, g_final, *, tm):
    m, d = x2.shape
    grid_spec = pltpu.PrefetchScalarGridSpec(
        num_scalar_prefetch=1,
        grid=(m // tm,),
        in_specs=[
            pl.BlockSpec(memory_space=pl.ANY),
            pl.BlockSpec((tm, d), lambda i, dest: (i, 0)),
            pl.BlockSpec((tm, LANES), lambda i, dest: (i, 0)),
            pl.BlockSpec((1, d), lambda i, dest: (0, 0)),
        ],
        out_specs=pl.BlockSpec((tm, d), lambda i, dest: (i, 0)),
        scratch_shapes=[pltpu.VMEM((TOP_K, tm, d), F32), pltpu.SemaphoreType.DMA],
    )
    return pl.pallas_call(
        functools.partial(_combine_kernel, tm=tm),
        grid_spec=grid_spec,
        out_shape=jax.ShapeDtypeStruct((m, d), F32),
        compiler_params=_cparams(("arbitrary",)),
        name="combine",
    )(dest, ys, x2, gates, g_final)


def _layer(x, attn_norm, w_in, dil_out_norm, sb_out_norm, w_out, ffn_norm,
           w_router, b_router, w_gate, b_gate, w_up, b_up, w_down, b_down, *, tiles):
    b, seq, d = x.shape
    m = b * seq
    half_width = d // 2
    x2d = x.reshape(m, d)
    cos_t, sin_t = _rope_tables(seq)
    proj = _inproj(x2d, attn_norm.reshape(1, d), w_in, cos_t, sin_t,
                   seq=seq, tm=tiles["in_tm"], tn=tiles["in_tn"])
    proj3 = proj.reshape(b, seq, 3 * d)
    o_d = _dilated_attention(proj3, half_width=half_width).reshape(m, half_width)
    o_s = _stickbreak_attention(proj3, half_width=half_width, qt=tiles["sb_qt"]).reshape(m, half_width)
    x2, h2, idx_l, gate_l = _outproj_router(
        o_d, o_s, x2d, w_out, dil_out_norm.reshape(1, half_width), sb_out_norm.reshape(1, half_width),
        ffn_norm.reshape(1, d), w_router, b_router.reshape(1, N_EXPERTS),
        tm=tiles["out_tm"], tk=tiles["out_tk"])
    meta, dest, n_rows = _routing_metadata(idx_l[:, :TOP_K], m)
    ys = _experts(h2, w_gate, b_gate, w_up, b_up, w_down, b_down, meta, n_rows=n_rows, th=tiles["moe_th"])
    return x2, ys, dest, gate_l


_TILES = dict(in_tm=1024, in_tn=512, sb_qt=512, out_tm=512, out_tk=512, moe_th=256, cmb_tm=128)


def kernel(x, attn_norm, w_in, dil_out_norm, sb_out_norm, w_out, ffn_norm, w_router, b_router,
           w_gate, b_gate, w_up, b_up, w_down, b_down, final_norm):
    depth = attn_norm.shape[0]
    b, seq, d = x.shape
    for l in range(depth):
        x2, ys, dest, gate_l = _layer(
            x, attn_norm[l], w_in[l], dil_out_norm[l], sb_out_norm[l], w_out[l], ffn_norm[l],
            w_router[l], b_router[l], w_gate[l], b_gate[l], w_up[l], b_up[l], w_down[l], b_down[l],
            tiles=_TILES)
        assert depth == 1
        x = _combine(dest, ys, x2, gate_l, final_norm.reshape(1, d), tm=_TILES["cmb_tm"])
    return x.reshape(b, seq, d)
```

```python
import functools
import math

import jax
import jax.numpy as jnp
from jax import lax
from jax.experimental import pallas as pl
from jax.experimental.pallas import tpu as pltpu

F32 = jnp.float32
BF16 = jnp.bfloat16

HEAD_DIM = 64
LANES = 128
BLOCK = 128
DIL_PATTERNS = ((128, 1), (512, 4), (2048, 16))
ROPE_THETA = 10000.0
N_EXPERTS = 32
TOP_K = 4
SWIGLU_LIMIT = 7.0
SWIGLU_ALPHA = 1.702
NORM_EPS = 1e-5
Q_SCALE = HEAD_DIM ** -0.5 * math.log2(math.e)
GATHER_CHUNK = 128
DIL_UNROLL = 4
ROW_ALIGN = 128
ROW_BLOCK = 256
ITEM_ROWS = 1280
VMEM_LIMIT = 56 * 1024 * 1024


def _cparams(sem):
    return pltpu.CompilerParams(dimension_semantics=sem, vmem_limit_bytes=VMEM_LIMIT)


def _inproj_kernel(x_ref, g_ref, w_ref, cos_ref, sin_ref, o_ref, h_scr, *, tn, half_width):
    j = pl.program_id(1)

    @pl.when(j == 0)
    def _():
        x = x_ref[...]
        y = x * lax.rsqrt(jnp.mean(x * x, axis=-1, keepdims=True) + NORM_EPS)
        h_scr[...] = (y * g_ref[...]).astype(BF16)

    acc = jnp.dot(h_scr[...], w_ref[...].astype(BF16), preferred_element_type=F32)
    section = (j * tn) // half_width
    scale = jnp.where((section == 0) | (section == 3), Q_SCALE, 1.0).astype(F32)

    @pl.when(section < 2)
    def _():
        lane = lax.broadcasted_iota(jnp.int32, (acc.shape[0], LANES), 1)
        first_half = (lane % HEAD_DIM) < (HEAD_DIM // 2)
        c = cos_ref[...]
        s = sin_ref[...]
        for g in range(tn // LANES):
            xg = acc[:, g * LANES:(g + 1) * LANES]
            partner = jnp.where(first_half,
                                pltpu.roll(xg, LANES - HEAD_DIM // 2, 1),
                                pltpu.roll(xg, HEAD_DIM // 2, 1))
            o_ref[:, g * LANES:(g + 1) * LANES] = ((xg * c + partner * s) * scale).astype(BF16)

    @pl.when(section >= 2)
    def _():
        o_ref[...] = (acc * scale).astype(BF16)


def _inproj(x2d, g, w_in, cos_t, sin_t, *, seq, tm, tn):
    m, d = x2d.shape
    n = w_in.shape[1]
    half_width = d // 2
    assert half_width % tn == 0 and tn % LANES == 0 and m % tm == 0 and seq % tm == 0
    pos_tiles = seq // tm
    return pl.pallas_call(
        functools.partial(_inproj_kernel, tn=tn, half_width=half_width),
        grid=(m // tm, n // tn),
        in_specs=[
            pl.BlockSpec((tm, d), lambda i, j: (i, 0)),
            pl.BlockSpec((1, d), lambda i, j: (0, 0)),
            pl.BlockSpec((d, tn), lambda i, j: (0, j)),
            pl.BlockSpec((tm, LANES), lambda i, j: (i % pos_tiles, 0)),
            pl.BlockSpec((tm, LANES), lambda i, j: (i % pos_tiles, 0)),
        ],
        out_specs=pl.BlockSpec((tm, tn), lambda i, j: (i, j)),
        out_shape=jax.ShapeDtypeStruct((m, n), BF16),
        scratch_shapes=[pltpu.VMEM((tm, d), BF16)],
        compiler_params=_cparams(("arbitrary", "arbitrary")),
        name="inproj",
    )(x2d, g, w_in, cos_t, sin_t)


def _rope_tables(seq):
    half = HEAD_DIM // 2
    inv_freq = 1.0 / (ROPE_THETA ** (jnp.arange(half, dtype=F32) * 2.0 / HEAD_DIM))
    ang = jnp.arange(seq, dtype=F32)[:, None] * inv_freq[None, :]
    cos = jnp.cos(ang)
    sin = jnp.sin(ang)
    reps = LANES // HEAD_DIM
    cos_t = jnp.tile(jnp.concatenate([cos, cos], axis=1), (1, reps))
    sin_t = jnp.tile(jnp.concatenate([-sin, sin], axis=1), (1, reps))
    return cos_t, sin_t


def _dil_kernel(q_ref, k_ref, v_ref, o_ref, f32_scr, qp, kp, vp, operm, lperm, onat, lnat, *, seq):
    n_blocks = seq // BLOCK
    lane_k = lax.broadcasted_iota(jnp.int32, (2 * BLOCK, LANES), 1)
    head_a_k = lane_k < HEAD_DIM
    lane_q = lax.broadcasted_iota(jnp.int32, (BLOCK, LANES), 1)
    head_a_q = lane_q < HEAD_DIM
    qi = lax.broadcasted_iota(jnp.int32, (BLOCK, 2 * BLOCK), 0)
    ki = lax.broadcasted_iota(jnp.int32, (BLOCK, 2 * BLOCK), 1)
    dist = qi + BLOCK - ki
    band = (dist >= 0) & (dist <= BLOCK)
    cur_only = ki >= BLOCK
    zero_b = jnp.zeros((), BF16)

    f32_scr[0] = q_ref[0].astype(F32)
    f32_scr[1] = k_ref[0].astype(F32)
    f32_scr[2] = v_ref[0].astype(F32)
    kp[0:BLOCK, :] = jnp.zeros((BLOCK, LANES), BF16)
    vp[0:BLOCK, :] = jnp.zeros((BLOCK, LANES), BF16)

    for p, (window, dil) in enumerate(DIL_PATTERNS):
        assert window // dil == BLOCK
        length = seq // dil
        per_stream = length // BLOCK
        assert length % BLOCK == 0
        if dil == 1:
            qp[...] = q_ref[0]
            kp[BLOCK:, :] = k_ref[0]
            vp[BLOCK:, :] = v_ref[0]
        else:
            for r in range(dil):
                rows = pl.ds(r, length, stride=dil)
                qp[r * length:(r + 1) * length, :] = f32_scr[0, rows, :].astype(BF16)
                kp[BLOCK + r * length:BLOCK + (r + 1) * length, :] = f32_scr[1, rows, :].astype(BF16)
                vp[BLOCK + r * length:BLOCK + (r + 1) * length, :] = f32_scr[2, rows, :].astype(BF16)
        o_dst = onat.at[p] if dil == 1 else operm
        l_dst = lnat.at[p] if dil == 1 else lperm

        def block_body(bi, carry, per_stream=per_stream, o_dst=o_dst, l_dst=l_dst):
            row0 = pl.multiple_of(bi * BLOCK, BLOCK)
            qb = qp[pl.ds(row0, BLOCK), :]
            kw = kp[pl.ds(row0, 2 * BLOCK), :]
            vw = vp[pl.ds(row0, 2 * BLOCK), :]
            has_prev = (bi % per_stream) != 0
            valid = band & (cur_only | has_prev)
            kk = jnp.concatenate([jnp.where(head_a_k, kw, zero_b), jnp.where(head_a_k, zero_b, kw)], axis=0)
            s2 = lax.dot_general(qb, kk, (((1,), (1,)), ((), ())), preferred_element_type=F32)
            ps, ms, dens = [], [], []
            for h in range(2):
                s = jnp.where(valid, s2[:, h * 2 * BLOCK:(h + 1) * 2 * BLOCK], -jnp.inf)
                m = jnp.max(s, axis=-1, keepdims=True)
                e = jnp.exp2(s - m)
                ps.append(e.astype(BF16))
                ms.append(m)
                dens.append(jnp.sum(e, axis=-1, keepdims=True))
            vv = jnp.concatenate([jnp.where(head_a_k, vw, zero_b), jnp.where(head_a_k, zero_b, vw)], axis=0)
            o = jnp.dot(jnp.concatenate(ps, axis=1), vv, preferred_element_type=F32)
            den = jnp.where(head_a_q, dens[0], dens[1])
            o_dst[pl.ds(row0, BLOCK), :] = o / den
            l_dst[pl.ds(row0, BLOCK), :] = jnp.where(head_a_q, ms[0], ms[1]) + jnp.log2(den)
            return carry

        lax.fori_loop(0, n_blocks, block_body, 0, unroll=DIL_UNROLL)
        if dil != 1:
            for r in range(dil):
                rows = pl.ds(r, length, stride=dil)
                onat[p, rows, :] = operm[r * length:(r + 1) * length, :]
                lnat[p, rows, :] = lperm[r * length:(r + 1) * length, :]

    def mix_body(c, carry):
        rows = pl.ds(pl.multiple_of(c * BLOCK, BLOCK), BLOCK)
        ls = [lnat[p, rows, :] for p in range(len(DIL_PATTERNS))]
        m = functools.reduce(jnp.maximum, ls)
        es = [jnp.exp2(l - m) for l in ls]
        num = functools.reduce(lambda a, b: a + b, [e * onat[p, rows, :] for p, e in enumerate(es)])
        o_ref[0, rows, :] = num / functools.reduce(lambda a, b: a + b, es)
        return carry

    lax.fori_loop(0, n_blocks, mix_body, 0)


def _dilated_attention(proj3, *, half_width):
    b, seq, _ = proj3.shape
    pairs = half_width // LANES
    n_pat = len(DIL_PATTERNS)
    spec = lambda off: pl.BlockSpec((1, seq, LANES), lambda bi, c, off=off: (bi, 0, off + c))
    return pl.pallas_call(
        functools.partial(_dil_kernel, seq=seq),
        grid=(b, pairs),
        in_specs=[spec(0), spec(pairs), spec(2 * pairs)],
        out_specs=pl.BlockSpec((1, seq, LANES), lambda bi, c: (bi, 0, c)),
        out_shape=jax.ShapeDtypeStruct((b, seq, half_width), F32),
        scratch_shapes=[
            pltpu.VMEM((3, seq, LANES), F32),
            pltpu.VMEM((seq, LANES), BF16),
            pltpu.VMEM((seq + BLOCK, LANES), BF16),
            pltpu.VMEM((seq + BLOCK, LANES), BF16),
            pltpu.VMEM((seq, LANES), F32),
            pltpu.VMEM((seq, LANES), F32),
            pltpu.VMEM((n_pat, seq, LANES), F32),
            pltpu.VMEM((n_pat, seq, LANES), F32),
        ],
        compiler_params=_cparams(("arbitrary", "arbitrary")),
        name="dilated_attention",
    )(proj3, proj3, proj3)


def _sb_kernel(q_ref, k_ref, v_ref, o_ref, kk_scr, vv_scr, tri_scr, acc_scr, carry_scr, *, seq, qt):
    n_blocks = seq // BLOCK
    per_tile = qt // BLOCK
    lane_k = lax.broadcasted_iota(jnp.int32, (BLOCK, LANES), 1)
    head_a = lane_k < HEAD_DIM
    zero_b = jnp.zeros((), BF16)

    def prep(jb, carry):
        rows = pl.ds(pl.multiple_of(jb * BLOCK, BLOCK), BLOCK)
        kb = k_ref[0, rows, :]
        vb = v_ref[0, rows, :]
        kk_scr[jb, 0:BLOCK, :] = jnp.where(head_a, kb, zero_b)
        kk_scr[jb, BLOCK:, :] = jnp.where(head_a, zero_b, kb)
        vv_scr[jb, 0:BLOCK, :] = jnp.where(head_a, vb, zero_b)
        vv_scr[jb, BLOCK:, :] = jnp.where(head_a, zero_b, vb)
        return carry

    lax.fori_loop(0, n_blocks, prep, 0)

    jj = lax.broadcasted_iota(jnp.int32, (2 * BLOCK, 2 * BLOCK), 0) % BLOCK
    ss = lax.broadcasted_iota(jnp.int32, (2 * BLOCK, 2 * BLOCK), 1)
    tri_scr[...] = jnp.where((ss >= BLOCK) | (jj > ss), 1.0, 0.0).astype(BF16)

    row = lax.broadcasted_iota(jnp.int32, (qt, 2 * BLOCK), 0)
    col = lax.broadcasted_iota(jnp.int32, (qt, 2 * BLOCK), 1) % BLOCK

    def key_block(qb, jb, key_off):
        z = lax.dot_general(qb, kk_scr[jb], (((1,), (1,)), ((), ())), preferred_element_type=F32)
        neg_abs = lax.bitcast_convert_type(lax.bitcast_convert_type(z, jnp.uint32) | jnp.uint32(0x80000000), F32)
        ls = jnp.minimum(z, 0.0) - jnp.log2(1.0 + jnp.exp2(neg_abs))
        lr = ls - z
        if key_off is not None:
            causal = (col + key_off) < row
            lr = jnp.where(causal, lr, 0.0)
        hi = lr.astype(BF16)
        lo = (lr - hi.astype(F32)).astype(BF16)
        tri = tri_scr[...]
        a_heads = []
        for h in range(2):
            sl = slice(h * BLOCK, (h + 1) * BLOCK)
            lhs = jnp.concatenate([hi[:, sl], lo[:, sl]], axis=1)
            cs_tot = jnp.dot(lhs, tri, preferred_element_type=F32)
            carry = carry_scr[:, sl]
            a = jnp.exp2(ls[:, sl] + cs_tot[:, :BLOCK] + carry)
            if key_off is not None:
                a = jnp.where(causal[:, :BLOCK], a, 0.0)
            carry_scr[:, sl] = carry + cs_tot[:, BLOCK:]
            a_heads.append(a.astype(BF16))
        acc_scr[...] += jnp.dot(jnp.concatenate(a_heads, axis=1), vv_scr[jb], preferred_element_type=F32)

    def q_tile(i, carry):
        rows = pl.ds(pl.multiple_of(i * qt, qt), qt)
        qb = q_ref[0, rows, :]
        acc_scr[...] = jnp.zeros_like(acc_scr)
        carry_scr[...] = jnp.zeros_like(carry_scr)
        for t in reversed(range(per_tile)):
            key_block(qb, i * per_tile + t, t * BLOCK)

        def inner(t, c):
            key_block(qb, i * per_tile - 1 - t, None)
            return c

        lax.fori_loop(0, i * per_tile, inner, 0)
        o_ref[0, rows, :] = acc_scr[...]
        return carry

    lax.fori_loop(0, seq // qt, q_tile, 0)


def _stickbreak_attention(proj3, *, half_width, qt):
    b, seq, _ = proj3.shape
    pairs = half_width // LANES
    n_blocks = seq // BLOCK
    assert seq % qt == 0 and qt % BLOCK == 0
    spec = lambda off: pl.BlockSpec((1, seq, LANES), lambda bi, c, off=off: (bi, 0, off + c))
    return pl.pallas_call(
        functools.partial(_sb_kernel, seq=seq, qt=qt),
        grid=(b, pairs),
        in_specs=[spec(3 * pairs), spec(4 * pairs), spec(5 * pairs)],
        out_specs=pl.BlockSpec((1, seq, LANES), lambda bi, c: (bi, 0, c)),
        out_shape=jax.ShapeDtypeStruct((b, seq, half_width), F32),
        scratch_shapes=[
            pltpu.VMEM((n_blocks, 2 * BLOCK, LANES), BF16),
            pltpu.VMEM((n_blocks, 2 * BLOCK, LANES), BF16),
            pltpu.VMEM((2 * BLOCK, 2 * BLOCK), BF16),
            pltpu.VMEM((qt, LANES), F32),
            pltpu.VMEM((qt, 2 * BLOCK), F32),
        ],
        compiler_params=_cparams(("arbitrary", "arbitrary")),
        name="stickbreak_attention",
    )(proj3, proj3, proj3)


def _split_bf16(x):
    hi = x.astype(BF16)
    return hi, (x - hi.astype(F32)).astype(BF16)


def _outproj_kernel(od_ref, os_ref, x_ref, w_ref, gd_ref, gs_ref, gf_ref, wr_ref, br_ref,
                    x2_ref, h2_ref, idx_ref, gate_ref, acc_d, acc_s, ssq_d, ssq_s, *, nk, half_width):
    k = pl.program_id(1)

    @pl.when(k == 0)
    def _():
        acc_d[...] = jnp.zeros_like(acc_d)
        acc_s[...] = jnp.zeros_like(acc_s)
        ssq_d[...] = jnp.zeros_like(ssq_d)
        ssq_s[...] = jnp.zeros_like(ssq_s)

    w = w_ref[...].astype(BF16)

    def accumulate(a_ref, g_ref, acc, ssq):
        a = a_ref[...]
        ssq[...] += jnp.sum(a * a, axis=-1, keepdims=True)
        acc[...] += jnp.dot((a * g_ref[...]).astype(BF16), w, preferred_element_type=F32)

    @pl.when(k < nk // 2)
    def _():
        accumulate(od_ref, gd_ref, acc_d, ssq_d)

    @pl.when(k >= nk // 2)
    def _():
        accumulate(os_ref, gs_ref, acc_s, ssq_s)

    @pl.when(k == nk - 1)
    def _():
        r_d = lax.rsqrt(ssq_d[...] / half_width + NORM_EPS)
        r_s = lax.rsqrt(ssq_s[...] / half_width + NORM_EPS)
        x2 = x_ref[...] + (acc_d[...] * r_d + acc_s[...] * r_s)
        x2_ref[...] = x2
        h2 = (x2 * lax.rsqrt(jnp.mean(x2 * x2, axis=-1, keepdims=True) + NORM_EPS)) * gf_ref[...]
        h2_ref[...] = _pack_pairs(h2)
        h_hi, h_lo = _split_bf16(h2)
        w_hi, w_lo = _split_bf16(wr_ref[...])
        logits = (jnp.dot(h_hi, w_hi, preferred_element_type=F32)
                  + jnp.dot(h_hi, w_lo, preferred_element_type=F32)
                  + jnp.dot(h_lo, w_hi, preferred_element_type=F32)) + br_ref[...]
        tm = logits.shape[0]
        e_iota = lax.broadcasted_iota(jnp.int32, logits.shape, 1)
        out_lane = lax.broadcasted_iota(jnp.int32, (tm, LANES), 1)
        idx_out = jnp.zeros((tm, LANES), jnp.int32)
        val_out = jnp.full((tm, LANES), -jnp.inf, F32)
        top0 = None
        for kk in range(TOP_K):
            m = jnp.max(logits, axis=-1, keepdims=True)
            sel = jnp.min(jnp.where(logits == m, e_iota, N_EXPERTS), axis=-1, keepdims=True)
            idx_out = jnp.where(out_lane == kk, sel, idx_out)
            val_out = jnp.where(out_lane == kk, m, val_out)
            logits = jnp.where(e_iota == sel, -jnp.inf, logits)
            if kk == 0:
                top0 = m
        ex = jnp.exp(val_out - top0)
        gate_ref[...] = ex / jnp.sum(ex, axis=-1, keepdims=True)
        idx_ref[...] = idx_out


def _outproj_router(o_d, o_s, x2d, w_out, g_d, g_s, g_f, w_router, b_router, *, tm, tk):
    m, d = x2d.shape
    half_width = d // 2
    nk = d // tk
    assert half_width % tk == 0 and m % tm == 0
    kh = nk // 2
    return pl.pallas_call(
        functools.partial(_outproj_kernel, nk=nk, half_width=half_width),
        grid=(m // tm, nk),
        in_specs=[
            pl.BlockSpec((tm, tk), lambda i, k: (i, jnp.minimum(k, kh - 1))),
            pl.BlockSpec((tm, tk), lambda i, k: (i, jnp.maximum(k - kh, 0))),
            pl.BlockSpec((tm, d), lambda i, k: (i, 0)),
            pl.BlockSpec((tk, d), lambda i, k: (k, 0)),
            pl.BlockSpec((1, tk), lambda i, k: (0, jnp.minimum(k, kh - 1))),
            pl.BlockSpec((1, tk), lambda i, k: (0, jnp.maximum(k - kh, 0))),
            pl.BlockSpec((1, d), lambda i, k: (0, 0)),
            pl.BlockSpec((d, N_EXPERTS), lambda i, k: (0, 0)),
            pl.BlockSpec((1, N_EXPERTS), lambda i, k: (0, 0)),
        ],
        out_specs=[
            pl.BlockSpec((tm, d), lambda i, k: (i, 0)),
            pl.BlockSpec((tm, d // 2), lambda i, k: (i, 0)),
            pl.BlockSpec((tm, LANES), lambda i, k: (i, 0)),
            pl.BlockSpec((tm, LANES), lambda i, k: (i, 0)),
        ],
        out_shape=[
            jax.ShapeDtypeStruct((m, d), F32),
            jax.ShapeDtypeStruct((m, d // 2), jnp.uint32),
            jax.ShapeDtypeStruct((m, LANES), jnp.int32),
            jax.ShapeDtypeStruct((m, LANES), F32),
        ],
        scratch_shapes=[
            pltpu.VMEM((tm, d), F32),
            pltpu.VMEM((tm, d), F32),
            pltpu.VMEM((tm, 1), F32),
            pltpu.VMEM((tm, 1), F32),
        ],
        compiler_params=_cparams(("arbitrary", "arbitrary")),
        name="outproj_router",
    )(o_d, o_s, x2d, w_out, g_d, g_s, g_f, w_router, b_router)


def _unpack_pairs(words):
    hi = lax.bitcast_convert_type(words & jnp.uint32(0xFFFF0000), F32)
    lo = lax.bitcast_convert_type(words << 16, F32)
    return jnp.concatenate([hi.astype(BF16), lo.astype(BF16)], axis=1)


def _pack_pairs(x):
    c = x.shape[1] // 2
    as_bits = lambda v: lax.bitcast_convert_type(v.astype(BF16).astype(F32), jnp.uint32)
    return as_bits(x[:, :c]) | (as_bits(x[:, c:]) >> 16)


def _expert_kernel(item_e, item_r0, item_nb, n_items, row_tok,
                   h_hbm, wg_ref, bg_ref, wu_ref, bu_ref, wd_ref, bd_ref,
                   ys_hbm, stage, xb16, acc, zbuf, issued, sem_in, sem_out, *, nj):
    i = pl.program_id(0)
    j = pl.program_id(1)
    n_live = n_items[0]
    active = i < n_live
    r0 = item_r0[i]
    n_align = item_nb[i]
    n_rows = n_align * ROW_ALIGN
    n_mm = (n_rows + ROW_BLOCK - 1) // ROW_BLOCK
    nxt = jnp.minimum(i + 1, pl.num_programs(0) - 1)
    r0_next = item_r0[nxt]
    rows_next = jnp.where(i + 1 < n_live, item_nb[nxt] * ROW_ALIGN, 0)

    def row_copy(base, r):
        tok = row_tok[base + r]
        return pltpu.make_async_copy(h_hbm.at[pl.ds(tok, 1)], stage.at[pl.ds(r, 1)], sem_in)

    def out_copy(rb):
        src = acc.at[pl.ds(pl.multiple_of(rb * ROW_ALIGN, ROW_ALIGN), ROW_ALIGN)]
        dst = ys_hbm.at[pl.ds(pl.multiple_of(r0 + rb * ROW_ALIGN, ROW_ALIGN), ROW_ALIGN)]
        return pltpu.make_async_copy(src, dst, sem_out)

    @pl.when((i == 0) & (j == 0))
    def _():
        xb16[...] = jnp.zeros_like(xb16)

        def issue(r, c):
            row_copy(r0, r).start()
            return c

        lax.fori_loop(0, n_rows, issue, 0)

    @pl.when(active & (j == 0))
    def _():
        def wait(r, c):
            row_copy(r0, r).wait()
            return c

        lax.fori_loop(0, n_rows, wait, 0)

        def unpack(rb, c):
            rows = pl.ds(pl.multiple_of(rb * ROW_ALIGN, ROW_ALIGN), ROW_ALIGN)
            xb16[rows, :] = _unpack_pairs(stage[rows, :])
            acc[rows, :] = jnp.broadcast_to(bd_ref[0], (ROW_ALIGN, acc.shape[1]))
            return c

        lax.fori_loop(0, n_align, unpack, 0)
        issued[0] = 0

    @pl.when(active)
    def _():
        def mm(rb, c):
            rows = pl.ds(pl.multiple_of(rb * ROW_BLOCK, ROW_BLOCK), ROW_BLOCK)
            xb = xb16[rows, :]
            g = jnp.dot(xb, wg_ref[0].astype(BF16), preferred_element_type=F32) + bg_ref[0]
            u = jnp.dot(xb, wu_ref[0].astype(BF16), preferred_element_type=F32) + bu_ref[0]
            g = jnp.minimum(g, SWIGLU_LIMIT)
            u = jnp.clip(u, -SWIGLU_LIMIT, SWIGLU_LIMIT)
            a = (u + 1.0) * (g * jax.nn.sigmoid(SWIGLU_ALPHA * g))
            acc[rows, :] += jnp.dot(a.astype(BF16), wd_ref[0].astype(BF16), preferred_element_type=F32)

            base = issued[0]

            @pl.when(base < rows_next)
            def _():
                for t in range(GATHER_CHUNK):
                    row_copy(r0_next, base + t).start()

            issued[0] = base + GATHER_CHUNK

            @pl.when(j == nj - 1)
            def _():
                for s in range(ROW_BLOCK // ROW_ALIGN):
                    blk = rb * (ROW_BLOCK // ROW_ALIGN) + s

                    @pl.when(blk < n_align)
                    def _():
                        out_copy(blk).start()

            return c

        lax.fori_loop(0, n_mm, mm, 0)

    @pl.when(active & (j == nj - 1))
    def _():
        def issue_rest(r, c):
            row_copy(r0_next, r).start()
            return c

        lax.fori_loop(issued[0], rows_next, issue_rest, 0)

        def wait(rb, c):
            out_copy(rb).wait()
            return c

        lax.fori_loop(0, n_align, wait, 0)

    @pl.when((i == pl.num_programs(0) - 1) & (j == nj - 1))
    def _():
        zbuf[...] = jnp.zeros_like(zbuf)
        used = n_items[1]

        def tail_copy(t):
            dst = ys_hbm.at[pl.ds(pl.multiple_of((used + t) * ROW_ALIGN, ROW_ALIGN), ROW_ALIGN)]
            return pltpu.make_async_copy(zbuf, dst, sem_out)

        def issue(t, c):
            tail_copy(t).start()
            return c

        n_tail = ys_hbm.shape[0] // ROW_ALIGN - used
        lax.fori_loop(0, n_tail, issue, 0)

        def wait(t, c):
            tail_copy(t).wait()
            return c

        lax.fori_loop(0, n_tail, wait, 0)


def _experts(h2, w_gate, b_gate, w_up, b_up, w_down, b_down, meta, *, n_rows, th):
    item_e, item_r0, item_nb, n_items, row_tok = meta
    n_exp, d, dh = w_gate.shape
    nj = dh // th
    n_item_slots = item_e.shape[0]

    def w_col(i, j, item_e, item_r0, item_nb, n_items, row_tok):
        return (item_e[i], 0, jnp.where(i < n_items[0], j, nj - 1))

    def w_row(i, j, item_e, item_r0, item_nb, n_items, row_tok):
        return (item_e[i], jnp.where(i < n_items[0], j, nj - 1), 0)

    def b_all(i, j, item_e, item_r0, item_nb, n_items, row_tok):
        return (item_e[i], 0, 0)

    grid_spec = pltpu.PrefetchScalarGridSpec(
        num_scalar_prefetch=5,
        grid=(n_item_slots, nj),
        in_specs=[
            pl.BlockSpec(memory_space=pl.ANY),
            pl.BlockSpec((1, d, th), w_col),
            pl.BlockSpec((1, 1, th), w_col),
            pl.BlockSpec((1, d, th), w_col),
            pl.BlockSpec((1, 1, th), w_col),
            pl.BlockSpec((1, th, d), w_row),
            pl.BlockSpec((1, 1, d), b_all),
        ],
        out_specs=pl.BlockSpec(memory_space=pl.ANY),
        scratch_shapes=[
            pltpu.VMEM((ITEM_ROWS, d // 2), jnp.uint32),
            pltpu.VMEM((ITEM_ROWS, d), BF16),
            pltpu.VMEM((ITEM_ROWS, d), F32),
            pltpu.VMEM((ROW_ALIGN, d), F32),
            pltpu.SMEM((1,), jnp.int32),
            pltpu.SemaphoreType.DMA,
            pltpu.SemaphoreType.DMA,
        ],
    )
    return pl.pallas_call(
        functools.partial(_expert_kernel, nj=nj),
        grid_spec=grid_spec,
        out_shape=jax.ShapeDtypeStruct((n_rows, d), F32),
        compiler_params=_cparams(("arbitrary", "arbitrary")),
        name="experts",
    )(item_e, item_r0, item_nb, n_items, row_tok,
      h2, w_gate, b_gate.reshape(n_exp, 1, dh), w_up, b_up.reshape(n_exp, 1, dh), w_down,
      b_down.reshape(n_exp, 1, d))


def _routing_metadata(top_idx, n_tok):
    n_assign = n_tok * TOP_K
    flat_e = top_idx.reshape(n_assign)
    onehot = (flat_e[:, None] == jnp.arange(N_EXPERTS, dtype=jnp.int32)[None, :]).astype(jnp.int32)
    rank = jnp.take_along_axis(jnp.cumsum(onehot, axis=0), flat_e[:, None], axis=1)[:, 0] - 1
    counts = jnp.sum(onehot, axis=0)
    padded = (counts + ROW_ALIGN - 1) // ROW_ALIGN * ROW_ALIGN
    pend = jnp.cumsum(padded)
    pstart = pend - padded
    dest = (pstart[flat_e] + rank).astype(jnp.int32)
    n_rows = (n_assign // ROW_ALIGN + N_EXPERTS) * ROW_ALIGN
    flat_tok = jnp.arange(n_assign, dtype=jnp.int32) // TOP_K
    row_tok = jnp.zeros((n_rows,), jnp.int32).at[dest].set(flat_tok)
    n_chunks = (padded + ITEM_ROWS - 1) // ITEM_ROWS
    cend = jnp.cumsum(n_chunks)
    cstart = cend - n_chunks
    n_slots = N_EXPERTS + n_assign // ITEM_ROWS + 1
    slot = jnp.arange(n_slots, dtype=jnp.int32)
    n_items = cend[-1]
    last_e = jnp.max(jnp.where(n_chunks > 0, jnp.arange(N_EXPERTS), 0))
    e_of = jnp.minimum(jnp.searchsorted(cend, slot, side='right'), N_EXPERTS - 1)
    live = slot < n_items
    item_e = jnp.where(live, e_of, last_e).astype(jnp.int32)
    chunk = slot - cstart[e_of]
    item_r0 = jnp.where(live, pstart[e_of] + chunk * ITEM_ROWS, 0).astype(jnp.int32)
    item_rows = jnp.clip(padded[e_of] - chunk * ITEM_ROWS, 0, ITEM_ROWS)
    item_nb = jnp.where(live, item_rows // ROW_ALIGN, 0).astype(jnp.int32)
    counts_blk = jnp.stack([n_items, pend[-1] // ROW_ALIGN]).astype(jnp.int32)
    meta = (item_e, item_r0, item_nb, counts_blk, row_tok)
    return meta, dest, n_rows


def _combine_kernel(dest, ys_hbm, x2_ref, gate_ref, g_ref, o_ref, buf, sem, *, tm):
    i = pl.program_id(0)

    def row_copy(t, kk):
        r = dest[(i * tm + t) * TOP_K + kk]
        return pltpu.make_async_copy(ys_hbm.at[pl.ds(r, 1)], buf.at[kk, pl.ds(t, 1)], sem)

    def issue(t, c):
        for kk in range(TOP_K):
            row_copy(t, kk).start()
        return c

    lax.fori_loop(0, tm, issue, 0)

    def wait(t, c):
        for kk in range(TOP_K):
            row_copy(t, kk).wait()
        return c

    lax.fori_loop(0, tm, wait, 0)

    gates = gate_ref[...]
    y = x2_ref[...]
    for kk in range(TOP_K):
        y = y + gates[:, kk:kk + 1] * buf[kk]
    o_ref[...] = (y * lax.rsqrt(jnp.mean(y * y, axis=-1, keepdims=True) + NORM_EPS)) * g_ref[...]


def _combine(dest, ys, x2, gates, g_final, *, tm):
    m, d = x2.shape
    grid_spec = pltpu.PrefetchScalarGridSpec(
        num_scalar_prefetch=1,
        grid=(m // tm,),
        in_specs=[
            pl.BlockSpec(memory_space=pl.ANY),
            pl.BlockSpec((tm, d), lambda i, dest: (i, 0)),
            pl.BlockSpec((tm, LANES), lambda i, dest: (i, 0)),
            pl.BlockSpec((1, d), lambda i, dest: (0, 0)),
        ],
        out_specs=pl.BlockSpec((tm, d), lambda i, dest: (i, 0)),
        scratch_shapes=[pltpu.VMEM((TOP_K, tm, d), F32), pltpu.SemaphoreType.DMA],
    )
    return pl.pallas_call(
        functools.partial(_combine_kernel, tm=tm),
        grid_spec=grid_spec,
        out_shape=jax.ShapeDtypeStruct((m, d), F32),
        compiler_params=_cparams(("arbitrary",)),
        name="combine",
    )(dest, ys, x2, gates, g_final)


def _layer(x, attn_norm, w_in, dil_out_norm, sb_out_norm, w_out, ffn_norm,
           w_router, b_router, w_gate, b_gate, w_up, b_up, w_down, b_down, *, tiles):
    b, seq, d = x.shape
    m = b * seq
    half_width = d // 2
    x2d = x.reshape(m, d)
    cos_t, sin_t = _rope_tables(seq)
    proj = _inproj(x2d, attn_norm.reshape(1, d), w_in, cos_t, sin_t,
                   seq=seq, tm=tiles["in_tm"], tn=tiles["in_tn"])
    proj3 = proj.reshape(b, seq, 3 * d)
    o_d = _dilated_attention(proj3, half_width=half_width).reshape(m, half_width)
    o_s = _stickbreak_attention(proj3, half_width=half_width, qt=tiles["sb_qt"]).reshape(m, half_width)
    x2, h2, idx_l, gate_l = _outproj_router(
        o_d, o_s, x2d, w_out, dil_out_norm.reshape(1, half_width), sb_out_norm.reshape(1, half_width),
        ffn_norm.reshape(1, d), w_router, b_router.reshape(1, N_EXPERTS),
        tm=tiles["out_tm"], tk=tiles["out_tk"])
    meta, dest, n_rows = _routing_metadata(idx_l[:, :TOP_K], m)
    ys = _experts(h2, w_gate, b_gate, w_up, b_up, w_down, b_down, meta, n_rows=n_rows, th=tiles["moe_th"])
    return x2, ys, dest, gate_l


_TILES = dict(in_tm=1024, in_tn=512, sb_qt=512, out_tm=512, out_tk=512, moe_th=512, cmb_tm=128)


def kernel(x, attn_norm, w_in, dil_out_norm, sb_out_norm, w_out, ffn_norm, w_router, b_router,
           w_gate, b_gate, w_up, b_up, w_down, b_down, final_norm):
    depth = attn_norm.shape[0]
    b, seq, d = x.shape
    for l in range(depth):
        x2, ys, dest, gate_l = _layer(
            x, attn_norm[l], w_in[l], dil_out_norm[l], sb_out_norm[l], w_out[l], ffn_norm[l],
            w_router[l], b_router[l], w_gate[l], b_gate[l], w_up[l], b_up[l], w_down[l], b_down[l],
            tiles=_TILES)
        assert depth == 1
        x = _combine(dest, ys, x2, gate_l, final_norm.reshape(1, d), tm=_TILES["cmb_tm"])
    return x.reshape(b, seq, d)
```

```python
import functools
import math

import jax
import jax.numpy as jnp
from jax import lax
from jax.experimental import pallas as pl
from jax.experimental.pallas import tpu as pltpu

F32 = jnp.float32
BF16 = jnp.bfloat16

HEAD_DIM = 64
LANES = 128
BLOCK = 128
DIL_PATTERNS = ((128, 1), (512, 4), (2048, 16))
ROPE_THETA = 10000.0
N_EXPERTS = 32
TOP_K = 4
SWIGLU_LIMIT = 7.0
SWIGLU_ALPHA = 1.702
NORM_EPS = 1e-5
Q_SCALE = HEAD_DIM ** -0.5 * math.log2(math.e)
GATHER_CHUNK = 128
DIL_UNROLL = 4
ROW_ALIGN = 128
ROW_BLOCK = 256
ITEM_ROWS = 1280
VMEM_LIMIT = 56 * 1024 * 1024


def _cparams(sem):
    return pltpu.CompilerParams(dimension_semantics=sem, vmem_limit_bytes=VMEM_LIMIT)


def _inproj_kernel(x_ref, g_ref, w_ref, cos_ref, sin_ref, o_ref, h_scr, *, tn, half_width):
    j = pl.program_id(1)

    @pl.when(j == 0)
    def _():
        x = x_ref[...]
        y = x * lax.rsqrt(jnp.mean(x * x, axis=-1, keepdims=True) + NORM_EPS)
        h_scr[...] = (y * g_ref[...]).astype(BF16)

    acc = jnp.dot(h_scr[...], w_ref[...].astype(BF16), preferred_element_type=F32)
    section = (j * tn) // half_width
    scale = jnp.where((section == 0) | (section == 3), Q_SCALE, 1.0).astype(F32)

    @pl.when(section < 2)
    def _():
        lane = lax.broadcasted_iota(jnp.int32, (acc.shape[0], LANES), 1)
        first_half = (lane % HEAD_DIM) < (HEAD_DIM // 2)
        c = cos_ref[...]
        s = sin_ref[...]
        for g in range(tn // LANES):
            xg = acc[:, g * LANES:(g + 1) * LANES]
            partner = jnp.where(first_half,
                                pltpu.roll(xg, LANES - HEAD_DIM // 2, 1),
                                pltpu.roll(xg, HEAD_DIM // 2, 1))
            o_ref[:, g * LANES:(g + 1) * LANES] = ((xg * c + partner * s) * scale).astype(BF16)

    @pl.when(section >= 2)
    def _():
        o_ref[...] = (acc * scale).astype(BF16)


def _inproj(x2d, g, w_in, cos_t, sin_t, *, seq, tm, tn):
    m, d = x2d.shape
    n = w_in.shape[1]
    half_width = d // 2
    assert half_width % tn == 0 and tn % LANES == 0 and m % tm == 0 and seq % tm == 0
    pos_tiles = seq // tm
    return pl.pallas_call(
        functools.partial(_inproj_kernel, tn=tn, half_width=half_width),
        grid=(m // tm, n // tn),
        in_specs=[
            pl.BlockSpec((tm, d), lambda i, j: (i, 0)),
            pl.BlockSpec((1, d), lambda i, j: (0, 0)),
            pl.BlockSpec((d, tn), lambda i, j: (0, j)),
            pl.BlockSpec((tm, LANES), lambda i, j: (i % pos_tiles, 0)),
            pl.BlockSpec((tm, LANES), lambda i, j: (i % pos_tiles, 0)),
        ],
        out_specs=pl.BlockSpec((tm, tn), lambda i, j: (i, j)),
        out_shape=jax.ShapeDtypeStruct((m, n), BF16),
        scratch_shapes=[pltpu.VMEM((tm, d), BF16)],
        compiler_params=_cparams(("arbitrary", "arbitrary")),
        name="inproj",
    )(x2d, g, w_in, cos_t, sin_t)


def _rope_tables(seq):
    half = HEAD_DIM // 2
    inv_freq = 1.0 / (ROPE_THETA ** (jnp.arange(half, dtype=F32) * 2.0 / HEAD_DIM))
    ang = jnp.arange(seq, dtype=F32)[:, None] * inv_freq[None, :]
    cos = jnp.cos(ang)
    sin = jnp.sin(ang)
    reps = LANES // HEAD_DIM
    cos_t = jnp.tile(jnp.concatenate([cos, cos], axis=1), (1, reps))
    sin_t = jnp.tile(jnp.concatenate([-sin, sin], axis=1), (1, reps))
    return cos_t, sin_t


def _dil_kernel(q_ref, k_ref, v_ref, o_ref, f32_scr, qp, kp, vp, operm, lperm, onat, lnat, *, seq):
    n_blocks = seq // BLOCK
    lane_k = lax.broadcasted_iota(jnp.int32, (2 * BLOCK, LANES), 1)
    head_a_k = lane_k < HEAD_DIM
    lane_q = lax.broadcasted_iota(jnp.int32, (BLOCK, LANES), 1)
    head_a_q = lane_q < HEAD_DIM
    qi = lax.broadcasted_iota(jnp.int32, (BLOCK, 2 * BLOCK), 0)
    ki = lax.broadcasted_iota(jnp.int32, (BLOCK, 2 * BLOCK), 1)
    dist = qi + BLOCK - ki
    band = (dist >= 0) & (dist <= BLOCK)
    cur_only = ki >= BLOCK
    zero_b = jnp.zeros((), BF16)

    f32_scr[0] = q_ref[0].astype(F32)
    f32_scr[1] = k_ref[0].astype(F32)
    f32_scr[2] = v_ref[0].astype(F32)
    kp[0:BLOCK, :] = jnp.zeros((BLOCK, LANES), BF16)
    vp[0:BLOCK, :] = jnp.zeros((BLOCK, LANES), BF16)

    for p, (window, dil) in enumerate(DIL_PATTERNS):
        assert window // dil == BLOCK
        length = seq // dil
        per_stream = length // BLOCK
        assert length % BLOCK == 0
        if dil == 1:
            qp[...] = q_ref[0]
            kp[BLOCK:, :] = k_ref[0]
            vp[BLOCK:, :] = v_ref[0]
        else:
            for r in range(dil):
                rows = pl.ds(r, length, stride=dil)
                qp[r * length:(r + 1) * length, :] = f32_scr[0, rows, :].astype(BF16)
                kp[BLOCK + r * length:BLOCK + (r + 1) * length, :] = f32_scr[1, rows, :].astype(BF16)
                vp[BLOCK + r * length:BLOCK + (r + 1) * length, :] = f32_scr[2, rows, :].astype(BF16)
        o_dst = onat.at[p] if dil == 1 else operm
        l_dst = lnat.at[p] if dil == 1 else lperm

        def block_body(bi, carry, per_stream=per_stream, o_dst=o_dst, l_dst=l_dst):
            row0 = pl.multiple_of(bi * BLOCK, BLOCK)
            qb = qp[pl.ds(row0, BLOCK), :]
            kw = kp[pl.ds(row0, 2 * BLOCK), :]
            vw = vp[pl.ds(row0, 2 * BLOCK), :]
            has_prev = (bi % per_stream) != 0
            valid = band & (cur_only | has_prev)
            kk = jnp.concatenate([jnp.where(head_a_k, kw, zero_b), jnp.where(head_a_k, zero_b, kw)], axis=0)
            s2 = lax.dot_general(qb, kk, (((1,), (1,)), ((), ())), preferred_element_type=F32)
            ps, ms, dens = [], [], []
            for h in range(2):
                s = jnp.where(valid, s2[:, h * 2 * BLOCK:(h + 1) * 2 * BLOCK], -jnp.inf)
                m = jnp.max(s, axis=-1, keepdims=True)
                e = jnp.exp2(s - m)
                ps.append(e.astype(BF16))
                ms.append(m)
                dens.append(jnp.sum(e, axis=-1, keepdims=True))
            vv = jnp.concatenate([jnp.where(head_a_k, vw, zero_b), jnp.where(head_a_k, zero_b, vw)], axis=0)
            o = jnp.dot(jnp.concatenate(ps, axis=1), vv, preferred_element_type=F32)
            den = jnp.where(head_a_q, dens[0], dens[1])
            o_dst[pl.ds(row0, BLOCK), :] = o / den
            l_dst[pl.ds(row0, BLOCK), :] = jnp.where(head_a_q, ms[0], ms[1]) + jnp.log2(den)
            return carry

        lax.fori_loop(0, n_blocks, block_body, 0, unroll=DIL_UNROLL)
        if dil != 1:
            for r in range(dil):
                rows = pl.ds(r, length, stride=dil)
                onat[p, rows, :] = operm[r * length:(r + 1) * length, :]
                lnat[p, rows, :] = lperm[r * length:(r + 1) * length, :]

    def mix_body(c, carry):
        rows = pl.ds(pl.multiple_of(c * BLOCK, BLOCK), BLOCK)
        ls = [lnat[p, rows, :] for p in range(len(DIL_PATTERNS))]
        m = functools.reduce(jnp.maximum, ls)
        es = [jnp.exp2(l - m) for l in ls]
        num = functools.reduce(lambda a, b: a + b, [e * onat[p, rows, :] for p, e in enumerate(es)])
        o_ref[0, rows, :] = num / functools.reduce(lambda a, b: a + b, es)
        return carry

    lax.fori_loop(0, n_blocks, mix_body, 0)


def _dilated_attention(proj3, *, half_width):
    b, seq, _ = proj3.shape
    pairs = half_width // LANES
    n_pat = len(DIL_PATTERNS)
    spec = lambda off: pl.BlockSpec((1, seq, LANES), lambda bi, c, off=off: (bi, 0, off + c))
    return pl.pallas_call(
        functools.partial(_dil_kernel, seq=seq),
        grid=(b, pairs),
        in_specs=[spec(0), spec(pairs), spec(2 * pairs)],
        out_specs=pl.BlockSpec((1, seq, LANES), lambda bi, c: (bi, 0, c)),
        out_shape=jax.ShapeDtypeStruct((b, seq, half_width), F32),
        scratch_shapes=[
            pltpu.VMEM((3, seq, LANES), F32),
            pltpu.VMEM((seq, LANES), BF16),
            pltpu.VMEM((seq + BLOCK, LANES), BF16),
            pltpu.VMEM((seq + BLOCK, LANES), BF16),
            pltpu.VMEM((seq, LANES), F32),
            pltpu.VMEM((seq, LANES), F32),
            pltpu.VMEM((n_pat, seq, LANES), F32),
            pltpu.VMEM((n_pat, seq, LANES), F32),
        ],
        compiler_params=_cparams(("arbitrary", "arbitrary")),
        name="dilated_attention",
    )(proj3, proj3, proj3)


def _sb_kernel(q_ref, k_ref, v_ref, o_ref, kk_scr, vv_scr, tri_scr, acc_scr, carry_scr, *, seq, qt):
    n_blocks = seq // BLOCK
    per_tile = qt // BLOCK
    lane_k = lax.broadcasted_iota(jnp.int32, (BLOCK, LANES), 1)
    head_a = lane_k < HEAD_DIM
    zero_b = jnp.zeros((), BF16)

    def prep(jb, carry):
        rows = pl.ds(pl.multiple_of(jb * BLOCK, BLOCK), BLOCK)
        kb = k_ref[0, rows, :]
        vb = v_ref[0, rows, :]
        kk_scr[jb, 0:BLOCK, :] = jnp.where(head_a, kb, zero_b)
        kk_scr[jb, BLOCK:, :] = jnp.where(head_a, zero_b, kb)
        vv_scr[jb, 0:BLOCK, :] = jnp.where(head_a, vb, zero_b)
        vv_scr[jb, BLOCK:, :] = jnp.where(head_a, zero_b, vb)
        return carry

    lax.fori_loop(0, n_blocks, prep, 0)

    jj = lax.broadcasted_iota(jnp.int32, (2 * BLOCK, 2 * BLOCK), 0) % BLOCK
    ss = lax.broadcasted_iota(jnp.int32, (2 * BLOCK, 2 * BLOCK), 1)
    tri_scr[...] = jnp.where((ss >= BLOCK) | (jj > ss), 1.0, 0.0).astype(BF16)

    row = lax.broadcasted_iota(jnp.int32, (qt, 2 * BLOCK), 0)
    col = lax.broadcasted_iota(jnp.int32, (qt, 2 * BLOCK), 1) % BLOCK

    def key_block(qb, jb, key_off):
        z = lax.dot_general(qb, kk_scr[jb], (((1,), (1,)), ((), ())), preferred_element_type=F32)
        neg_abs = lax.bitcast_convert_type(lax.bitcast_convert_type(z, jnp.uint32) | jnp.uint32(0x80000000), F32)
        ls = jnp.minimum(z, 0.0) - jnp.log2(1.0 + jnp.exp2(neg_abs))
        lr = ls - z
        if key_off is not None:
            causal = (col + key_off) < row
            lr = jnp.where(causal, lr, 0.0)
        hi = lr.astype(BF16)
        lo = (lr - hi.astype(F32)).astype(BF16)
        tri = tri_scr[...]
        a_heads = []
        for h in range(2):
            sl = slice(h * BLOCK, (h + 1) * BLOCK)
            lhs = jnp.concatenate([hi[:, sl], lo[:, sl]], axis=1)
            cs_tot = jnp.dot(lhs, tri, preferred_element_type=F32)
            carry = carry_scr[:, sl]
            a = jnp.exp2(ls[:, sl] + cs_tot[:, :BLOCK] + carry)
            if key_off is not None:
                a = jnp.where(causal[:, :BLOCK], a, 0.0)
            carry_scr[:, sl] = carry + cs_tot[:, BLOCK:]
            a_heads.append(a.astype(BF16))
        acc_scr[...] += jnp.dot(jnp.concatenate(a_heads, axis=1), vv_scr[jb], preferred_element_type=F32)

    def q_tile(i, carry):
        rows = pl.ds(pl.multiple_of(i * qt, qt), qt)
        qb = q_ref[0, rows, :]
        acc_scr[...] = jnp.zeros_like(acc_scr)
        carry_scr[...] = jnp.zeros_like(carry_scr)
        for t in reversed(range(per_tile)):
            key_block(qb, i * per_tile + t, t * BLOCK)

        def inner(t, c):
            key_block(qb, i * per_tile - 1 - t, None)
            return c

        lax.fori_loop(0, i * per_tile, inner, 0)
        o_ref[0, rows, :] = acc_scr[...]
        return carry

    lax.fori_loop(0, seq // qt, q_tile, 0)


def _stickbreak_attention(proj3, *, half_width, qt):
    b, seq, _ = proj3.shape
    pairs = half_width // LANES
    n_blocks = seq // BLOCK
    assert seq % qt == 0 and qt % BLOCK == 0
    spec = lambda off: pl.BlockSpec((1, seq, LANES), lambda bi, c, off=off: (bi, 0, off + c))
    return pl.pallas_call(
        functools.partial(_sb_kernel, seq=seq, qt=qt),
        grid=(b, pairs),
        in_specs=[spec(3 * pairs), spec(4 * pairs), spec(5 * pairs)],
        out_specs=pl.BlockSpec((1, seq, LANES), lambda bi, c: (bi, 0, c)),
        out_shape=jax.ShapeDtypeStruct((b, seq, half_width), F32),
        scratch_shapes=[
            pltpu.VMEM((n_blocks, 2 * BLOCK, LANES), BF16),
            pltpu.VMEM((n_blocks, 2 * BLOCK, LANES), BF16),
            pltpu.VMEM((2 * BLOCK, 2 * BLOCK), BF16),
            pltpu.VMEM((qt, LANES), F32),
            pltpu.VMEM((qt, 2 * BLOCK), F32),
        ],
        compiler_params=_cparams(("arbitrary", "arbitrary")),
        name="stickbreak_attention",
    )(proj3, proj3, proj3)


def _split_bf16(x):
    hi = x.astype(BF16)
    return hi, (x - hi.astype(F32)).astype(BF16)


def _outproj_kernel(od_ref, os_ref, x_ref, w_ref, gd_ref, gs_ref, gf_ref, wr_ref, br_ref,
                    x2_ref, h2_ref, idx_ref, gate_ref, acc_d, acc_s, ssq_d, ssq_s, *, nk, half_width):
    k = pl.program_id(1)

    @pl.when(k == 0)
    def _():
        acc_d[...] = jnp.zeros_like(acc_d)
        acc_s[...] = jnp.zeros_like(acc_s)
        ssq_d[...] = jnp.zeros_like(ssq_d)
        ssq_s[...] = jnp.zeros_like(ssq_s)

    w = w_ref[...].astype(BF16)

    def accumulate(a_ref, g_ref, acc, ssq):
        a = a_ref[...]
        ssq[...] += jnp.sum(a * a, axis=-1, keepdims=True)
        acc[...] += jnp.dot((a * g_ref[...]).astype(BF16), w, preferred_element_type=F32)

    @pl.when(k < nk // 2)
    def _():
        accumulate(od_ref, gd_ref, acc_d, ssq_d)

    @pl.when(k >= nk // 2)
    def _():
        accumulate(os_ref, gs_ref, acc_s, ssq_s)

    @pl.when(k == nk - 1)
    def _():
        r_d = lax.rsqrt(ssq_d[...] / half_width + NORM_EPS)
        r_s = lax.rsqrt(ssq_s[...] / half_width + NORM_EPS)
        x2 = x_ref[...] + (acc_d[...] * r_d + acc_s[...] * r_s)
        x2_ref[...] = x2
        h2 = (x2 * lax.rsqrt(jnp.mean(x2 * x2, axis=-1, keepdims=True) + NORM_EPS)) * gf_ref[...]
        packed = _pack_pairs(h2)
        n_chunk = packed.shape[1] // LANES
        for ch in range(n_chunk):
            h2_ref[pl.ds(ch, packed.shape[0], stride=n_chunk), :] = packed[:, ch * LANES:(ch + 1) * LANES]
        h_hi, h_lo = _split_bf16(h2)
        w_hi, w_lo = _split_bf16(wr_ref[...])
        logits = (jnp.dot(h_hi, w_hi, preferred_element_type=F32)
                  + jnp.dot(h_hi, w_lo, preferred_element_type=F32)
                  + jnp.dot(h_lo, w_hi, preferred_element_type=F32)) + br_ref[...]
        tm = logits.shape[0]
        e_iota = lax.broadcasted_iota(jnp.int32, logits.shape, 1)
        out_lane = lax.broadcasted_iota(jnp.int32, (tm, LANES), 1)
        idx_out = jnp.zeros((tm, LANES), jnp.int32)
        val_out = jnp.full((tm, LANES), -jnp.inf, F32)
        top0 = None
        for kk in range(TOP_K):
            m = jnp.max(logits, axis=-1, keepdims=True)
            sel = jnp.min(jnp.where(logits == m, e_iota, N_EXPERTS), axis=-1, keepdims=True)
            idx_out = jnp.where(out_lane == kk, sel, idx_out)
            val_out = jnp.where(out_lane == kk, m, val_out)
            logits = jnp.where(e_iota == sel, -jnp.inf, logits)
            if kk == 0:
                top0 = m
        ex = jnp.exp(val_out - top0)
        gate_ref[...] = ex / jnp.sum(ex, axis=-1, keepdims=True)
        idx_ref[...] = idx_out


def _outproj_router(o_d, o_s, x2d, w_out, g_d, g_s, g_f, w_router, b_router, *, tm, tk):
    m, d = x2d.shape
    half_width = d // 2
    nk = d // tk
    assert half_width % tk == 0 and m % tm == 0
    kh = nk // 2
    return pl.pallas_call(
        functools.partial(_outproj_kernel, nk=nk, half_width=half_width),
        grid=(m // tm, nk),
        in_specs=[
            pl.BlockSpec((tm, tk), lambda i, k: (i, jnp.minimum(k, kh - 1))),
            pl.BlockSpec((tm, tk), lambda i, k: (i, jnp.maximum(k - kh, 0))),
            pl.BlockSpec((tm, d), lambda i, k: (i, 0)),
            pl.BlockSpec((tk, d), lambda i, k: (k, 0)),
            pl.BlockSpec((1, tk), lambda i, k: (0, jnp.minimum(k, kh - 1))),
            pl.BlockSpec((1, tk), lambda i, k: (0, jnp.maximum(k - kh, 0))),
            pl.BlockSpec((1, d), lambda i, k: (0, 0)),
            pl.BlockSpec((d, N_EXPERTS), lambda i, k: (0, 0)),
            pl.BlockSpec((1, N_EXPERTS), lambda i, k: (0, 0)),
        ],
        out_specs=[
            pl.BlockSpec((tm, d), lambda i, k: (i, 0)),
            pl.BlockSpec((tm * (d // 2 // LANES), LANES), lambda i, k: (i, 0)),
            pl.BlockSpec((tm, LANES), lambda i, k: (i, 0)),
            pl.BlockSpec((tm, LANES), lambda i, k: (i, 0)),
        ],
        out_shape=[
            jax.ShapeDtypeStruct((m, d), F32),
            jax.ShapeDtypeStruct((m * (d // 2 // LANES), LANES), jnp.uint32),
            jax.ShapeDtypeStruct((m, LANES), jnp.int32),
            jax.ShapeDtypeStruct((m, LANES), F32),
        ],
        scratch_shapes=[
            pltpu.VMEM((tm, d), F32),
            pltpu.VMEM((tm, d), F32),
            pltpu.VMEM((tm, 1), F32),
            pltpu.VMEM((tm, 1), F32),
        ],
        compiler_params=_cparams(("arbitrary", "arbitrary")),
        name="outproj_router",
    )(o_d, o_s, x2d, w_out, g_d, g_s, g_f, w_router, b_router)


def _unpack_pairs(words):
    hi = lax.bitcast_convert_type(words & jnp.uint32(0xFFFF0000), F32)
    lo = lax.bitcast_convert_type(words << 16, F32)
    return hi.astype(BF16), lo.astype(BF16)


def _pack_pairs(x):
    c = x.shape[1] // 2
    as_bits = lambda v: lax.bitcast_convert_type(v.astype(BF16).astype(F32), jnp.uint32)
    return as_bits(x[:, :c]) | (as_bits(x[:, c:]) >> 16)


def _expert_kernel(item_e, item_r0, item_nb, n_items, row_tok,
                   h_hbm, wg_ref, bg_ref, wu_ref, bu_ref, wd_ref, bd_ref,
                   ys_hbm, stage, xb16, acc, zbuf, issued, sem_in, sem_out, *, nj):
    i = pl.program_id(0)
    j = pl.program_id(1)
    n_live = n_items[0]
    active = i < n_live
    r0 = item_r0[i]
    n_align = item_nb[i]
    n_rows = n_align * ROW_ALIGN
    n_mm = (n_rows + ROW_BLOCK - 1) // ROW_BLOCK
    nxt = jnp.minimum(i + 1, pl.num_programs(0) - 1)
    r0_next = item_r0[nxt]
    rows_next = jnp.where(i + 1 < n_live, item_nb[nxt] * ROW_ALIGN, 0)

    n_chunk = xb16.shape[1] // 2 // LANES

    def row_copy(base, r):
        tok = row_tok[base + r]
        src = h_hbm.at[pl.ds(pl.multiple_of(tok * n_chunk, n_chunk), n_chunk)]
        return pltpu.make_async_copy(src, stage.at[pl.ds(pl.multiple_of(r * n_chunk, n_chunk), n_chunk)], sem_in)

    def out_copy(rb):
        src = acc.at[pl.ds(pl.multiple_of(rb * ROW_ALIGN, ROW_ALIGN), ROW_ALIGN)]
        dst = ys_hbm.at[pl.ds(pl.multiple_of(r0 + rb * ROW_ALIGN, ROW_ALIGN), ROW_ALIGN)]
        return pltpu.make_async_copy(src, dst, sem_out)

    @pl.when((i == 0) & (j == 0))
    def _():
        xb16[...] = jnp.zeros_like(xb16)

        def issue(r, c):
            row_copy(r0, r).start()
            return c

        lax.fori_loop(0, n_rows, issue, 0)

    @pl.when(active & (j == 0))
    def _():
        def wait(r, c):
            row_copy(r0, r).wait()
            return c

        lax.fori_loop(0, n_rows, wait, 0)

        def unpack(rb, c):
            rows = pl.ds(pl.multiple_of(rb * ROW_ALIGN, ROW_ALIGN), ROW_ALIGN)
            half = xb16.shape[1] // 2
            for ch in range(n_chunk):
                words = stage[pl.ds(rb * (ROW_ALIGN * n_chunk) + ch, ROW_ALIGN, stride=n_chunk), :]
                hi, lo = _unpack_pairs(words)
                xb16[rows, ch * LANES:(ch + 1) * LANES] = hi
                xb16[rows, half + ch * LANES:half + (ch + 1) * LANES] = lo
            acc[rows, :] = jnp.broadcast_to(bd_ref[0], (ROW_ALIGN, acc.shape[1]))
            return c

        lax.fori_loop(0, n_align, unpack, 0)
        issued[0] = 0

    @pl.when(active)
    def _():
        def mm(rb, c):
            rows = pl.ds(pl.multiple_of(rb * ROW_BLOCK, ROW_BLOCK), ROW_BLOCK)
            xb = xb16[rows, :]
            g = jnp.dot(xb, wg_ref[0].astype(BF16), preferred_element_type=F32) + bg_ref[0]
            u = jnp.dot(xb, wu_ref[0].astype(BF16), preferred_element_type=F32) + bu_ref[0]
            g = jnp.minimum(g, SWIGLU_LIMIT)
            u = jnp.clip(u, -SWIGLU_LIMIT, SWIGLU_LIMIT)
            a = (u + 1.0) * (g * jax.nn.sigmoid(SWIGLU_ALPHA * g))
            acc[rows, :] += jnp.dot(a.astype(BF16), wd_ref[0].astype(BF16), preferred_element_type=F32)

            base = issued[0]

            @pl.when(base < rows_next)
            def _():
                for t in range(GATHER_CHUNK):
                    row_copy(r0_next, base + t).start()

            issued[0] = base + GATHER_CHUNK

            @pl.when(j == nj - 1)
            def _():
                for s in range(ROW_BLOCK // ROW_ALIGN):
                    blk = rb * (ROW_BLOCK // ROW_ALIGN) + s

                    @pl.when(blk < n_align)
                    def _():
                        out_copy(blk).start()

            return c

        lax.fori_loop(0, n_mm, mm, 0)

    @pl.when(active & (j == nj - 1))
    def _():
        def issue_rest(r, c):
            row_copy(r0_next, r).start()
            return c

        lax.fori_loop(issued[0], rows_next, issue_rest, 0)

        def wait(rb, c):
            out_copy(rb).wait()
            return c

        lax.fori_loop(0, n_align, wait, 0)

    @pl.when((i == pl.num_programs(0) - 1) & (j == nj - 1))
    def _():
        zbuf[...] = jnp.zeros_like(zbuf)
        used = n_items[1]

        def tail_copy(t):
            dst = ys_hbm.at[pl.ds(pl.multiple_of((used + t) * ROW_ALIGN, ROW_ALIGN), ROW_ALIGN)]
            return pltpu.make_async_copy(zbuf, dst, sem_out)

        def issue(t, c):
            tail_copy(t).start()
            return c

        n_tail = ys_hbm.shape[0] // ROW_ALIGN - used
        lax.fori_loop(0, n_tail, issue, 0)

        def wait(t, c):
            tail_copy(t).wait()
            return c

        lax.fori_loop(0, n_tail, wait, 0)


def _experts(h2, w_gate, b_gate, w_up, b_up, w_down, b_down, meta, *, n_rows, th):
    item_e, item_r0, item_nb, n_items, row_tok = meta
    n_exp, d, dh = w_gate.shape
    nj = dh // th
    n_item_slots = item_e.shape[0]

    def w_col(i, j, item_e, item_r0, item_nb, n_items, row_tok):
        return (item_e[i], 0, jnp.where(i < n_items[0], j, nj - 1))

    def w_row(i, j, item_e, item_r0, item_nb, n_items, row_tok):
        return (item_e[i], jnp.where(i < n_items[0], j, nj - 1), 0)

    def b_all(i, j, item_e, item_r0, item_nb, n_items, row_tok):
        return (item_e[i], 0, 0)

    grid_spec = pltpu.PrefetchScalarGridSpec(
        num_scalar_prefetch=5,
        grid=(n_item_slots, nj),
        in_specs=[
            pl.BlockSpec(memory_space=pl.ANY),
            pl.BlockSpec((1, d, th), w_col),
            pl.BlockSpec((1, 1, th), w_col),
            pl.BlockSpec((1, d, th), w_col),
            pl.BlockSpec((1, 1, th), w_col),
            pl.BlockSpec((1, th, d), w_row),
            pl.BlockSpec((1, 1, d), b_all),
        ],
        out_specs=pl.BlockSpec(memory_space=pl.ANY),
        scratch_shapes=[
            pltpu.VMEM((ITEM_ROWS * (d // 2 // LANES), LANES), jnp.uint32),
            pltpu.VMEM((ITEM_ROWS, d), BF16),
            pltpu.VMEM((ITEM_ROWS, d), F32),
            pltpu.VMEM((ROW_ALIGN, d), F32),
            pltpu.SMEM((1,), jnp.int32),
            pltpu.SemaphoreType.DMA,
            pltpu.SemaphoreType.DMA,
        ],
    )
    return pl.pallas_call(
        functools.partial(_expert_kernel, nj=nj),
        grid_spec=grid_spec,
        out_shape=jax.ShapeDtypeStruct((n_rows, d), F32),
        compiler_params=_cparams(("arbitrary", "arbitrary")),
        name="experts",
    )(item_e, item_r0, item_nb, n_items, row_tok,
      h2, w_gate, b_gate.reshape(n_exp, 1, dh), w_up, b_up.reshape(n_exp, 1, dh), w_down,
      b_down.reshape(n_exp, 1, d))


def _routing_metadata(top_idx, n_tok):
    n_assign = n_tok * TOP_K
    flat_e = top_idx.reshape(n_assign)
    onehot = (flat_e[:, None] == jnp.arange(N_EXPERTS, dtype=jnp.int32)[None, :]).astype(jnp.int32)
    rank = jnp.take_along_axis(jnp.cumsum(onehot, axis=0), flat_e[:, None], axis=1)[:, 0] - 1
    counts = jnp.sum(onehot, axis=0)
    padded = (counts + ROW_ALIGN - 1) // ROW_ALIGN * ROW_ALIGN
    pend = jnp.cumsum(padded)
    pstart = pend - padded
    dest = (pstart[flat_e] + rank).astype(jnp.int32)
    n_rows = (n_assign // ROW_ALIGN + N_EXPERTS) * ROW_ALIGN
    flat_tok = jnp.arange(n_assign, dtype=jnp.int32) // TOP_K
    row_tok = jnp.zeros((n_rows,), jnp.int32).at[dest].set(flat_tok)
    n_chunks = (padded + ITEM_ROWS - 1) // ITEM_ROWS
    cend = jnp.cumsum(n_chunks)
    cstart = cend - n_chunks
    n_slots = N_EXPERTS + n_assign // ITEM_ROWS + 1
    slot = jnp.arange(n_slots, dtype=jnp.int32)
    n_items = cend[-1]
    last_e = jnp.max(jnp.where(n_chunks > 0, jnp.arange(N_EXPERTS), 0))
    e_of = jnp.minimum(jnp.searchsorted(cend, slot, side='right'), N_EXPERTS - 1)
    live = slot < n_items
    item_e = jnp.where(live, e_of, last_e).astype(jnp.int32)
    chunk = slot - cstart[e_of]
    item_r0 = jnp.where(live, pstart[e_of] + chunk * ITEM_ROWS, 0).astype(jnp.int32)
    item_rows = jnp.clip(padded[e_of] - chunk * ITEM_ROWS, 0, ITEM_ROWS)
    item_nb = jnp.where(live, item_rows // ROW_ALIGN, 0).astype(jnp.int32)
    counts_blk = jnp.stack([n_items, pend[-1] // ROW_ALIGN]).astype(jnp.int32)
    meta = (item_e, item_r0, item_nb, counts_blk, row_tok)
    return meta, dest, n_rows


def _combine_kernel(dest, ys_hbm, x2_ref, gate_ref, g_ref, o_ref, buf, sem, *, tm):
    i = pl.program_id(0)

    def row_copy(t, kk):
        r = dest[(i * tm + t) * TOP_K + kk]
        return pltpu.make_async_copy(ys_hbm.at[pl.ds(r, 1)], buf.at[kk, pl.ds(t, 1)], sem)

    def issue(t, c):
        for kk in range(TOP_K):
            row_copy(t, kk).start()
        return c

    lax.fori_loop(0, tm, issue, 0)

    def wait(t, c):
        for kk in range(TOP_K):
            row_copy(t, kk).wait()
        return c

    lax.fori_loop(0, tm, wait, 0)

    gates = gate_ref[...]
    y = x2_ref[...]
    for kk in range(TOP_K):
        y = y + gates[:, kk:kk + 1] * buf[kk]
    o_ref[...] = (y * lax.rsqrt(jnp.mean(y * y, axis=-1, keepdims=True) + NORM_EPS)) * g_ref[...]


def _combine(dest, ys, x2, gates, g_final, *, tm):
    m, d = x2.shape
    grid_spec = pltpu.PrefetchScalarGridSpec(
        num_scalar_prefetch=1,
        grid=(m // tm,),
        in_specs=[
            pl.BlockSpec(memory_space=pl.ANY),
            pl.BlockSpec((tm, d), lambda i, dest: (i, 0)),
            pl.BlockSpec((tm, LANES), lambda i, dest: (i, 0)),
            pl.BlockSpec((1, d), lambda i, dest: (0, 0)),
        ],
        out_specs=pl.BlockSpec((tm, d), lambda i, dest: (i, 0)),
        scratch_shapes=[pltpu.VMEM((TOP_K, tm, d), F32), pltpu.SemaphoreType.DMA],
    )
    return pl.pallas_call(
        functools.partial(_combine_kernel, tm=tm),
        grid_spec=grid_spec,
        out_shape=jax.ShapeDtypeStruct((m, d), F32),
        compiler_params=_cparams(("arbitrary",)),
        name="combine",
    )(dest, ys, x2, gates, g_final)


def _layer(x, attn_norm, w_in, dil_out_norm, sb_out_norm, w_out, ffn_norm,
           w_router, b_router, w_gate, b_gate, w_up, b_up, w_down, b_down, *, tiles):
    b, seq, d = x.shape
    m = b * seq
    half_width = d // 2
    x2d = x.reshape(m, d)
    cos_t, sin_t = _rope_tables(seq)
    proj = _inproj(x2d, attn_norm.reshape(1, d), w_in, cos_t, sin_t,
                   seq=seq, tm=tiles["in_tm"], tn=tiles["in_tn"])
    proj3 = proj.reshape(b, seq, 3 * d)
    o_d = _dilated_attention(proj3, half_width=half_width).reshape(m, half_width)
    o_s = _stickbreak_attention(proj3, half_width=half_width, qt=tiles["sb_qt"]).reshape(m, half_width)
    x2, h2, idx_l, gate_l = _outproj_router(
        o_d, o_s, x2d, w_out, dil_out_norm.reshape(1, half_width), sb_out_norm.reshape(1, half_width),
        ffn_norm.reshape(1, d), w_router, b_router.reshape(1, N_EXPERTS),
        tm=tiles["out_tm"], tk=tiles["out_tk"])
    meta, dest, n_rows = _routing_metadata(idx_l[:, :TOP_K], m)
    ys = _experts(h2, w_gate, b_gate, w_up, b_up, w_down, b_down, meta, n_rows=n_rows, th=tiles["moe_th"])
    return x2, ys, dest, gate_l


_TILES = dict(in_tm=1024, in_tn=512, sb_qt=512, out_tm=512, out_tk=512, moe_th=512, cmb_tm=128)


def kernel(x, attn_norm, w_in, dil_out_norm, sb_out_norm, w_out, ffn_norm, w_router, b_router,
           w_gate, b_gate, w_up, b_up, w_down, b_down, final_norm):
    depth = attn_norm.shape[0]
    b, seq, d = x.shape
    for l in range(depth):
        x2, ys, dest, gate_l = _layer(
            x, attn_norm[l], w_in[l], dil_out_norm[l], sb_out_norm[l], w_out[l], ffn_norm[l],
            w_router[l], b_router[l], w_gate[l], b_gate[l], w_up[l], b_up[l], w_down[l], b_down[l],
            tiles=_TILES)
        assert depth == 1
        x = _combine(dest, ys, x2, gate_l, final_norm.reshape(1, d), tm=_TILES["cmb_tm"])
    return x.reshape(b, seq, d)
```

```python
import functools
import math

import jax
import jax.numpy as jnp
from jax import lax
from jax.experimental import pallas as pl
from jax.experimental.pallas import tpu as pltpu

F32 = jnp.float32
BF16 = jnp.bfloat16

HEAD_DIM = 64
LANES = 128
BLOCK = 128
DIL_PATTERNS = ((128, 1), (512, 4), (2048, 16))
ROPE_THETA = 10000.0
N_EXPERTS = 32
TOP_K = 4
SWIGLU_LIMIT = 7.0
SWIGLU_ALPHA = 1.702
NORM_EPS = 1e-5
Q_SCALE = HEAD_DIM ** -0.5 * math.log2(math.e)
GATHER_CHUNK = 128
DIL_UNROLL = 4
ROW_ALIGN = 128
ROW_BLOCK = 512
ITEM_ROWS = 1280
VMEM_LIMIT = 56 * 1024 * 1024


def _cparams(sem):
    return pltpu.CompilerParams(dimension_semantics=sem, vmem_limit_bytes=VMEM_LIMIT)


def _inproj_kernel(x_ref, g_ref, w_ref, cos_ref, sin_ref, o_ref, h_scr, *, tn, half_width):
    j = pl.program_id(1)

    @pl.when(j == 0)
    def _():
        x = x_ref[...]
        y = x * lax.rsqrt(jnp.mean(x * x, axis=-1, keepdims=True) + NORM_EPS)
        h_scr[...] = (y * g_ref[...]).astype(BF16)

    acc = jnp.dot(h_scr[...], w_ref[...].astype(BF16), preferred_element_type=F32)
    section = (j * tn) // half_width
    scale = jnp.where((section == 0) | (section == 3), Q_SCALE, 1.0).astype(F32)

    @pl.when(section < 2)
    def _():
        lane = lax.broadcasted_iota(jnp.int32, (acc.shape[0], LANES), 1)
        first_half = (lane % HEAD_DIM) < (HEAD_DIM // 2)
        c = cos_ref[...]
        s = sin_ref[...]
        for g in range(tn // LANES):
            xg = acc[:, g * LANES:(g + 1) * LANES]
            partner = jnp.where(first_half,
                                pltpu.roll(xg, LANES - HEAD_DIM // 2, 1),
                                pltpu.roll(xg, HEAD_DIM // 2, 1))
            o_ref[:, g * LANES:(g + 1) * LANES] = ((xg * c + partner * s) * scale).astype(BF16)

    @pl.when(section >= 2)
    def _():
        o_ref[...] = (acc * scale).astype(BF16)


def _inproj(x2d, g, w_in, cos_t, sin_t, *, seq, tm, tn):
    m, d = x2d.shape
    n = w_in.shape[1]
    half_width = d // 2
    assert half_width % tn == 0 and tn % LANES == 0 and m % tm == 0 and seq % tm == 0
    pos_tiles = seq // tm
    return pl.pallas_call(
        functools.partial(_inproj_kernel, tn=tn, half_width=half_width),
        grid=(m // tm, n // tn),
        in_specs=[
            pl.BlockSpec((tm, d), lambda i, j: (i, 0)),
            pl.BlockSpec((1, d), lambda i, j: (0, 0)),
            pl.BlockSpec((d, tn), lambda i, j: (0, j)),
            pl.BlockSpec((tm, LANES), lambda i, j: (i % pos_tiles, 0)),
            pl.BlockSpec((tm, LANES), lambda i, j: (i % pos_tiles, 0)),
        ],
        out_specs=pl.BlockSpec((tm, tn), lambda i, j: (i, j)),
        out_shape=jax.ShapeDtypeStruct((m, n), BF16),
        scratch_shapes=[pltpu.VMEM((tm, d), BF16)],
        compiler_params=_cparams(("arbitrary", "arbitrary")),
        name="inproj",
    )(x2d, g, w_in, cos_t, sin_t)


def _rope_tables(seq):
    half = HEAD_DIM // 2
    inv_freq = 1.0 / (ROPE_THETA ** (jnp.arange(half, dtype=F32) * 2.0 / HEAD_DIM))
    ang = jnp.arange(seq, dtype=F32)[:, None] * inv_freq[None, :]
    cos = jnp.cos(ang)
    sin = jnp.sin(ang)
    reps = LANES // HEAD_DIM
    cos_t = jnp.tile(jnp.concatenate([cos, cos], axis=1), (1, reps))
    sin_t = jnp.tile(jnp.concatenate([-sin, sin], axis=1), (1, reps))
    return cos_t, sin_t


def _dil_kernel(q_ref, k_ref, v_ref, o_ref, f32_scr, qp, kp, vp, operm, lperm, onat, lnat, *, seq):
    n_blocks = seq // BLOCK
    lane_k = lax.broadcasted_iota(jnp.int32, (2 * BLOCK, LANES), 1)
    head_a_k = lane_k < HEAD_DIM
    lane_q = lax.broadcasted_iota(jnp.int32, (BLOCK, LANES), 1)
    head_a_q = lane_q < HEAD_DIM
    qi = lax.broadcasted_iota(jnp.int32, (BLOCK, 2 * BLOCK), 0)
    ki = lax.broadcasted_iota(jnp.int32, (BLOCK, 2 * BLOCK), 1)
    dist = qi + BLOCK - ki
    band = (dist >= 0) & (dist <= BLOCK)
    cur_only = ki >= BLOCK
    zero_b = jnp.zeros((), BF16)

    f32_scr[0] = q_ref[0].astype(F32)
    f32_scr[1] = k_ref[0].astype(F32)
    f32_scr[2] = v_ref[0].astype(F32)
    kp[0:BLOCK, :] = jnp.zeros((BLOCK, LANES), BF16)
    vp[0:BLOCK, :] = jnp.zeros((BLOCK, LANES), BF16)

    for p, (window, dil) in enumerate(DIL_PATTERNS):
        assert window // dil == BLOCK
        length = seq // dil
        per_stream = length // BLOCK
        assert length % BLOCK == 0
        if dil == 1:
            qp[...] = q_ref[0]
            kp[BLOCK:, :] = k_ref[0]
            vp[BLOCK:, :] = v_ref[0]
        else:
            for r in range(dil):
                rows = pl.ds(r, length, stride=dil)
                qp[r * length:(r + 1) * length, :] = f32_scr[0, rows, :].astype(BF16)
                kp[BLOCK + r * length:BLOCK + (r + 1) * length, :] = f32_scr[1, rows, :].astype(BF16)
                vp[BLOCK + r * length:BLOCK + (r + 1) * length, :] = f32_scr[2, rows, :].astype(BF16)
        o_dst = onat.at[p] if dil == 1 else operm
        l_dst = lnat.at[p] if dil == 1 else lperm

        def block_body(bi, carry, per_stream=per_stream, o_dst=o_dst, l_dst=l_dst):
            row0 = pl.multiple_of(bi * BLOCK, BLOCK)
            qb = qp[pl.ds(row0, BLOCK), :]
            kw = kp[pl.ds(row0, 2 * BLOCK), :]
            vw = vp[pl.ds(row0, 2 * BLOCK), :]
            has_prev = (bi % per_stream) != 0
            valid = band & (cur_only | has_prev)
            kk = jnp.concatenate([jnp.where(head_a_k, kw, zero_b), jnp.where(head_a_k, zero_b, kw)], axis=0)
            s2 = lax.dot_general(qb, kk, (((1,), (1,)), ((), ())), preferred_element_type=F32)
            ps, ms, dens = [], [], []
            for h in range(2):
                s = jnp.where(valid, s2[:, h * 2 * BLOCK:(h + 1) * 2 * BLOCK], -jnp.inf)
                m = jnp.max(s, axis=-1, keepdims=True)
                e = jnp.exp2(s - m)
                ps.append(e.astype(BF16))
                ms.append(m)
                dens.append(jnp.sum(e, axis=-1, keepdims=True))
            vv = jnp.concatenate([jnp.where(head_a_k, vw, zero_b), jnp.where(head_a_k, zero_b, vw)], axis=0)
            o = jnp.dot(jnp.concatenate(ps, axis=1), vv, preferred_element_type=F32)
            den = jnp.where(head_a_q, dens[0], dens[1])
            o_dst[pl.ds(row0, BLOCK), :] = o / den
            l_dst[pl.ds(row0, BLOCK), :] = jnp.where(head_a_q, ms[0], ms[1]) + jnp.log2(den)
            return carry

        lax.fori_loop(0, n_blocks, block_body, 0, unroll=DIL_UNROLL)
        if dil != 1:
            for r in range(dil):
                rows = pl.ds(r, length, stride=dil)
                onat[p, rows, :] = operm[r * length:(r + 1) * length, :]
                lnat[p, rows, :] = lperm[r * length:(r + 1) * length, :]

    def mix_body(c, carry):
        rows = pl.ds(pl.multiple_of(c * BLOCK, BLOCK), BLOCK)
        ls = [lnat[p, rows, :] for p in range(len(DIL_PATTERNS))]
        m = functools.reduce(jnp.maximum, ls)
        es = [jnp.exp2(l - m) for l in ls]
        num = functools.reduce(lambda a, b: a + b, [e * onat[p, rows, :] for p, e in enumerate(es)])
        o_ref[0, rows, :] = num / functools.reduce(lambda a, b: a + b, es)
        return carry

    lax.fori_loop(0, n_blocks, mix_body, 0)


def _dilated_attention(proj3, *, half_width):
    b, seq, _ = proj3.shape
    pairs = half_width // LANES
    n_pat = len(DIL_PATTERNS)
    spec = lambda off: pl.BlockSpec((1, seq, LANES), lambda bi, c, off=off: (bi, 0, off + c))
    return pl.pallas_call(
        functools.partial(_dil_kernel, seq=seq),
        grid=(b, pairs),
        in_specs=[spec(0), spec(pairs), spec(2 * pairs)],
        out_specs=pl.BlockSpec((1, seq, LANES), lambda bi, c: (bi, 0, c)),
        out_shape=jax.ShapeDtypeStruct((b, seq, half_width), F32),
        scratch_shapes=[
            pltpu.VMEM((3, seq, LANES), F32),
            pltpu.VMEM((seq, LANES), BF16),
            pltpu.VMEM((seq + BLOCK, LANES), BF16),
            pltpu.VMEM((seq + BLOCK, LANES), BF16),
            pltpu.VMEM((seq, LANES), F32),
            pltpu.VMEM((seq, LANES), F32),
            pltpu.VMEM((n_pat, seq, LANES), F32),
            pltpu.VMEM((n_pat, seq, LANES), F32),
        ],
        compiler_params=_cparams(("arbitrary", "arbitrary")),
        name="dilated_attention",
    )(proj3, proj3, proj3)


def _sb_kernel(q_ref, k_ref, v_ref, o_ref, kk_scr, vv_scr, tri_scr, acc_scr, carry_scr, *, seq, qt):
    n_blocks = seq // BLOCK
    per_tile = qt // BLOCK
    lane_k = lax.broadcasted_iota(jnp.int32, (BLOCK, LANES), 1)
    head_a = lane_k < HEAD_DIM
    zero_b = jnp.zeros((), BF16)

    def prep(jb, carry):
        rows = pl.ds(pl.multiple_of(jb * BLOCK, BLOCK), BLOCK)
        kb = k_ref[0, rows, :]
        vb = v_ref[0, rows, :]
        kk_scr[jb, 0:BLOCK, :] = jnp.where(head_a, kb, zero_b)
        kk_scr[jb, BLOCK:, :] = jnp.where(head_a, zero_b, kb)
        vv_scr[jb, 0:BLOCK, :] = jnp.where(head_a, vb, zero_b)
        vv_scr[jb, BLOCK:, :] = jnp.where(head_a, zero_b, vb)
        return carry

    lax.fori_loop(0, n_blocks, prep, 0)

    jj = lax.broadcasted_iota(jnp.int32, (2 * BLOCK, 2 * BLOCK), 0) % BLOCK
    ss = lax.broadcasted_iota(jnp.int32, (2 * BLOCK, 2 * BLOCK), 1)
    tri_scr[...] = jnp.where((ss >= BLOCK) | (jj > ss), 1.0, 0.0).astype(BF16)

    row = lax.broadcasted_iota(jnp.int32, (qt, 2 * BLOCK), 0)
    col = lax.broadcasted_iota(jnp.int32, (qt, 2 * BLOCK), 1) % BLOCK

    def key_block(qb, jb, key_off):
        z = lax.dot_general(qb, kk_scr[jb], (((1,), (1,)), ((), ())), preferred_element_type=F32)
        neg_abs = lax.bitcast_convert_type(lax.bitcast_convert_type(z, jnp.uint32) | jnp.uint32(0x80000000), F32)
        ls = jnp.minimum(z, 0.0) - jnp.log2(1.0 + jnp.exp2(neg_abs))
        lr = ls - z
        if key_off is not None:
            causal = (col + key_off) < row
            lr = jnp.where(causal, lr, 0.0)
        hi = lr.astype(BF16)
        lo = (lr - hi.astype(F32)).astype(BF16)
        tri = tri_scr[...]
        a_heads = []
        for h in range(2):
            sl = slice(h * BLOCK, (h + 1) * BLOCK)
            lhs = jnp.concatenate([hi[:, sl], lo[:, sl]], axis=1)
            cs_tot = jnp.dot(lhs, tri, preferred_element_type=F32)
            carry = carry_scr[:, sl]
            a = jnp.exp2(ls[:, sl] + cs_tot[:, :BLOCK] + carry)
            if key_off is not None:
                a = jnp.where(causal[:, :BLOCK], a, 0.0)
            carry_scr[:, sl] = carry + cs_tot[:, BLOCK:]
            a_heads.append(a.astype(BF16))
        acc_scr[...] += jnp.dot(jnp.concatenate(a_heads, axis=1), vv_scr[jb], preferred_element_type=F32)

    def q_tile(i, carry):
        rows = pl.ds(pl.multiple_of(i * qt, qt), qt)
        qb = q_ref[0, rows, :]
        acc_scr[...] = jnp.zeros_like(acc_scr)
        carry_scr[...] = jnp.zeros_like(carry_scr)
        for t in reversed(range(per_tile)):
            key_block(qb, i * per_tile + t, t * BLOCK)

        def inner(t, c):
            key_block(qb, i * per_tile - 1 - t, None)
            return c

        lax.fori_loop(0, i * per_tile, inner, 0)
        o_ref[0, rows, :] = acc_scr[...]
        return carry

    lax.fori_loop(0, seq // qt, q_tile, 0)


def _stickbreak_attention(proj3, *, half_width, qt):
    b, seq, _ = proj3.shape
    pairs = half_width // LANES
    n_blocks = seq // BLOCK
    assert seq % qt == 0 and qt % BLOCK == 0
    spec = lambda off: pl.BlockSpec((1, seq, LANES), lambda bi, c, off=off: (bi, 0, off + c))
    return pl.pallas_call(
        functools.partial(_sb_kernel, seq=seq, qt=qt),
        grid=(b, pairs),
        in_specs=[spec(3 * pairs), spec(4 * pairs), spec(5 * pairs)],
        out_specs=pl.BlockSpec((1, seq, LANES), lambda bi, c: (bi, 0, c)),
        out_shape=jax.ShapeDtypeStruct((b, seq, half_width), F32),
        scratch_shapes=[
            pltpu.VMEM((n_blocks, 2 * BLOCK, LANES), BF16),
            pltpu.VMEM((n_blocks, 2 * BLOCK, LANES), BF16),
            pltpu.VMEM((2 * BLOCK, 2 * BLOCK), BF16),
            pltpu.VMEM((qt, LANES), F32),
            pltpu.VMEM((qt, 2 * BLOCK), F32),
        ],
        compiler_params=_cparams(("arbitrary", "arbitrary")),
        name="stickbreak_attention",
    )(proj3, proj3, proj3)


def _split_bf16(x):
    hi = x.astype(BF16)
    return hi, (x - hi.astype(F32)).astype(BF16)


def _outproj_kernel(od_ref, os_ref, x_ref, w_ref, gd_ref, gs_ref, gf_ref, wr_ref, br_ref,
                    x2_ref, h2_ref, idx_ref, gate_ref, acc_d, acc_s, ssq_d, ssq_s, *, nk, half_width):
    k = pl.program_id(1)

    @pl.when(k == 0)
    def _():
        acc_d[...] = jnp.zeros_like(acc_d)
        acc_s[...] = jnp.zeros_like(acc_s)
        ssq_d[...] = jnp.zeros_like(ssq_d)
        ssq_s[...] = jnp.zeros_like(ssq_s)

    w = w_ref[...].astype(BF16)

    def accumulate(a_ref, g_ref, acc, ssq):
        a = a_ref[...]
        ssq[...] += jnp.sum(a * a, axis=-1, keepdims=True)
        acc[...] += jnp.dot((a * g_ref[...]).astype(BF16), w, preferred_element_type=F32)

    @pl.when(k < nk // 2)
    def _():
        accumulate(od_ref, gd_ref, acc_d, ssq_d)

    @pl.when(k >= nk // 2)
    def _():
        accumulate(os_ref, gs_ref, acc_s, ssq_s)

    @pl.when(k == nk - 1)
    def _():
        r_d = lax.rsqrt(ssq_d[...] / half_width + NORM_EPS)
        r_s = lax.rsqrt(ssq_s[...] / half_width + NORM_EPS)
        x2 = x_ref[...] + (acc_d[...] * r_d + acc_s[...] * r_s)
        x2_ref[...] = x2
        h2 = (x2 * lax.rsqrt(jnp.mean(x2 * x2, axis=-1, keepdims=True) + NORM_EPS)) * gf_ref[...]
        packed = _pack_pairs(h2)
        n_chunk = packed.shape[1] // LANES
        for ch in range(n_chunk):
            h2_ref[pl.ds(ch, packed.shape[0], stride=n_chunk), :] = packed[:, ch * LANES:(ch + 1) * LANES]
        h_hi, h_lo = _split_bf16(h2)
        w_hi, w_lo = _split_bf16(wr_ref[...])
        logits = (jnp.dot(h_hi, w_hi, preferred_element_type=F32)
                  + jnp.dot(h_hi, w_lo, preferred_element_type=F32)
                  + jnp.dot(h_lo, w_hi, preferred_element_type=F32)) + br_ref[...]
        tm = logits.shape[0]
        e_iota = lax.broadcasted_iota(jnp.int32, logits.shape, 1)
        out_lane = lax.broadcasted_iota(jnp.int32, (tm, LANES), 1)
        idx_out = jnp.zeros((tm, LANES), jnp.int32)
        val_out = jnp.full((tm, LANES), -jnp.inf, F32)
        top0 = None
        for kk in range(TOP_K):
            m = jnp.max(logits, axis=-1, keepdims=True)
            sel = jnp.min(jnp.where(logits == m, e_iota, N_EXPERTS), axis=-1, keepdims=True)
            idx_out = jnp.where(out_lane == kk, sel, idx_out)
            val_out = jnp.where(out_lane == kk, m, val_out)
            logits = jnp.where(e_iota == sel, -jnp.inf, logits)
            if kk == 0:
                top0 = m
        ex = jnp.exp(val_out - top0)
        gate_ref[...] = ex / jnp.sum(ex, axis=-1, keepdims=True)
        idx_ref[...] = idx_out


def _outproj_router(o_d, o_s, x2d, w_out, g_d, g_s, g_f, w_router, b_router, *, tm, tk):
    m, d = x2d.shape
    half_width = d // 2
    nk = d // tk
    assert half_width % tk == 0 and m % tm == 0
    kh = nk // 2
    return pl.pallas_call(
        functools.partial(_outproj_kernel, nk=nk, half_width=half_width),
        grid=(m // tm, nk),
        in_specs=[
            pl.BlockSpec((tm, tk), lambda i, k: (i, jnp.minimum(k, kh - 1))),
            pl.BlockSpec((tm, tk), lambda i, k: (i, jnp.maximum(k - kh, 0))),
            pl.BlockSpec((tm, d), lambda i, k: (i, 0)),
            pl.BlockSpec((tk, d), lambda i, k: (k, 0)),
            pl.BlockSpec((1, tk), lambda i, k: (0, jnp.minimum(k, kh - 1))),
            pl.BlockSpec((1, tk), lambda i, k: (0, jnp.maximum(k - kh, 0))),
            pl.BlockSpec((1, d), lambda i, k: (0, 0)),
            pl.BlockSpec((d, N_EXPERTS), lambda i, k: (0, 0)),
            pl.BlockSpec((1, N_EXPERTS), lambda i, k: (0, 0)),
        ],
        out_specs=[
            pl.BlockSpec((tm, d), lambda i, k: (i, 0)),
            pl.BlockSpec((tm * (d // 2 // LANES), LANES), lambda i, k: (i, 0)),
            pl.BlockSpec((tm, LANES), lambda i, k: (i, 0)),
            pl.BlockSpec((tm, LANES), lambda i, k: (i, 0)),
        ],
        out_shape=[
            jax.ShapeDtypeStruct((m, d), F32),
            jax.ShapeDtypeStruct((m * (d // 2 // LANES), LANES), jnp.uint32),
            jax.ShapeDtypeStruct((m, LANES), jnp.int32),
            jax.ShapeDtypeStruct((m, LANES), F32),
        ],
        scratch_shapes=[
            pltpu.VMEM((tm, d), F32),
            pltpu.VMEM((tm, d), F32),
            pltpu.VMEM((tm, 1), F32),
            pltpu.VMEM((tm, 1), F32),
        ],
        compiler_params=_cparams(("arbitrary", "arbitrary")),
        name="outproj_router",
    )(o_d, o_s, x2d, w_out, g_d, g_s, g_f, w_router, b_router)


def _unpack_pairs(words):
    hi = lax.bitcast_convert_type(words & jnp.uint32(0xFFFF0000), F32)
    lo = lax.bitcast_convert_type(words << 16, F32)
    return hi.astype(BF16), lo.astype(BF16)


def _pack_pairs(x):
    c = x.shape[1] // 2
    as_bits = lambda v: lax.bitcast_convert_type(v.astype(BF16).astype(F32), jnp.uint32)
    return as_bits(x[:, :c]) | (as_bits(x[:, c:]) >> 16)


def _expert_kernel(item_e, item_r0, item_nb, n_items, row_tok,
                   h_hbm, wg_ref, bg_ref, wu_ref, bu_ref, wd_ref, bd_ref,
                   ys_hbm, stage, xb16, acc, issued, pending, sem_in, sem_out, *, nj):
    i = pl.program_id(0)
    j = pl.program_id(1)
    n_live = n_items[0]
    active = i < n_live
    r0 = item_r0[i]
    n_align = item_nb[i]
    n_rows = n_align * ROW_ALIGN
    per_trip = ROW_BLOCK // ROW_ALIGN
    n_main = n_align // per_trip
    nxt = jnp.minimum(i + 1, pl.num_programs(0) - 1)
    r0_next = item_r0[nxt]
    rows_next = jnp.where(i + 1 < n_live, item_nb[nxt] * ROW_ALIGN, 0)
    d = acc.shape[1]
    n_chunk = d // 2 // LANES

    def row_copy(base, r):
        tok = row_tok[base + r]
        src = h_hbm.at[pl.ds(pl.multiple_of(tok * n_chunk, n_chunk), n_chunk)]
        return pltpu.make_async_copy(src, stage.at[pl.ds(pl.multiple_of(r * n_chunk, n_chunk), n_chunk)], sem_in)

    def out_copy(blk):
        src = acc.at[pl.ds(pl.multiple_of(blk * ROW_ALIGN, ROW_ALIGN), ROW_ALIGN)]
        dst = ys_hbm.at[pl.ds(pl.multiple_of(r0 + blk * ROW_ALIGN, ROW_ALIGN), ROW_ALIGN)]
        return pltpu.make_async_copy(src, dst, sem_out)

    def drain():
        def wait(t, c):
            out_copy(0).wait()
            return c

        lax.fori_loop(0, pending[0], wait, 0)
        pending[0] = 0

    @pl.when((i == 0) & (j == 0))
    def _():
        pending[0] = 0

        def issue(r, c):
            row_copy(r0, r).start()
            return c

        lax.fori_loop(0, n_rows, issue, 0)

    @pl.when(active & (j == 0))
    def _():
        def wait(r, c):
            row_copy(r0, r).wait()
            return c

        lax.fori_loop(0, n_rows, wait, 0)
        drain()

        def unpack(rb, c):
            rows = pl.ds(pl.multiple_of(rb * ROW_ALIGN, ROW_ALIGN), ROW_ALIGN)
            for ch in range(n_chunk):
                words = stage[pl.ds(rb * (ROW_ALIGN * n_chunk) + ch, ROW_ALIGN, stride=n_chunk), :]
                hi, lo = _unpack_pairs(words)
                xb16[rows, ch * LANES:(ch + 1) * LANES] = hi
                xb16[rows, d // 2 + ch * LANES:d // 2 + (ch + 1) * LANES] = lo
            acc[rows, :] = jnp.broadcast_to(bd_ref[0], (ROW_ALIGN, d))
            return c

        lax.fori_loop(0, n_align, unpack, 0)
        issued[0] = 0

    def trip(row0, n_blk):
        rows = pl.ds(row0, n_blk * ROW_ALIGN)
        xb = xb16[rows, :]
        g = jnp.dot(xb, wg_ref[0].astype(BF16), preferred_element_type=F32) + bg_ref[0]
        u = jnp.dot(xb, wu_ref[0].astype(BF16), preferred_element_type=F32) + bu_ref[0]
        g = jnp.minimum(g, SWIGLU_LIMIT)
        u = jnp.clip(u, -SWIGLU_LIMIT, SWIGLU_LIMIT)
        a = (u + 1.0) * (g * jax.nn.sigmoid(SWIGLU_ALPHA * g))
        acc[rows, :] += jnp.dot(a.astype(BF16), wd_ref[0].astype(BF16), preferred_element_type=F32)

        base = issued[0]

        @pl.when(base < rows_next)
        def _():
            for t in range(GATHER_CHUNK):
                row_copy(r0_next, base + t).start()

        issued[0] = base + GATHER_CHUNK

        @pl.when(j == nj - 1)
        def _():
            for s in range(n_blk):
                out_copy(row0 // ROW_ALIGN + s).start()
            pending[0] = pending[0] + n_blk

    @pl.when(active)
    def _():
        def main_trip(rb, c):
            trip(pl.multiple_of(rb * ROW_BLOCK, ROW_BLOCK), per_trip)
            return c

        lax.fori_loop(0, n_main, main_trip, 0)

        def tail_trip(t, c):
            trip(pl.multiple_of((n_main * per_trip + t) * ROW_ALIGN, ROW_ALIGN), 1)
            return c

        lax.fori_loop(0, n_align - n_main * per_trip, tail_trip, 0)

    @pl.when(active & (j == nj - 1))
    def _():
        def issue_rest(r, c):
            row_copy(r0_next, r).start()
            return c

        lax.fori_loop(issued[0], rows_next, issue_rest, 0)

    @pl.when((i == pl.num_programs(0) - 1) & (j == nj - 1))
    def _():
        drain()
        acc[0:ROW_ALIGN, :] = jnp.zeros((ROW_ALIGN, d), F32)
        used = n_items[1]

        def tail_copy(t):
            dst = ys_hbm.at[pl.ds(pl.multiple_of((used + t) * ROW_ALIGN, ROW_ALIGN), ROW_ALIGN)]
            return pltpu.make_async_copy(acc.at[pl.ds(0, ROW_ALIGN)], dst, sem_out)

        def issue(t, c):
            tail_copy(t).start()
            return c

        n_tail = ys_hbm.shape[0] // ROW_ALIGN - used
        lax.fori_loop(0, n_tail, issue, 0)

        def wait(t, c):
            tail_copy(t).wait()
            return c

        lax.fori_loop(0, n_tail, wait, 0)


def _experts(h2, w_gate, b_gate, w_up, b_up, w_down, b_down, meta, *, n_rows, th):
    item_e, item_r0, item_nb, n_items, row_tok = meta
    n_exp, d, dh = w_gate.shape
    nj = dh // th
    n_item_slots = item_e.shape[0]

    def w_col(i, j, item_e, item_r0, item_nb, n_items, row_tok):
        return (item_e[i], 0, jnp.where(i < n_items[0], j, nj - 1))

    def w_row(i, j, item_e, item_r0, item_nb, n_items, row_tok):
        return (item_e[i], jnp.where(i < n_items[0], j, nj - 1), 0)

    def b_all(i, j, item_e, item_r0, item_nb, n_items, row_tok):
        return (item_e[i], 0, 0)

    grid_spec = pltpu.PrefetchScalarGridSpec(
        num_scalar_prefetch=5,
        grid=(n_item_slots, nj),
        in_specs=[
            pl.BlockSpec(memory_space=pl.ANY),
            pl.BlockSpec((1, d, th), w_col),
            pl.BlockSpec((1, 1, th), w_col),
            pl.BlockSpec((1, d, th), w_col),
            pl.BlockSpec((1, 1, th), w_col),
            pl.BlockSpec((1, th, d), w_row),
            pl.BlockSpec((1, 1, d), b_all),
        ],
        out_specs=pl.BlockSpec(memory_space=pl.ANY),
        scratch_shapes=[
            pltpu.VMEM((ITEM_ROWS * (d // 2 // LANES), LANES), jnp.uint32),
            pltpu.VMEM((ITEM_ROWS, d), BF16),
            pltpu.VMEM((ITEM_ROWS, d), F32),
            pltpu.SMEM((1,), jnp.int32),
            pltpu.SMEM((1,), jnp.int32),
            pltpu.SemaphoreType.DMA,
            pltpu.SemaphoreType.DMA,
        ],
    )
    return pl.pallas_call(
        functools.partial(_expert_kernel, nj=nj),
        grid_spec=grid_spec,
        out_shape=jax.ShapeDtypeStruct((n_rows, d), F32),
        compiler_params=_cparams(("arbitrary", "arbitrary")),
        name="experts",
    )(item_e, item_r0, item_nb, n_items, row_tok,
      h2, w_gate, b_gate.reshape(n_exp, 1, dh), w_up, b_up.reshape(n_exp, 1, dh), w_down,
      b_down.reshape(n_exp, 1, d))


def _routing_metadata(top_idx, n_tok):
    n_assign = n_tok * TOP_K
    flat_e = top_idx.reshape(n_assign)
    onehot = (flat_e[:, None] == jnp.arange(N_EXPERTS, dtype=jnp.int32)[None, :]).astype(jnp.int32)
    rank = jnp.take_along_axis(jnp.cumsum(onehot, axis=0), flat_e[:, None], axis=1)[:, 0] - 1
    counts = jnp.sum(onehot, axis=0)
    padded = (counts + ROW_ALIGN - 1) // ROW_ALIGN * ROW_ALIGN
    pend = jnp.cumsum(padded)
    pstart = pend - padded
    dest = (pstart[flat_e] + rank).astype(jnp.int32)
    n_rows = (n_assign // ROW_ALIGN + N_EXPERTS) * ROW_ALIGN
    flat_tok = jnp.arange(n_assign, dtype=jnp.int32) // TOP_K
    row_tok = jnp.zeros((n_rows,), jnp.int32).at[dest].set(flat_tok)
    n_chunks = (padded + ITEM_ROWS - 1) // ITEM_ROWS
    cend = jnp.cumsum(n_chunks)
    cstart = cend - n_chunks
    n_slots = N_EXPERTS + n_assign // ITEM_ROWS + 1
    slot = jnp.arange(n_slots, dtype=jnp.int32)
    n_items = cend[-1]
    last_e = jnp.max(jnp.where(n_chunks > 0, jnp.arange(N_EXPERTS), 0))
    e_of = jnp.minimum(jnp.searchsorted(cend, slot, side='right'), N_EXPERTS - 1)
    live = slot < n_items
    item_e = jnp.where(live, e_of, last_e).astype(jnp.int32)
    chunk = slot - cstart[e_of]
    item_r0 = jnp.where(live, pstart[e_of] + chunk * ITEM_ROWS, 0).astype(jnp.int32)
    item_rows = jnp.clip(padded[e_of] - chunk * ITEM_ROWS, 0, ITEM_ROWS)
    item_nb = jnp.where(live, item_rows // ROW_ALIGN, 0).astype(jnp.int32)
    counts_blk = jnp.stack([n_items, pend[-1] // ROW_ALIGN]).astype(jnp.int32)
    meta = (item_e, item_r0, item_nb, counts_blk, row_tok)
    return meta, dest, n_rows


def _combine_kernel(dest, ys_hbm, x2_ref, gate_ref, g_ref, o_ref, buf, sem, *, tm):
    i = pl.program_id(0)
    n = pl.num_programs(0)

    def row_copy(tile, slot, t, kk):
        r = dest[(tile * tm + t) * TOP_K + kk]
        return pltpu.make_async_copy(ys_hbm.at[pl.ds(r, 1)], buf.at[slot, kk, pl.ds(t, 1)], sem.at[slot])

    def issue_tile(tile, slot):
        def issue(t, c):
            for kk in range(TOP_K):
                row_copy(tile, slot, t, kk).start()
            return c

        lax.fori_loop(0, tm, issue, 0, unroll=8)

    slot = i % 2

    @pl.when(i == 0)
    def _():
        issue_tile(0, 0)

    @pl.when(i + 1 < n)
    def _():
        issue_tile(i + 1, 1 - slot)

    def wait(t, c):
        for kk in range(TOP_K):
            row_copy(i, slot, t, kk).wait()
        return c

    lax.fori_loop(0, tm, wait, 0, unroll=8)

    gates = gate_ref[...]
    y = x2_ref[...]
    for kk in range(TOP_K):
        y = y + gates[:, kk:kk + 1] * buf[slot, kk]
    o_ref[...] = (y * lax.rsqrt(jnp.mean(y * y, axis=-1, keepdims=True) + NORM_EPS)) * g_ref[...]


def _combine(dest, ys, x2, gates, g_final, *, tm):
    m, d = x2.shape
    grid_spec = pltpu.PrefetchScalarGridSpec(
        num_scalar_prefetch=1,
        grid=(m // tm,),
        in_specs=[
            pl.BlockSpec(memory_space=pl.ANY),
            pl.BlockSpec((tm, d), lambda i, dest: (i, 0)),
            pl.BlockSpec((tm, LANES), lambda i, dest: (i, 0)),
            pl.BlockSpec((1, d), lambda i, dest: (0, 0)),
        ],
        out_specs=pl.BlockSpec((tm, d), lambda i, dest: (i, 0)),
        scratch_shapes=[pltpu.VMEM((2, TOP_K, tm, d), F32), pltpu.SemaphoreType.DMA((2,))],
    )
    return pl.pallas_call(
        functools.partial(_combine_kernel, tm=tm),
        grid_spec=grid_spec,
        out_shape=jax.ShapeDtypeStruct((m, d), F32),
        compiler_params=_cparams(("arbitrary",)),
        name="combine",
    )(dest, ys, x2, gates, g_final)


def _layer(x, attn_norm, w_in, dil_out_norm, sb_out_norm, w_out, ffn_norm,
           w_router, b_router, w_gate, b_gate, w_up, b_up, w_down, b_down, *, tiles):
    b, seq, d = x.shape
    m = b * seq
    half_width = d // 2
    x2d = x.reshape(m, d)
    cos_t, sin_t = _rope_tables(seq)
    proj = _inproj(x2d, attn_norm.reshape(1, d), w_in, cos_t, sin_t,
                   seq=seq, tm=tiles["in_tm"], tn=tiles["in_tn"])
    proj3 = proj.reshape(b, seq, 3 * d)
    o_d = _dilated_attention(proj3, half_width=half_width).reshape(m, half_width)
    o_s = _stickbreak_attention(proj3, half_width=half_width, qt=tiles["sb_qt"]).reshape(m, half_width)
    x2, h2, idx_l, gate_l = _outproj_router(
        o_d, o_s, x2d, w_out, dil_out_norm.reshape(1, half_width), sb_out_norm.reshape(1, half_width),
        ffn_norm.reshape(1, d), w_router, b_router.reshape(1, N_EXPERTS),
        tm=tiles["out_tm"], tk=tiles["out_tk"])
    meta, dest, n_rows = _routing_metadata(idx_l[:, :TOP_K], m)
    ys = _experts(h2, w_gate, b_gate, w_up, b_up, w_down, b_down, meta, n_rows=n_rows, th=tiles["moe_th"])
    return x2, ys, dest, gate_l


_TILES = dict(in_tm=1024, in_tn=512, sb_qt=512, out_tm=512, out_tk=512, moe_th=512, cmb_tm=128)


def kernel(x, attn_norm, w_in, dil_out_norm, sb_out_norm, w_out, ffn_norm, w_router, b_router,
           w_gate, b_gate, w_up, b_up, w_down, b_down, final_norm):
    depth = attn_norm.shape[0]
    b, seq, d = x.shape
    for l in range(depth):
        x2, ys, dest, gate_l = _layer(
            x, attn_norm[l], w_in[l], dil_out_norm[l], sb_out_norm[l], w_out[l], ffn_norm[l],
            w_router[l], b_router[l], w_gate[l], b_gate[l], w_up[l], b_up[l], w_down[l], b_down[l],
            tiles=_TILES)
        assert depth == 1
        x = _combine(dest, ys, x2, gate_l, final_norm.reshape(1, d), tm=_TILES["cmb_tm"])
    return x.reshape(b, seq, d)
```

```python
import functools
import math

import jax
import jax.numpy as jnp
from jax import lax
from jax.experimental import pallas as pl
from jax.experimental.pallas import tpu as pltpu

F32 = jnp.float32
BF16 = jnp.bfloat16

HEAD_DIM = 64
LANES = 128
BLOCK = 128
DIL_PATTERNS = ((128, 1), (512, 4), (2048, 16))
ROPE_THETA = 10000.0
N_EXPERTS = 32
TOP_K = 4
SWIGLU_LIMIT = 7.0
SWIGLU_ALPHA = 1.702
NORM_EPS = 1e-5
Q_SCALE = HEAD_DIM ** -0.5 * math.log2(math.e)
GATHER_CHUNK = 128
SB_UNROLL = 4
DIL_UNROLL = 8
ROW_ALIGN = 128
ROW_BLOCK = 512
ITEM_ROWS = 1280
VMEM_LIMIT = 56 * 1024 * 1024


def _cparams(sem):
    return pltpu.CompilerParams(dimension_semantics=sem, vmem_limit_bytes=VMEM_LIMIT)


def _inproj_kernel(x_ref, g_ref, w_ref, cos_ref, sin_ref, o_ref, h_scr, *, tn, half_width):
    j = pl.program_id(1)

    @pl.when(j == 0)
    def _():
        x = x_ref[...]
        y = x * lax.rsqrt(jnp.mean(x * x, axis=-1, keepdims=True) + NORM_EPS)
        h_scr[...] = (y * g_ref[...]).astype(BF16)

    acc = jnp.dot(h_scr[...], w_ref[...].astype(BF16), preferred_element_type=F32)
    section = (j * tn) // half_width
    scale = jnp.where((section == 0) | (section == 3), Q_SCALE, 1.0).astype(F32)

    @pl.when(section < 2)
    def _():
        lane = lax.broadcasted_iota(jnp.int32, (acc.shape[0], LANES), 1)
        first_half = (lane % HEAD_DIM) < (HEAD_DIM // 2)
        c = cos_ref[...]
        s = sin_ref[...]
        for g in range(tn // LANES):
            xg = acc[:, g * LANES:(g + 1) * LANES]
            partner = jnp.where(first_half,
                                pltpu.roll(xg, LANES - HEAD_DIM // 2, 1),
                                pltpu.roll(xg, HEAD_DIM // 2, 1))
            o_ref[:, g * LANES:(g + 1) * LANES] = ((xg * c + partner * s) * scale).astype(BF16)

    @pl.when(section >= 2)
    def _():
        o_ref[...] = (acc * scale).astype(BF16)


def _inproj(x2d, g, w_in, cos_t, sin_t, *, seq, tm, tn):
    m, d = x2d.shape
    n = w_in.shape[1]
    half_width = d // 2
    assert half_width % tn == 0 and tn % LANES == 0 and m % tm == 0 and seq % tm == 0
    pos_tiles = seq // tm
    return pl.pallas_call(
        functools.partial(_inproj_kernel, tn=tn, half_width=half_width),
        grid=(m // tm, n // tn),
        in_specs=[
            pl.BlockSpec((tm, d), lambda i, j: (i, 0)),
            pl.BlockSpec((1, d), lambda i, j: (0, 0)),
            pl.BlockSpec((d, tn), lambda i, j: (0, j)),
            pl.BlockSpec((tm, LANES), lambda i, j: (i % pos_tiles, 0)),
            pl.BlockSpec((tm, LANES), lambda i, j: (i % pos_tiles, 0)),
        ],
        out_specs=pl.BlockSpec((tm, tn), lambda i, j: (i, j)),
        out_shape=jax.ShapeDtypeStruct((m, n), BF16),
        scratch_shapes=[pltpu.VMEM((tm, d), BF16)],
        compiler_params=_cparams(("arbitrary", "arbitrary")),
        name="inproj",
    )(x2d, g, w_in, cos_t, sin_t)


def _rope_tables(seq):
    half = HEAD_DIM // 2
    inv_freq = 1.0 / (ROPE_THETA ** (jnp.arange(half, dtype=F32) * 2.0 / HEAD_DIM))
    ang = jnp.arange(seq, dtype=F32)[:, None] * inv_freq[None, :]
    cos = jnp.cos(ang)
    sin = jnp.sin(ang)
    reps = LANES // HEAD_DIM
    cos_t = jnp.tile(jnp.concatenate([cos, cos], axis=1), (1, reps))
    sin_t = jnp.tile(jnp.concatenate([-sin, sin], axis=1), (1, reps))
    return cos_t, sin_t


def _dil_kernel(q_ref, k_ref, v_ref, o_ref, f32_scr, qp, kp, vp, operm, lperm, onat, lnat, *, seq):
    n_blocks = seq // BLOCK
    lane_k = lax.broadcasted_iota(jnp.int32, (2 * BLOCK, LANES), 1)
    head_a_k = lane_k < HEAD_DIM
    lane_q = lax.broadcasted_iota(jnp.int32, (BLOCK, LANES), 1)
    head_a_q = lane_q < HEAD_DIM
    qi = lax.broadcasted_iota(jnp.int32, (BLOCK, 2 * BLOCK), 0)
    ki = lax.broadcasted_iota(jnp.int32, (BLOCK, 2 * BLOCK), 1)
    dist = qi + BLOCK - ki
    band = (dist >= 0) & (dist <= BLOCK)
    cur_only = ki >= BLOCK
    zero_b = jnp.zeros((), BF16)
    head_ones = jnp.concatenate([jnp.where(head_a_k, 1.0, 0.0), jnp.where(head_a_k, 0.0, 1.0)], axis=0).astype(BF16)

    f32_scr[0] = q_ref[0].astype(F32)
    f32_scr[1] = k_ref[0].astype(F32)
    f32_scr[2] = v_ref[0].astype(F32)
    kp[0:BLOCK, :] = jnp.zeros((BLOCK, LANES), BF16)
    vp[0:BLOCK, :] = jnp.zeros((BLOCK, LANES), BF16)

    for p, (window, dil) in enumerate(DIL_PATTERNS):
        assert window // dil == BLOCK
        length = seq // dil
        per_stream = length // BLOCK
        assert length % BLOCK == 0
        if dil == 1:
            qp[...] = q_ref[0]
            kp[BLOCK:, :] = k_ref[0]
            vp[BLOCK:, :] = v_ref[0]
        else:
            for r in range(dil):
                rows = pl.ds(r, length, stride=dil)
                qp[r * length:(r + 1) * length, :] = f32_scr[0, rows, :].astype(BF16)
                kp[BLOCK + r * length:BLOCK + (r + 1) * length, :] = f32_scr[1, rows, :].astype(BF16)
                vp[BLOCK + r * length:BLOCK + (r + 1) * length, :] = f32_scr[2, rows, :].astype(BF16)
        o_dst = onat.at[p] if dil == 1 else operm
        l_dst = lnat.at[p] if dil == 1 else lperm

        def block_body(bi, carry, per_stream=per_stream, o_dst=o_dst, l_dst=l_dst):
            row0 = pl.multiple_of(bi * BLOCK, BLOCK)
            qb = qp[pl.ds(row0, BLOCK), :]
            kw = kp[pl.ds(row0, 2 * BLOCK), :]
            vw = vp[pl.ds(row0, 2 * BLOCK), :]
            has_prev = (bi % per_stream) != 0
            valid = band & (cur_only | has_prev)
            kk = jnp.concatenate([jnp.where(head_a_k, kw, zero_b), jnp.where(head_a_k, zero_b, kw)], axis=0)
            s2 = lax.dot_general(qb, kk, (((1,), (1,)), ((), ())), preferred_element_type=F32)
            ps, ms = [], []
            for h in range(2):
                s = jnp.where(valid, s2[:, h * 2 * BLOCK:(h + 1) * 2 * BLOCK], -jnp.inf)
                m = jnp.max(s, axis=-1, keepdims=True)
                ps.append(jnp.exp2(s - m).astype(BF16))
                ms.append(m)
            vv = jnp.concatenate([jnp.where(head_a_k, vw, zero_b), jnp.where(head_a_k, zero_b, vw)], axis=0)
            o_den = jnp.dot(jnp.concatenate(ps, axis=1), jnp.concatenate([vv, head_ones], axis=1),
                            preferred_element_type=F32)
            o = o_den[:, :LANES]
            den = o_den[:, LANES:]
            o_dst[pl.ds(row0, BLOCK), :] = o / den
            l_dst[pl.ds(row0, BLOCK), :] = jnp.where(head_a_q, ms[0], ms[1]) + jnp.log2(den)
            return carry

        lax.fori_loop(0, n_blocks, block_body, 0, unroll=DIL_UNROLL)
        if dil != 1:
            for r in range(dil):
                rows = pl.ds(r, length, stride=dil)
                onat[p, rows, :] = operm[r * length:(r + 1) * length, :]
                lnat[p, rows, :] = lperm[r * length:(r + 1) * length, :]

    def mix_body(c, carry):
        rows = pl.ds(pl.multiple_of(c * BLOCK, BLOCK), BLOCK)
        ls = [lnat[p, rows, :] for p in range(len(DIL_PATTERNS))]
        m = functools.reduce(jnp.maximum, ls)
        es = [jnp.exp2(l - m) for l in ls]
        num = functools.reduce(lambda a, b: a + b, [e * onat[p, rows, :] for p, e in enumerate(es)])
        o_ref[0, rows, :] = num / functools.reduce(lambda a, b: a + b, es)
        return carry

    lax.fori_loop(0, n_blocks, mix_body, 0)


def _dilated_attention(proj3, *, half_width):
    b, seq, _ = proj3.shape
    pairs = half_width // LANES
    n_pat = len(DIL_PATTERNS)
    spec = lambda off: pl.BlockSpec((1, seq, LANES), lambda bi, c, off=off: (bi, 0, off + c))
    return pl.pallas_call(
        functools.partial(_dil_kernel, seq=seq),
        grid=(b, pairs),
        in_specs=[spec(0), spec(pairs), spec(2 * pairs)],
        out_specs=pl.BlockSpec((1, seq, LANES), lambda bi, c: (bi, 0, c)),
        out_shape=jax.ShapeDtypeStruct((b, seq, half_width), F32),
        scratch_shapes=[
            pltpu.VMEM((3, seq, LANES), F32),
            pltpu.VMEM((seq, LANES), BF16),
            pltpu.VMEM((seq + BLOCK, LANES), BF16),
            pltpu.VMEM((seq + BLOCK, LANES), BF16),
            pltpu.VMEM((seq, LANES), F32),
            pltpu.VMEM((seq, LANES), F32),
            pltpu.VMEM((n_pat, seq, LANES), F32),
            pltpu.VMEM((n_pat, seq, LANES), F32),
        ],
        compiler_params=_cparams(("arbitrary", "arbitrary")),
        name="dilated_attention",
    )(proj3, proj3, proj3)


def _sb_kernel(q_ref, k_ref, v_ref, o_ref, kk_scr, vv_scr, tri_scr, acc_scr, carry_scr, *, seq, qt):
    n_blocks = seq // BLOCK
    per_tile = qt // BLOCK
    lane_k = lax.broadcasted_iota(jnp.int32, (BLOCK, LANES), 1)
    head_a = lane_k < HEAD_DIM
    zero_b = jnp.zeros((), BF16)

    def prep(jb, carry):
        rows = pl.ds(pl.multiple_of(jb * BLOCK, BLOCK), BLOCK)
        kb = k_ref[0, rows, :]
        vb = v_ref[0, rows, :]
        kk_scr[jb, 0:BLOCK, :] = jnp.where(head_a, kb, zero_b)
        kk_scr[jb, BLOCK:, :] = jnp.where(head_a, zero_b, kb)
        vv_scr[jb, 0:BLOCK, :] = jnp.where(head_a, vb, zero_b)
        vv_scr[jb, BLOCK:, :] = jnp.where(head_a, zero_b, vb)
        return carry

    lax.fori_loop(0, n_blocks, prep, 0)

    jj = lax.broadcasted_iota(jnp.int32, (2 * BLOCK, 2 * BLOCK), 0) % BLOCK
    ss = lax.broadcasted_iota(jnp.int32, (2 * BLOCK, 2 * BLOCK), 1)
    tri_scr[...] = jnp.where((ss >= BLOCK) | (jj > ss), 1.0, 0.0).astype(BF16)

    row = lax.broadcasted_iota(jnp.int32, (qt, 2 * BLOCK), 0)
    col = lax.broadcasted_iota(jnp.int32, (qt, 2 * BLOCK), 1) % BLOCK

    def key_block(qb, jb, key_off):
        z = lax.dot_general(qb, kk_scr[jb], (((1,), (1,)), ((), ())), preferred_element_type=F32)
        neg_abs = lax.bitcast_convert_type(lax.bitcast_convert_type(z, jnp.uint32) | jnp.uint32(0x80000000), F32)
        ls = jnp.minimum(z, 0.0) - jnp.log2(1.0 + jnp.exp2(neg_abs))
        lr = ls - z
        if key_off is not None:
            causal = (col + key_off) < row
            lr = jnp.where(causal, lr, 0.0)
        hi = lr.astype(BF16)
        lo = (lr - hi.astype(F32)).astype(BF16)
        tri = tri_scr[...]
        a_heads = []
        for h in range(2):
            sl = slice(h * BLOCK, (h + 1) * BLOCK)
            lhs = jnp.concatenate([hi[:, sl], lo[:, sl]], axis=1)
            cs_tot = jnp.dot(lhs, tri, preferred_element_type=F32)
            carry = carry_scr[:, sl]
            a = jnp.exp2(ls[:, sl] + cs_tot[:, :BLOCK] + carry)
            if key_off is not None:
                a = jnp.where(causal[:, :BLOCK], a, 0.0)
            carry_scr[:, sl] = carry + cs_tot[:, BLOCK:]
            a_heads.append(a.astype(BF16))
        acc_scr[...] += jnp.dot(jnp.concatenate(a_heads, axis=1), vv_scr[jb], preferred_element_type=F32)

    def q_tile(i, carry):
        rows = pl.ds(pl.multiple_of(i * qt, qt), qt)
        qb = q_ref[0, rows, :]
        acc_scr[...] = jnp.zeros_like(acc_scr)
        carry_scr[...] = jnp.zeros_like(carry_scr)
        for t in reversed(range(per_tile)):
            key_block(qb, i * per_tile + t, t * BLOCK)

        def inner(t, c):
            for s in range(SB_UNROLL):
                key_block(qb, i * per_tile - 1 - (t * SB_UNROLL + s), None)
            return c

        lax.fori_loop(0, i * (per_tile // SB_UNROLL), inner, 0)
        o_ref[0, rows, :] = acc_scr[...]
        return carry

    lax.fori_loop(0, seq // qt, q_tile, 0)


def _stickbreak_attention(proj3, *, half_width, qt):
    b, seq, _ = proj3.shape
    pairs = half_width // LANES
    n_blocks = seq // BLOCK
    assert seq % qt == 0 and qt % BLOCK == 0
    spec = lambda off: pl.BlockSpec((1, seq, LANES), lambda bi, c, off=off: (bi, 0, off + c))
    return pl.pallas_call(
        functools.partial(_sb_kernel, seq=seq, qt=qt),
        grid=(b, pairs),
        in_specs=[spec(3 * pairs), spec(4 * pairs), spec(5 * pairs)],
        out_specs=pl.BlockSpec((1, seq, LANES), lambda bi, c: (bi, 0, c)),
        out_shape=jax.ShapeDtypeStruct((b, seq, half_width), F32),
        scratch_shapes=[
            pltpu.VMEM((n_blocks, 2 * BLOCK, LANES), BF16),
            pltpu.VMEM((n_blocks, 2 * BLOCK, LANES), BF16),
            pltpu.VMEM((2 * BLOCK, 2 * BLOCK), BF16),
            pltpu.VMEM((qt, LANES), F32),
            pltpu.VMEM((qt, 2 * BLOCK), F32),
        ],
        compiler_params=_cparams(("arbitrary", "arbitrary")),
        name="stickbreak_attention",
    )(proj3, proj3, proj3)


def _split_bf16(x):
    hi = x.astype(BF16)
    return hi, (x - hi.astype(F32)).astype(BF16)


def _outproj_kernel(od_ref, os_ref, x_ref, w_ref, gd_ref, gs_ref, gf_ref, wr_ref, br_ref,
                    x2_ref, h2_ref, idx_ref, gate_ref, acc_d, acc_s, ssq_d, ssq_s, *, nk, half_width):
    k = pl.program_id(1)

    @pl.when(k == 0)
    def _():
        acc_d[...] = jnp.zeros_like(acc_d)
        acc_s[...] = jnp.zeros_like(acc_s)
        ssq_d[...] = jnp.zeros_like(ssq_d)
        ssq_s[...] = jnp.zeros_like(ssq_s)

    w = w_ref[...].astype(BF16)

    def accumulate(a_ref, g_ref, acc, ssq):
        a = a_ref[...]
        ssq[...] += jnp.sum(a * a, axis=-1, keepdims=True)
        acc[...] += jnp.dot((a * g_ref[...]).astype(BF16), w, preferred_element_type=F32)

    @pl.when(k < nk // 2)
    def _():
        accumulate(od_ref, gd_ref, acc_d, ssq_d)

    @pl.when(k >= nk // 2)
    def _():
        accumulate(os_ref, gs_ref, acc_s, ssq_s)

    @pl.when(k == nk - 1)
    def _():
        r_d = lax.rsqrt(ssq_d[...] / half_width + NORM_EPS)
        r_s = lax.rsqrt(ssq_s[...] / half_width + NORM_EPS)
        x2 = x_ref[...] + (acc_d[...] * r_d + acc_s[...] * r_s)
        x2_ref[...] = x2
        h2 = (x2 * lax.rsqrt(jnp.mean(x2 * x2, axis=-1, keepdims=True) + NORM_EPS)) * gf_ref[...]
        packed = _pack_pairs(h2)
        n_chunk = packed.shape[1] // LANES
        for ch in range(n_chunk):
            h2_ref[pl.ds(ch, packed.shape[0], stride=n_chunk), :] = packed[:, ch * LANES:(ch + 1) * LANES]
        h_hi, h_lo = _split_bf16(h2)
        w_hi, w_lo = _split_bf16(wr_ref[...])
        logits = (jnp.dot(h_hi, w_hi, preferred_element_type=F32)
                  + jnp.dot(h_hi, w_lo, preferred_element_type=F32)
                  + jnp.dot(h_lo, w_hi, preferred_element_type=F32)) + br_ref[...]
        tm = logits.shape[0]
        e_iota = lax.broadcasted_iota(jnp.int32, logits.shape, 1)
        out_lane = lax.broadcasted_iota(jnp.int32, (tm, LANES), 1)
        idx_out = jnp.zeros((tm, LANES), jnp.int32)
        val_out = jnp.full((tm, LANES), -jnp.inf, F32)
        top0 = None
        for kk in range(TOP_K):
            m = jnp.max(logits, axis=-1, keepdims=True)
            sel = jnp.min(jnp.where(logits == m, e_iota, N_EXPERTS), axis=-1, keepdims=True)
            idx_out = jnp.where(out_lane == kk, sel, idx_out)
            val_out = jnp.where(out_lane == kk, m, val_out)
            logits = jnp.where(e_iota == sel, -jnp.inf, logits)
            if kk == 0:
                top0 = m
        ex = jnp.exp(val_out - top0)
        gate_ref[...] = ex / jnp.sum(ex, axis=-1, keepdims=True)
        idx_ref[...] = idx_out


def _outproj_router(o_d, o_s, x2d, w_out, g_d, g_s, g_f, w_router, b_router, *, tm, tk):
    m, d = x2d.shape
    half_width = d // 2
    nk = d // tk
    assert half_width % tk == 0 and m % tm == 0
    kh = nk // 2
    return pl.pallas_call(
        functools.partial(_outproj_kernel, nk=nk, half_width=half_width),
        grid=(m // tm, nk),
        in_specs=[
            pl.BlockSpec((tm, tk), lambda i, k: (i, jnp.minimum(k, kh - 1))),
            pl.BlockSpec((tm, tk), lambda i, k: (i, jnp.maximum(k - kh, 0))),
            pl.BlockSpec((tm, d), lambda i, k: (i, 0)),
            pl.BlockSpec((tk, d), lambda i, k: (k, 0)),
            pl.BlockSpec((1, tk), lambda i, k: (0, jnp.minimum(k, kh - 1))),
            pl.BlockSpec((1, tk), lambda i, k: (0, jnp.maximum(k - kh, 0))),
            pl.BlockSpec((1, d), lambda i, k: (0, 0)),
            pl.BlockSpec((d, N_EXPERTS), lambda i, k: (0, 0)),
            pl.BlockSpec((1, N_EXPERTS), lambda i, k: (0, 0)),
        ],
        out_specs=[
            pl.BlockSpec((tm, d), lambda i, k: (i, 0)),
            pl.BlockSpec((tm * (d // 2 // LANES), LANES), lambda i, k: (i, 0)),
            pl.BlockSpec((tm, LANES), lambda i, k: (i, 0)),
            pl.BlockSpec((tm, LANES), lambda i, k: (i, 0)),
        ],
        out_shape=[
            jax.ShapeDtypeStruct((m, d), F32),
            jax.ShapeDtypeStruct((m * (d // 2 // LANES), LANES), jnp.uint32),
            jax.ShapeDtypeStruct((m, LANES), jnp.int32),
            jax.ShapeDtypeStruct((m, LANES), F32),
        ],
        scratch_shapes=[
            pltpu.VMEM((tm, d), F32),
            pltpu.VMEM((tm, d), F32),
            pltpu.VMEM((tm, 1), F32),
            pltpu.VMEM((tm, 1), F32),
        ],
        compiler_params=_cparams(("arbitrary", "arbitrary")),
        name="outproj_router",
    )(o_d, o_s, x2d, w_out, g_d, g_s, g_f, w_router, b_router)


def _unpack_pairs(words):
    hi = lax.bitcast_convert_type(words & jnp.uint32(0xFFFF0000), F32)
    lo = lax.bitcast_convert_type(words << 16, F32)
    return hi.astype(BF16), lo.astype(BF16)


def _pack_pairs(x):
    c = x.shape[1] // 2
    as_bits = lambda v: lax.bitcast_convert_type(v.astype(BF16).astype(F32), jnp.uint32)
    return as_bits(x[:, :c]) | (as_bits(x[:, c:]) >> 16)


def _expert_kernel(item_e, item_r0, item_nb, n_items, row_tok,
                   h_hbm, wg_ref, bg_ref, wu_ref, bu_ref, wd_ref, bd_ref,
                   ys_hbm, stage, xb16, acc, issued, pending, sem_in, sem_out, *, nj):
    i = pl.program_id(0)
    j = pl.program_id(1)
    n_live = n_items[0]
    active = i < n_live
    r0 = item_r0[i]
    n_align = item_nb[i]
    n_rows = n_align * ROW_ALIGN
    per_trip = ROW_BLOCK // ROW_ALIGN
    n_main = n_align // per_trip
    nxt = jnp.minimum(i + 1, pl.num_programs(0) - 1)
    r0_next = item_r0[nxt]
    rows_next = jnp.where(i + 1 < n_live, item_nb[nxt] * ROW_ALIGN, 0)
    d = acc.shape[1]
    n_chunk = d // 2 // LANES

    def row_copy(base, r):
        tok = row_tok[base + r]
        src = h_hbm.at[pl.ds(pl.multiple_of(tok * n_chunk, n_chunk), n_chunk)]
        return pltpu.make_async_copy(src, stage.at[pl.ds(pl.multiple_of(r * n_chunk, n_chunk), n_chunk)], sem_in)

    def out_copy(blk):
        src = acc.at[pl.ds(pl.multiple_of(blk * ROW_ALIGN, ROW_ALIGN), ROW_ALIGN)]
        dst = ys_hbm.at[pl.ds(pl.multiple_of(r0 + blk * ROW_ALIGN, ROW_ALIGN), ROW_ALIGN)]
        return pltpu.make_async_copy(src, dst, sem_out)

    def drain():
        def wait(t, c):
            out_copy(0).wait()
            return c

        lax.fori_loop(0, pending[0], wait, 0)
        pending[0] = 0

    @pl.when((i == 0) & (j == 0))
    def _():
        pending[0] = 0

        def issue(r, c):
            row_copy(r0, r).start()
            return c

        lax.fori_loop(0, n_rows, issue, 0)

    @pl.when(active & (j == 0))
    def _():
        def wait(r, c):
            row_copy(r0, r).wait()
            return c

        lax.fori_loop(0, n_rows, wait, 0)
        drain()

        def unpack(rb, c):
            rows = pl.ds(pl.multiple_of(rb * ROW_ALIGN, ROW_ALIGN), ROW_ALIGN)
            for ch in range(n_chunk):
                words = stage[pl.ds(rb * (ROW_ALIGN * n_chunk) + ch, ROW_ALIGN, stride=n_chunk), :]
                hi, lo = _unpack_pairs(words)
                xb16[rows, ch * LANES:(ch + 1) * LANES] = hi
                xb16[rows, d // 2 + ch * LANES:d // 2 + (ch + 1) * LANES] = lo
            acc[rows, :] = jnp.broadcast_to(bd_ref[0], (ROW_ALIGN, d))
            return c

        lax.fori_loop(0, n_align, unpack, 0)
        issued[0] = 0

    def trip(row0, n_blk):
        rows = pl.ds(row0, n_blk * ROW_ALIGN)
        xb = xb16[rows, :]
        g = jnp.dot(xb, wg_ref[0].astype(BF16), preferred_element_type=F32) + bg_ref[0]
        u = jnp.dot(xb, wu_ref[0].astype(BF16), preferred_element_type=F32) + bu_ref[0]
        g = jnp.minimum(g, SWIGLU_LIMIT)
        u = jnp.clip(u, -SWIGLU_LIMIT, SWIGLU_LIMIT)
        a = (u + 1.0) * (g * jax.nn.sigmoid(SWIGLU_ALPHA * g))
        acc[rows, :] += jnp.dot(a.astype(BF16), wd_ref[0].astype(BF16), preferred_element_type=F32)

        base = issued[0]

        @pl.when(base < rows_next)
        def _():
            for t in range(GATHER_CHUNK):
                row_copy(r0_next, base + t).start()

        issued[0] = base + GATHER_CHUNK

        @pl.when(j == nj - 1)
        def _():
            for s in range(n_blk):
                out_copy(row0 // ROW_ALIGN + s).start()
            pending[0] = pending[0] + n_blk

    @pl.when(active)
    def _():
        def main_trip(rb, c):
            trip(pl.multiple_of(rb * ROW_BLOCK, ROW_BLOCK), per_trip)
            return c

        lax.fori_loop(0, n_main, main_trip, 0)

        def tail_trip(t, c):
            trip(pl.multiple_of((n_main * per_trip + t) * ROW_ALIGN, ROW_ALIGN), 1)
            return c

        lax.fori_loop(0, n_align - n_main * per_trip, tail_trip, 0)

    @pl.when(active & (j == nj - 1))
    def _():
        def issue_rest(r, c):
            row_copy(r0_next, r).start()
            return c

        lax.fori_loop(issued[0], rows_next, issue_rest, 0)

    @pl.when((i == pl.num_programs(0) - 1) & (j == nj - 1))
    def _():
        drain()
        acc[0:ROW_ALIGN, :] = jnp.zeros((ROW_ALIGN, d), F32)
        used = n_items[1]

        def tail_copy(t):
            dst = ys_hbm.at[pl.ds(pl.multiple_of((used + t) * ROW_ALIGN, ROW_ALIGN), ROW_ALIGN)]
            return pltpu.make_async_copy(acc.at[pl.ds(0, ROW_ALIGN)], dst, sem_out)

        def issue(t, c):
            tail_copy(t).start()
            return c

        n_tail = ys_hbm.shape[0] // ROW_ALIGN - used
        lax.fori_loop(0, n_tail, issue, 0)

        def wait(t, c):
            tail_copy(t).wait()
            return c

        lax.fori_loop(0, n_tail, wait, 0)


def _experts(h2, w_gate, b_gate, w_up, b_up, w_down, b_down, meta, *, n_rows, th):
    item_e, item_r0, item_nb, n_items, row_tok = meta
    n_exp, d, dh = w_gate.shape
    nj = dh // th
    n_item_slots = item_e.shape[0]

    def w_col(i, j, item_e, item_r0, item_nb, n_items, row_tok):
        return (item_e[i], 0, jnp.where(i < n_items[0], j, nj - 1))

    def w_row(i, j, item_e, item_r0, item_nb, n_items, row_tok):
        return (item_e[i], jnp.where(i < n_items[0], j, nj - 1), 0)

    def b_all(i, j, item_e, item_r0, item_nb, n_items, row_tok):
        return (item_e[i], 0, 0)

    grid_spec = pltpu.PrefetchScalarGridSpec(
        num_scalar_prefetch=5,
        grid=(n_item_slots, nj),
        in_specs=[
            pl.BlockSpec(memory_space=pl.ANY),
            pl.BlockSpec((1, d, th), w_col),
            pl.BlockSpec((1, 1, th), w_col),
            pl.BlockSpec((1, d, th), w_col),
            pl.BlockSpec((1, 1, th), w_col),
            pl.BlockSpec((1, th, d), w_row),
            pl.BlockSpec((1, 1, d), b_all),
        ],
        out_specs=pl.BlockSpec(memory_space=pl.ANY),
        scratch_shapes=[
            pltpu.VMEM((ITEM_ROWS * (d // 2 // LANES), LANES), jnp.uint32),
            pltpu.VMEM((ITEM_ROWS, d), BF16),
            pltpu.VMEM((ITEM_ROWS, d), F32),
            pltpu.SMEM((1,), jnp.int32),
            pltpu.SMEM((1,), jnp.int32),
            pltpu.SemaphoreType.DMA,
            pltpu.SemaphoreType.DMA,
        ],
    )
    return pl.pallas_call(
        functools.partial(_expert_kernel, nj=nj),
        grid_spec=grid_spec,
        out_shape=jax.ShapeDtypeStruct((n_rows, d), F32),
        compiler_params=_cparams(("arbitrary", "arbitrary")),
        name="experts",
    )(item_e, item_r0, item_nb, n_items, row_tok,
      h2, w_gate, b_gate.reshape(n_exp, 1, dh), w_up, b_up.reshape(n_exp, 1, dh), w_down,
      b_down.reshape(n_exp, 1, d))


def _routing_metadata(top_idx, n_tok):
    n_assign = n_tok * TOP_K
    flat_e = top_idx.reshape(n_assign)
    onehot = (flat_e[:, None] == jnp.arange(N_EXPERTS, dtype=jnp.int32)[None, :]).astype(jnp.int32)
    rank = jnp.take_along_axis(jnp.cumsum(onehot, axis=0), flat_e[:, None], axis=1)[:, 0] - 1
    counts = jnp.sum(onehot, axis=0)
    padded = (counts + ROW_ALIGN - 1) // ROW_ALIGN * ROW_ALIGN
    pend = jnp.cumsum(padded)
    pstart = pend - padded
    dest = (pstart[flat_e] + rank).astype(jnp.int32)
    n_rows = (n_assign // ROW_ALIGN + N_EXPERTS) * ROW_ALIGN
    flat_tok = jnp.arange(n_assign, dtype=jnp.int32) // TOP_K
    row_tok = jnp.zeros((n_rows,), jnp.int32).at[dest].set(flat_tok)
    n_chunks = (padded + ITEM_ROWS - 1) // ITEM_ROWS
    cend = jnp.cumsum(n_chunks)
    cstart = cend - n_chunks
    n_slots = N_EXPERTS + n_assign // ITEM_ROWS + 1
    slot = jnp.arange(n_slots, dtype=jnp.int32)
    n_items = cend[-1]
    last_e = jnp.max(jnp.where(n_chunks > 0, jnp.arange(N_EXPERTS), 0))
    e_of = jnp.minimum(jnp.searchsorted(cend, slot, side='right'), N_EXPERTS - 1)
    live = slot < n_items
    item_e = jnp.where(live, e_of, last_e).astype(jnp.int32)
    chunk = slot - cstart[e_of]
    item_r0 = jnp.where(live, pstart[e_of] + chunk * ITEM_ROWS, 0).astype(jnp.int32)
    item_rows = jnp.clip(padded[e_of] - chunk * ITEM_ROWS, 0, ITEM_ROWS)
    item_nb = jnp.where(live, item_rows // ROW_ALIGN, 0).astype(jnp.int32)
    counts_blk = jnp.stack([n_items, pend[-1] // ROW_ALIGN]).astype(jnp.int32)
    meta = (item_e, item_r0, item_nb, counts_blk, row_tok)
    return meta, dest, n_rows


def _combine_kernel(dest, ys_hbm, x2_ref, gate_ref, g_ref, o_ref, buf, sem, *, tm):
    i = pl.program_id(0)
    n = pl.num_programs(0)

    def row_copy(tile, slot, t, kk):
        r = dest[(tile * tm + t) * TOP_K + kk]
        return pltpu.make_async_copy(ys_hbm.at[pl.ds(r, 1)], buf.at[slot, kk, pl.ds(t, 1)], sem.at[slot])

    def issue_tile(tile, slot):
        def issue(t, c):
            for kk in range(TOP_K):
                row_copy(tile, slot, t, kk).start()
            return c

        lax.fori_loop(0, tm, issue, 0, unroll=8)

    slot = i % 2

    @pl.when(i == 0)
    def _():
        issue_tile(0, 0)

    @pl.when(i + 1 < n)
    def _():
        issue_tile(i + 1, 1 - slot)

    def wait(t, c):
        for kk in range(TOP_K):
            row_copy(i, slot, t, kk).wait()
        return c

    lax.fori_loop(0, tm, wait, 0, unroll=8)

    gates = gate_ref[...]
    y = x2_ref[...]
    for kk in range(TOP_K):
        y = y + gates[:, kk:kk + 1] * buf[slot, kk]
    o_ref[...] = (y * lax.rsqrt(jnp.mean(y * y, axis=-1, keepdims=True) + NORM_EPS)) * g_ref[...]


def _combine(dest, ys, x2, gates, g_final, *, tm):
    m, d = x2.shape
    grid_spec = pltpu.PrefetchScalarGridSpec(
        num_scalar_prefetch=1,
        grid=(m // tm,),
        in_specs=[
            pl.BlockSpec(memory_space=pl.ANY),
            pl.BlockSpec((tm, d), lambda i, dest: (i, 0)),
            pl.BlockSpec((tm, LANES), lambda i, dest: (i, 0)),
            pl.BlockSpec((1, d), lambda i, dest: (0, 0)),
        ],
        out_specs=pl.BlockSpec((tm, d), lambda i, dest: (i, 0)),
        scratch_shapes=[pltpu.VMEM((2, TOP_K, tm, d), F32), pltpu.SemaphoreType.DMA((2,))],
    )
    return pl.pallas_call(
        functools.partial(_combine_kernel, tm=tm),
        grid_spec=grid_spec,
        out_shape=jax.ShapeDtypeStruct((m, d), F32),
        compiler_params=_cparams(("arbitrary",)),
        name="combine",
    )(dest, ys, x2, gates, g_final)


def _layer(x, attn_norm, w_in, dil_out_norm, sb_out_norm, w_out, ffn_norm,
           w_router, b_router, w_gate, b_gate, w_up, b_up, w_down, b_down, *, tiles):
    b, seq, d = x.shape
    m = b * seq
    half_width = d // 2
    x2d = x.reshape(m, d)
    cos_t, sin_t = _rope_tables(seq)
    proj = _inproj(x2d, attn_norm.reshape(1, d), w_in, cos_t, sin_t,
                   seq=seq, tm=tiles["in_tm"], tn=tiles["in_tn"])
    proj3 = proj.reshape(b, seq, 3 * d)
    o_d = _dilated_attention(proj3, half_width=half_width).reshape(m, half_width)
    o_s = _stickbreak_attention(proj3, half_width=half_width, qt=tiles["sb_qt"]).reshape(m, half_width)
    x2, h2, idx_l, gate_l = _outproj_router(
        o_d, o_s, x2d, w_out, dil_out_norm.reshape(1, half_width), sb_out_norm.reshape(1, half_width),
        ffn_norm.reshape(1, d), w_router, b_router.reshape(1, N_EXPERTS),
        tm=tiles["out_tm"], tk=tiles["out_tk"])
    meta, dest, n_rows = _routing_metadata(idx_l[:, :TOP_K], m)
    ys = _experts(h2, w_gate, b_gate, w_up, b_up, w_down, b_down, meta, n_rows=n_rows, th=tiles["moe_th"])
    return x2, ys, dest, gate_l


_TILES = dict(in_tm=1024, in_tn=512, sb_qt=512, out_tm=512, out_tk=512, moe_th=512, cmb_tm=128)


def kernel(x, attn_norm, w_in, dil_out_norm, sb_out_norm, w_out, ffn_norm, w_router, b_router,
           w_gate, b_gate, w_up, b_up, w_down, b_down, final_norm):
    depth = attn_norm.shape[0]
    b, seq, d = x.shape
    for l in range(depth):
        x2, ys, dest, gate_l = _layer(
            x, attn_norm[l], w_in[l], dil_out_norm[l], sb_out_norm[l], w_out[l], ffn_norm[l],
            w_router[l], b_router[l], w_gate[l], b_gate[l], w_up[l], b_up[l], w_down[l], b_down[l],
            tiles=_TILES)
        assert depth == 1
        x = _combine(dest, ys, x2, gate_l, final_norm.reshape(1, d), tm=_TILES["cmb_tm"])
    return x.reshape(b, seq, d)
```

```python
import functools
import math

import jax
import jax.numpy as jnp
from jax import lax
from jax.experimental import pallas as pl
from jax.experimental.pallas import tpu as pltpu

F32 = jnp.float32
BF16 = jnp.bfloat16

HEAD_DIM = 64
LANES = 128
BLOCK = 128
DIL_PATTERNS = ((128, 1), (512, 4), (2048, 16))
ROPE_THETA = 10000.0
N_EXPERTS = 32
TOP_K = 4
SWIGLU_LIMIT = 7.0
SWIGLU_ALPHA = 1.702
NORM_EPS = 1e-5
Q_SCALE = HEAD_DIM ** -0.5 * math.log2(math.e)
SB_UNROLL = 4
DIL_UNROLL = 8
ROW_ALIGN = 128
ROW_BLOCK = 512
ITEM_ROWS = 1280
VMEM_LIMIT = 56 * 1024 * 1024


def _cparams(sem):
    return pltpu.CompilerParams(dimension_semantics=sem, vmem_limit_bytes=VMEM_LIMIT)


def _inproj_kernel(x_ref, g_ref, w_ref, cos_ref, sin_ref, o_ref, h_scr, *, tn, half_width):
    j = pl.program_id(1)

    @pl.when(j == 0)
    def _():
        x = x_ref[...]
        y = x * lax.rsqrt(jnp.mean(x * x, axis=-1, keepdims=True) + NORM_EPS)
        h_scr[...] = (y * g_ref[...]).astype(BF16)

    acc = jnp.dot(h_scr[...], w_ref[...].astype(BF16), preferred_element_type=F32)
    section = (j * tn) // half_width
    scale = jnp.where((section == 0) | (section == 3), Q_SCALE, 1.0).astype(F32)

    @pl.when(section < 2)
    def _():
        lane = lax.broadcasted_iota(jnp.int32, (acc.shape[0], LANES), 1)
        first_half = (lane % HEAD_DIM) < (HEAD_DIM // 2)
        c = cos_ref[...]
        s = sin_ref[...]
        for g in range(tn // LANES):
            xg = acc[:, g * LANES:(g + 1) * LANES]
            partner = jnp.where(first_half,
                                pltpu.roll(xg, LANES - HEAD_DIM // 2, 1),
                                pltpu.roll(xg, HEAD_DIM // 2, 1))
            o_ref[:, g * LANES:(g + 1) * LANES] = ((xg * c + partner * s) * scale).astype(BF16)

    @pl.when(section >= 2)
    def _():
        o_ref[...] = (acc * scale).astype(BF16)


def _inproj(x2d, g, w_in, cos_t, sin_t, *, seq, tm, tn):
    m, d = x2d.shape
    n = w_in.shape[1]
    half_width = d // 2
    assert half_width % tn == 0 and tn % LANES == 0 and m % tm == 0 and seq % tm == 0
    pos_tiles = seq // tm
    return pl.pallas_call(
        functools.partial(_inproj_kernel, tn=tn, half_width=half_width),
        grid=(m // tm, n // tn),
        in_specs=[
            pl.BlockSpec((tm, d), lambda i, j: (i, 0)),
            pl.BlockSpec((1, d), lambda i, j: (0, 0)),
            pl.BlockSpec((d, tn), lambda i, j: (0, j)),
            pl.BlockSpec((tm, LANES), lambda i, j: (i % pos_tiles, 0)),
            pl.BlockSpec((tm, LANES), lambda i, j: (i % pos_tiles, 0)),
        ],
        out_specs=pl.BlockSpec((tm, tn), lambda i, j: (i, j)),
        out_shape=jax.ShapeDtypeStruct((m, n), BF16),
        scratch_shapes=[pltpu.VMEM((tm, d), BF16)],
        compiler_params=_cparams(("arbitrary", "arbitrary")),
        name="inproj",
    )(x2d, g, w_in, cos_t, sin_t)


def _rope_tables(seq):
    half = HEAD_DIM // 2
    inv_freq = 1.0 / (ROPE_THETA ** (jnp.arange(half, dtype=F32) * 2.0 / HEAD_DIM))
    ang = jnp.arange(seq, dtype=F32)[:, None] * inv_freq[None, :]
    cos = jnp.cos(ang)
    sin = jnp.sin(ang)
    reps = LANES // HEAD_DIM
    cos_t = jnp.tile(jnp.concatenate([cos, cos], axis=1), (1, reps))
    sin_t = jnp.tile(jnp.concatenate([-sin, sin], axis=1), (1, reps))
    return cos_t, sin_t


def _dil_kernel(q_ref, k_ref, v_ref, o_ref, f32_scr, qp, kp, vp, operm, lperm, onat, lnat, *, seq):
    n_blocks = seq // BLOCK
    lane_k = lax.broadcasted_iota(jnp.int32, (2 * BLOCK, LANES), 1)
    head_a_k = lane_k < HEAD_DIM
    lane_q = lax.broadcasted_iota(jnp.int32, (BLOCK, LANES), 1)
    head_a_q = lane_q < HEAD_DIM
    qi = lax.broadcasted_iota(jnp.int32, (BLOCK, 2 * BLOCK), 0)
    ki = lax.broadcasted_iota(jnp.int32, (BLOCK, 2 * BLOCK), 1)
    dist = qi + BLOCK - ki
    band = (dist >= 0) & (dist <= BLOCK)
    cur_only = ki >= BLOCK
    zero_b = jnp.zeros((), BF16)
    head_ones = jnp.concatenate([jnp.where(head_a_k, 1.0, 0.0), jnp.where(head_a_k, 0.0, 1.0)], axis=0).astype(BF16)

    f32_scr[0] = q_ref[0].astype(F32)
    f32_scr[1] = k_ref[0].astype(F32)
    f32_scr[2] = v_ref[0].astype(F32)
    kp[0:BLOCK, :] = jnp.zeros((BLOCK, LANES), BF16)
    vp[0:BLOCK, :] = jnp.zeros((BLOCK, LANES), BF16)

    for p, (window, dil) in enumerate(DIL_PATTERNS):
        assert window // dil == BLOCK
        length = seq // dil
        per_stream = length // BLOCK
        assert length % BLOCK == 0
        if dil == 1:
            qp[...] = q_ref[0]
            kp[BLOCK:, :] = k_ref[0]
            vp[BLOCK:, :] = v_ref[0]
        else:
            for r in range(dil):
                rows = pl.ds(r, length, stride=dil)
                qp[r * length:(r + 1) * length, :] = f32_scr[0, rows, :].astype(BF16)
                kp[BLOCK + r * length:BLOCK + (r + 1) * length, :] = f32_scr[1, rows, :].astype(BF16)
                vp[BLOCK + r * length:BLOCK + (r + 1) * length, :] = f32_scr[2, rows, :].astype(BF16)
        o_dst = onat.at[p] if dil == 1 else operm
        l_dst = lnat.at[p] if dil == 1 else lperm

        def block_body(bi, carry, per_stream=per_stream, o_dst=o_dst, l_dst=l_dst):
            row0 = pl.multiple_of(bi * BLOCK, BLOCK)
            qb = qp[pl.ds(row0, BLOCK), :]
            kw = kp[pl.ds(row0, 2 * BLOCK), :]
            vw = vp[pl.ds(row0, 2 * BLOCK), :]
            has_prev = (bi % per_stream) != 0
            valid = band & (cur_only | has_prev)
            kk = jnp.concatenate([jnp.where(head_a_k, kw, zero_b), jnp.where(head_a_k, zero_b, kw)], axis=0)
            s2 = lax.dot_general(qb, kk, (((1,), (1,)), ((), ())), preferred_element_type=F32)
            ps, ms = [], []
            for h in range(2):
                s = jnp.where(valid, s2[:, h * 2 * BLOCK:(h + 1) * 2 * BLOCK], -jnp.inf)
                m = jnp.max(s, axis=-1, keepdims=True)
                ps.append(jnp.exp2(s - m).astype(BF16))
                ms.append(m)
            vv = jnp.concatenate([jnp.where(head_a_k, vw, zero_b), jnp.where(head_a_k, zero_b, vw)], axis=0)
            o_den = jnp.dot(jnp.concatenate(ps, axis=1), jnp.concatenate([vv, head_ones], axis=1),
                            preferred_element_type=F32)
            o = o_den[:, :LANES]
            den = o_den[:, LANES:]
            o_dst[pl.ds(row0, BLOCK), :] = o / den
            l_dst[pl.ds(row0, BLOCK), :] = jnp.where(head_a_q, ms[0], ms[1]) + jnp.log2(den)
            return carry

        lax.fori_loop(0, n_blocks, block_body, 0, unroll=DIL_UNROLL)
        if dil != 1:
            for r in range(dil):
                rows = pl.ds(r, length, stride=dil)
                onat[p, rows, :] = operm[r * length:(r + 1) * length, :]
                lnat[p, rows, :] = lperm[r * length:(r + 1) * length, :]

    def mix_body(c, carry):
        rows = pl.ds(pl.multiple_of(c * BLOCK, BLOCK), BLOCK)
        ls = [lnat[p, rows, :] for p in range(len(DIL_PATTERNS))]
        m = functools.reduce(jnp.maximum, ls)
        es = [jnp.exp2(l - m) for l in ls]
        num = functools.reduce(lambda a, b: a + b, [e * onat[p, rows, :] for p, e in enumerate(es)])
        o_ref[0, rows, :] = num / functools.reduce(lambda a, b: a + b, es)
        return carry

    lax.fori_loop(0, n_blocks, mix_body, 0)


def _dilated_attention(proj3, *, half_width):
    b, seq, _ = proj3.shape
    pairs = half_width // LANES
    n_pat = len(DIL_PATTERNS)
    spec = lambda off: pl.BlockSpec((1, seq, LANES), lambda bi, c, off=off: (bi, 0, off + c))
    return pl.pallas_call(
        functools.partial(_dil_kernel, seq=seq),
        grid=(b, pairs),
        in_specs=[spec(0), spec(pairs), spec(2 * pairs)],
        out_specs=pl.BlockSpec((1, seq, LANES), lambda bi, c: (bi, 0, c)),
        out_shape=jax.ShapeDtypeStruct((b, seq, half_width), F32),
        scratch_shapes=[
            pltpu.VMEM((3, seq, LANES), F32),
            pltpu.VMEM((seq, LANES), BF16),
            pltpu.VMEM((seq + BLOCK, LANES), BF16),
            pltpu.VMEM((seq + BLOCK, LANES), BF16),
            pltpu.VMEM((seq, LANES), F32),
            pltpu.VMEM((seq, LANES), F32),
            pltpu.VMEM((n_pat, seq, LANES), F32),
            pltpu.VMEM((n_pat, seq, LANES), F32),
        ],
        compiler_params=_cparams(("arbitrary", "arbitrary")),
        name="dilated_attention",
    )(proj3, proj3, proj3)


def _sb_kernel(q_ref, k_ref, v_ref, o_ref, kk_scr, vv_scr, tri_scr, acc_scr, carry_scr, *, seq, qt):
    n_blocks = seq // BLOCK
    per_tile = qt // BLOCK
    lane_k = lax.broadcasted_iota(jnp.int32, (BLOCK, LANES), 1)
    head_a = lane_k < HEAD_DIM
    zero_b = jnp.zeros((), BF16)

    def prep(jb, carry):
        rows = pl.ds(pl.multiple_of(jb * BLOCK, BLOCK), BLOCK)
        kb = k_ref[0, rows, :]
        vb = v_ref[0, rows, :]
        kk_scr[jb, 0:BLOCK, :] = jnp.where(head_a, kb, zero_b)
        kk_scr[jb, BLOCK:, :] = jnp.where(head_a, zero_b, kb)
        vv_scr[jb, 0:BLOCK, :] = jnp.where(head_a, vb, zero_b)
        vv_scr[jb, BLOCK:, :] = jnp.where(head_a, zero_b, vb)
        return carry

    lax.fori_loop(0, n_blocks, prep, 0)

    jj = lax.broadcasted_iota(jnp.int32, (2 * BLOCK, 2 * BLOCK), 0) % BLOCK
    ss = lax.broadcasted_iota(jnp.int32, (2 * BLOCK, 2 * BLOCK), 1)
    tri_scr[...] = jnp.where((ss >= BLOCK) | (jj > ss), 1.0, 0.0).astype(BF16)

    row = lax.broadcasted_iota(jnp.int32, (qt, 2 * BLOCK), 0)
    col = lax.broadcasted_iota(jnp.int32, (qt, 2 * BLOCK), 1) % BLOCK

    def key_block(qb, jb, key_off):
        z = lax.dot_general(qb, kk_scr[jb], (((1,), (1,)), ((), ())), preferred_element_type=F32)
        neg_abs = lax.bitcast_convert_type(lax.bitcast_convert_type(z, jnp.uint32) | jnp.uint32(0x80000000), F32)
        ls = jnp.minimum(z, 0.0) - jnp.log2(1.0 + jnp.exp2(neg_abs))
        lr = ls - z
        if key_off is not None:
            causal = (col + key_off) < row
            lr = jnp.where(causal, lr, 0.0)
        hi = lr.astype(BF16)
        lo = (lr - hi.astype(F32)).astype(BF16)
        tri = tri_scr[...]
        a_heads = []
        for h in range(2):
            sl = slice(h * BLOCK, (h + 1) * BLOCK)
            lhs = jnp.concatenate([hi[:, sl], lo[:, sl]], axis=1)
            cs_tot = jnp.dot(lhs, tri, preferred_element_type=F32)
            carry = carry_scr[:, sl]
            a = jnp.exp2(ls[:, sl] + cs_tot[:, :BLOCK] + carry)
            if key_off is not None:
                a = jnp.where(causal[:, :BLOCK], a, 0.0)
            carry_scr[:, sl] = carry + cs_tot[:, BLOCK:]
            a_heads.append(a.astype(BF16))
        acc_scr[...] += jnp.dot(jnp.concatenate(a_heads, axis=1), vv_scr[jb], preferred_element_type=F32)

    def q_tile(i, carry):
        rows = pl.ds(pl.multiple_of(i * qt, qt), qt)
        qb = q_ref[0, rows, :]
        acc_scr[...] = jnp.zeros_like(acc_scr)
        carry_scr[...] = jnp.zeros_like(carry_scr)
        for t in reversed(range(per_tile)):
            key_block(qb, i * per_tile + t, t * BLOCK)

        def inner(t, c):
            for s in range(SB_UNROLL):
                key_block(qb, i * per_tile - 1 - (t * SB_UNROLL + s), None)
            return c

        lax.fori_loop(0, i * (per_tile // SB_UNROLL), inner, 0)
        o_ref[0, rows, :] = acc_scr[...]
        return carry

    lax.fori_loop(0, seq // qt, q_tile, 0)


def _stickbreak_attention(proj3, *, half_width, qt):
    b, seq, _ = proj3.shape
    pairs = half_width // LANES
    n_blocks = seq // BLOCK
    assert seq % qt == 0 and qt % BLOCK == 0
    spec = lambda off: pl.BlockSpec((1, seq, LANES), lambda bi, c, off=off: (bi, 0, off + c))
    return pl.pallas_call(
        functools.partial(_sb_kernel, seq=seq, qt=qt),
        grid=(b, pairs),
        in_specs=[spec(3 * pairs), spec(4 * pairs), spec(5 * pairs)],
        out_specs=pl.BlockSpec((1, seq, LANES), lambda bi, c: (bi, 0, c)),
        out_shape=jax.ShapeDtypeStruct((b, seq, half_width), F32),
        scratch_shapes=[
            pltpu.VMEM((n_blocks, 2 * BLOCK, LANES), BF16),
            pltpu.VMEM((n_blocks, 2 * BLOCK, LANES), BF16),
            pltpu.VMEM((2 * BLOCK, 2 * BLOCK), BF16),
            pltpu.VMEM((qt, LANES), F32),
            pltpu.VMEM((qt, 2 * BLOCK), F32),
        ],
        compiler_params=_cparams(("arbitrary", "arbitrary")),
        name="stickbreak_attention",
    )(proj3, proj3, proj3)


def _split_bf16(x):
    hi = x.astype(BF16)
    return hi, (x - hi.astype(F32)).astype(BF16)


def _outproj_kernel(od_ref, os_ref, x_ref, w_ref, gd_ref, gs_ref, gf_ref, wr_ref, br_ref,
                    x2_ref, h2_ref, idx_ref, gate_ref, acc_d, acc_s, ssq_d, ssq_s, *, nk, half_width):
    k = pl.program_id(1)

    @pl.when(k == 0)
    def _():
        acc_d[...] = jnp.zeros_like(acc_d)
        acc_s[...] = jnp.zeros_like(acc_s)
        ssq_d[...] = jnp.zeros_like(ssq_d)
        ssq_s[...] = jnp.zeros_like(ssq_s)

    w = w_ref[...].astype(BF16)

    def accumulate(a_ref, g_ref, acc, ssq):
        a = a_ref[...]
        ssq[...] += jnp.sum(a * a, axis=-1, keepdims=True)
        acc[...] += jnp.dot((a * g_ref[...]).astype(BF16), w, preferred_element_type=F32)

    @pl.when(k < nk // 2)
    def _():
        accumulate(od_ref, gd_ref, acc_d, ssq_d)

    @pl.when(k >= nk // 2)
    def _():
        accumulate(os_ref, gs_ref, acc_s, ssq_s)

    @pl.when(k == nk - 1)
    def _():
        r_d = lax.rsqrt(ssq_d[...] / half_width + NORM_EPS)
        r_s = lax.rsqrt(ssq_s[...] / half_width + NORM_EPS)
        x2 = x_ref[...] + (acc_d[...] * r_d + acc_s[...] * r_s)
        x2_ref[...] = x2
        h2 = (x2 * lax.rsqrt(jnp.mean(x2 * x2, axis=-1, keepdims=True) + NORM_EPS)) * gf_ref[...]
        packed = _pack_pairs(h2)
        n_chunk = packed.shape[1] // LANES
        for ch in range(n_chunk):
            h2_ref[pl.ds(ch, packed.shape[0], stride=n_chunk), :] = packed[:, ch * LANES:(ch + 1) * LANES]
        h_hi, h_lo = _split_bf16(h2)
        w_hi, w_lo = _split_bf16(wr_ref[...])
        logits = (jnp.dot(h_hi, w_hi, preferred_element_type=F32)
                  + jnp.dot(h_hi, w_lo, preferred_element_type=F32)
                  + jnp.dot(h_lo, w_hi, preferred_element_type=F32)) + br_ref[...]
        tm = logits.shape[0]
        e_iota = lax.broadcasted_iota(jnp.int32, logits.shape, 1)
        out_lane = lax.broadcasted_iota(jnp.int32, (tm, LANES), 1)
        idx_out = jnp.zeros((tm, LANES), jnp.int32)
        val_out = jnp.full((tm, LANES), -jnp.inf, F32)
        top0 = None
        for kk in range(TOP_K):
            m = jnp.max(logits, axis=-1, keepdims=True)
            sel = jnp.min(jnp.where(logits == m, e_iota, N_EXPERTS), axis=-1, keepdims=True)
            idx_out = jnp.where(out_lane == kk, sel, idx_out)
            val_out = jnp.where(out_lane == kk, m, val_out)
            logits = jnp.where(e_iota == sel, -jnp.inf, logits)
            if kk == 0:
                top0 = m
        ex = jnp.exp(val_out - top0)
        gate_ref[...] = ex / jnp.sum(ex, axis=-1, keepdims=True)
        idx_ref[...] = idx_out


def _outproj_router(o_d, o_s, x2d, w_out, g_d, g_s, g_f, w_router, b_router, *, tm, tk):
    m, d = x2d.shape
    half_width = d // 2
    nk = d // tk
    assert half_width % tk == 0 and m % tm == 0
    kh = nk // 2
    return pl.pallas_call(
        functools.partial(_outproj_kernel, nk=nk, half_width=half_width),
        grid=(m // tm, nk),
        in_specs=[
            pl.BlockSpec((tm, tk), lambda i, k: (i, jnp.minimum(k, kh - 1))),
            pl.BlockSpec((tm, tk), lambda i, k: (i, jnp.maximum(k - kh, 0))),
            pl.BlockSpec((tm, d), lambda i, k: (i, 0)),
            pl.BlockSpec((tk, d), lambda i, k: (k, 0)),
            pl.BlockSpec((1, tk), lambda i, k: (0, jnp.minimum(k, kh - 1))),
            pl.BlockSpec((1, tk), lambda i, k: (0, jnp.maximum(k - kh, 0))),
            pl.BlockSpec((1, d), lambda i, k: (0, 0)),
            pl.BlockSpec((d, N_EXPERTS), lambda i, k: (0, 0)),
            pl.BlockSpec((1, N_EXPERTS), lambda i, k: (0, 0)),
        ],
        out_specs=[
            pl.BlockSpec((tm, d), lambda i, k: (i, 0)),
            pl.BlockSpec((tm * (d // 2 // LANES), LANES), lambda i, k: (i, 0)),
            pl.BlockSpec((tm, LANES), lambda i, k: (i, 0)),
            pl.BlockSpec((tm, LANES), lambda i, k: (i, 0)),
        ],
        out_shape=[
            jax.ShapeDtypeStruct((m, d), F32),
            jax.ShapeDtypeStruct((m * (d // 2 // LANES), LANES), jnp.uint32),
            jax.ShapeDtypeStruct((m, LANES), jnp.int32),
            jax.ShapeDtypeStruct((m, LANES), F32),
        ],
        scratch_shapes=[
            pltpu.VMEM((tm, d), F32),
            pltpu.VMEM((tm, d), F32),
            pltpu.VMEM((tm, 1), F32),
            pltpu.VMEM((tm, 1), F32),
        ],
        compiler_params=_cparams(("arbitrary", "arbitrary")),
        name="outproj_router",
    )(o_d, o_s, x2d, w_out, g_d, g_s, g_f, w_router, b_router)


def _unpack_pairs(words):
    hi = lax.bitcast_convert_type(words & jnp.uint32(0xFFFF0000), F32)
    lo = lax.bitcast_convert_type(words << 16, F32)
    return hi.astype(BF16), lo.astype(BF16)


def _pack_pairs(x):
    c = x.shape[1] // 2
    as_bits = lambda v: lax.bitcast_convert_type(v.astype(BF16).astype(F32), jnp.uint32)
    return as_bits(x[:, :c]) | (as_bits(x[:, c:]) >> 16)


def _dispatch_kernel(row_tok, h_hbm, o_ref, h_vmem, sem, *, rows_blk, n_chunk):
    i = pl.program_id(0)

    @pl.when(i == 0)
    def _():
        cp = pltpu.make_async_copy(h_hbm, h_vmem, sem)
        cp.start()
        cp.wait()

    base = i * rows_blk

    def body(r, c):
        tok = row_tok[base + r]
        src = pl.ds(pl.multiple_of(tok * n_chunk, n_chunk), n_chunk)
        o_ref[pl.ds(pl.multiple_of(r * n_chunk, n_chunk), n_chunk), :] = h_vmem[src, :]
        return c

    lax.fori_loop(0, rows_blk, body, 0, unroll=8)


def _dispatch(hp, row_tok, *, rows_blk, n_chunk):
    n_rows = row_tok.shape[0]
    assert n_rows % rows_blk == 0
    grid_spec = pltpu.PrefetchScalarGridSpec(
        num_scalar_prefetch=1,
        grid=(n_rows // rows_blk,),
        in_specs=[pl.BlockSpec(memory_space=pl.ANY)],
        out_specs=pl.BlockSpec((rows_blk * n_chunk, LANES), lambda i, row_tok: (i, 0)),
        scratch_shapes=[pltpu.VMEM(hp.shape, hp.dtype), pltpu.SemaphoreType.DMA],
    )
    return pl.pallas_call(
        functools.partial(_dispatch_kernel, rows_blk=rows_blk, n_chunk=n_chunk),
        grid_spec=grid_spec,
        out_shape=jax.ShapeDtypeStruct((n_rows * n_chunk, LANES), hp.dtype),
        compiler_params=_cparams(("arbitrary",)),
        name="dispatch",
    )(row_tok, hp)


def _expert_kernel(item_e, item_r0, item_nb, n_items,
                   xs_hbm, wg_ref, bg_ref, wu_ref, bu_ref, wd_ref, bd_ref,
                   ys_hbm, stage, xb16, acc, pending, sem_in, sem_out, *, nj):
    i = pl.program_id(0)
    j = pl.program_id(1)
    n_live = n_items[0]
    active = i < n_live
    r0 = item_r0[i]
    n_align = item_nb[i]
    n_rows = n_align * ROW_ALIGN
    per_trip = ROW_BLOCK // ROW_ALIGN
    n_main = n_align // per_trip
    nxt = jnp.minimum(i + 1, pl.num_programs(0) - 1)
    r0_next = item_r0[nxt]
    blocks_next = jnp.where(i + 1 < n_live, item_nb[nxt], 0)
    d = acc.shape[1]
    n_chunk = d // 2 // LANES
    blk_words = ROW_ALIGN * n_chunk

    def in_copy(base_row, blk):
        src = xs_hbm.at[pl.ds(pl.multiple_of((base_row + blk * ROW_ALIGN) * n_chunk, blk_words), blk_words)]
        return pltpu.make_async_copy(src, stage.at[pl.ds(pl.multiple_of(blk * blk_words, blk_words), blk_words)],
                                     sem_in)

    def start_in(base_row, n_blk):
        def issue(blk, c):
            in_copy(base_row, blk).start()
            return c

        lax.fori_loop(0, n_blk, issue, 0)

    def out_copy(blk):
        src = acc.at[pl.ds(pl.multiple_of(blk * ROW_ALIGN, ROW_ALIGN), ROW_ALIGN)]
        dst = ys_hbm.at[pl.ds(pl.multiple_of(r0 + blk * ROW_ALIGN, ROW_ALIGN), ROW_ALIGN)]
        return pltpu.make_async_copy(src, dst, sem_out)

    def drain():
        @pl.when(pending[0] > 0)
        def _():
            rows = pl.ds(0, pending[0] * ROW_ALIGN)
            pltpu.make_async_copy(acc.at[rows], ys_hbm.at[rows], sem_out).wait()
            pending[0] = 0

    @pl.when((i == 0) & (j == 0))
    def _():
        pending[0] = 0
        start_in(r0, n_align)

    @pl.when(active & (j == 0))
    def _():
        all_rows = pl.ds(0, n_rows * n_chunk)
        pltpu.make_async_copy(xs_hbm.at[all_rows], stage.at[all_rows], sem_in).wait()
        drain()

        def unpack(rb, c):
            rows = pl.ds(pl.multiple_of(rb * ROW_ALIGN, ROW_ALIGN), ROW_ALIGN)
            for ch in range(n_chunk):
                words = stage[pl.ds(rb * (ROW_ALIGN * n_chunk) + ch, ROW_ALIGN, stride=n_chunk), :]
                hi, lo = _unpack_pairs(words)
                xb16[rows, ch * LANES:(ch + 1) * LANES] = hi
                xb16[rows, d // 2 + ch * LANES:d // 2 + (ch + 1) * LANES] = lo
            acc[rows, :] = jnp.broadcast_to(bd_ref[0], (ROW_ALIGN, d))
            return c

        lax.fori_loop(0, n_align, unpack, 0)
        start_in(r0_next, blocks_next)

    def trip(row0, n_blk):
        rows = pl.ds(row0, n_blk * ROW_ALIGN)
        xb = xb16[rows, :]
        g = jnp.dot(xb, wg_ref[0].astype(BF16), preferred_element_type=F32) + bg_ref[0]
        u = jnp.dot(xb, wu_ref[0].astype(BF16), preferred_element_type=F32) + bu_ref[0]
        g = jnp.minimum(g, SWIGLU_LIMIT)
        u = jnp.clip(u, -SWIGLU_LIMIT, SWIGLU_LIMIT)
        a = (u + 1.0) * (g * jax.nn.sigmoid(SWIGLU_ALPHA * g))
        acc[rows, :] += jnp.dot(a.astype(BF16), wd_ref[0].astype(BF16), preferred_element_type=F32)

        @pl.when(j == nj - 1)
        def _():
            for s in range(n_blk):
                out_copy(row0 // ROW_ALIGN + s).start()
            pending[0] = pending[0] + n_blk

    @pl.when(active)
    def _():
        def main_trip(rb, c):
            trip(pl.multiple_of(rb * ROW_BLOCK, ROW_BLOCK), per_trip)
            return c

        lax.fori_loop(0, n_main, main_trip, 0)

        def tail_trip(t, c):
            trip(pl.multiple_of((n_main * per_trip + t) * ROW_ALIGN, ROW_ALIGN), 1)
            return c

        lax.fori_loop(0, n_align - n_main * per_trip, tail_trip, 0)

    @pl.when((i == pl.num_programs(0) - 1) & (j == nj - 1))
    def _():
        drain()
        acc[0:ROW_ALIGN, :] = jnp.zeros((ROW_ALIGN, d), F32)
        used = n_items[1]

        def tail_copy(t):
            dst = ys_hbm.at[pl.ds(pl.multiple_of((used + t) * ROW_ALIGN, ROW_ALIGN), ROW_ALIGN)]
            return pltpu.make_async_copy(acc.at[pl.ds(0, ROW_ALIGN)], dst, sem_out)

        def issue(t, c):
            tail_copy(t).start()
            return c

        n_tail = ys_hbm.shape[0] // ROW_ALIGN - used
        lax.fori_loop(0, n_tail, issue, 0)

        def wait(t, c):
            tail_copy(t).wait()
            return c

        lax.fori_loop(0, n_tail, wait, 0)


def _experts(xs, w_gate, b_gate, w_up, b_up, w_down, b_down, meta, *, n_rows, th):
    item_e, item_r0, item_nb, n_items = meta
    n_exp, d, dh = w_gate.shape
    nj = dh // th
    n_item_slots = item_e.shape[0]

    def w_col(i, j, item_e, item_r0, item_nb, n_items):
        return (item_e[i], 0, jnp.where(i < n_items[0], j, nj - 1))

    def w_row(i, j, item_e, item_r0, item_nb, n_items):
        return (item_e[i], jnp.where(i < n_items[0], j, nj - 1), 0)

    def b_all(i, j, item_e, item_r0, item_nb, n_items):
        return (item_e[i], 0, 0)

    grid_spec = pltpu.PrefetchScalarGridSpec(
        num_scalar_prefetch=4,
        grid=(n_item_slots, nj),
        in_specs=[
            pl.BlockSpec(memory_space=pl.ANY),
            pl.BlockSpec((1, d, th), w_col),
            pl.BlockSpec((1, 1, th), w_col),
            pl.BlockSpec((1, d, th), w_col),
            pl.BlockSpec((1, 1, th), w_col),
            pl.BlockSpec((1, th, d), w_row),
            pl.BlockSpec((1, 1, d), b_all),
        ],
        out_specs=pl.BlockSpec(memory_space=pl.ANY),
        scratch_shapes=[
            pltpu.VMEM((ITEM_ROWS * (d // 2 // LANES), LANES), jnp.uint32),
            pltpu.VMEM((ITEM_ROWS, d), BF16),
            pltpu.VMEM((ITEM_ROWS, d), F32),
            pltpu.SMEM((1,), jnp.int32),
            pltpu.SemaphoreType.DMA,
            pltpu.SemaphoreType.DMA,
        ],
    )
    return pl.pallas_call(
        functools.partial(_expert_kernel, nj=nj),
        grid_spec=grid_spec,
        out_shape=jax.ShapeDtypeStruct((n_rows, d), F32),
        compiler_params=_cparams(("arbitrary", "arbitrary")),
        name="experts",
    )(item_e, item_r0, item_nb, n_items,
      xs, w_gate, b_gate.reshape(n_exp, 1, dh), w_up, b_up.reshape(n_exp, 1, dh), w_down,
      b_down.reshape(n_exp, 1, d))


def _routing_metadata(top_idx, n_tok):
    n_assign = n_tok * TOP_K
    flat_e = top_idx.reshape(n_assign)
    onehot = (flat_e[:, None] == jnp.arange(N_EXPERTS, dtype=jnp.int32)[None, :]).astype(jnp.int32)
    rank = jnp.take_along_axis(jnp.cumsum(onehot, axis=0), flat_e[:, None], axis=1)[:, 0] - 1
    counts = jnp.sum(onehot, axis=0)
    padded = (counts + ROW_ALIGN - 1) // ROW_ALIGN * ROW_ALIGN
    pend = jnp.cumsum(padded)
    pstart = pend - padded
    dest = (pstart[flat_e] + rank).astype(jnp.int32)
    n_rows = (n_assign // ROW_ALIGN + N_EXPERTS) * ROW_ALIGN
    flat_tok = jnp.arange(n_assign, dtype=jnp.int32) // TOP_K
    row_tok = jnp.zeros((n_rows,), jnp.int32).at[dest].set(flat_tok)
    n_chunks = (padded + ITEM_ROWS - 1) // ITEM_ROWS
    cend = jnp.cumsum(n_chunks)
    cstart = cend - n_chunks
    n_slots = N_EXPERTS + n_assign // ITEM_ROWS + 1
    slot = jnp.arange(n_slots, dtype=jnp.int32)
    n_items = cend[-1]
    last_e = jnp.max(jnp.where(n_chunks > 0, jnp.arange(N_EXPERTS), 0))
    e_of = jnp.minimum(jnp.searchsorted(cend, slot, side='right'), N_EXPERTS - 1)
    live = slot < n_items
    item_e = jnp.where(live, e_of, last_e).astype(jnp.int32)
    chunk = slot - cstart[e_of]
    item_r0 = jnp.where(live, pstart[e_of] + chunk * ITEM_ROWS, 0).astype(jnp.int32)
    item_rows = jnp.clip(padded[e_of] - chunk * ITEM_ROWS, 0, ITEM_ROWS)
    item_nb = jnp.where(live, item_rows // ROW_ALIGN, 0).astype(jnp.int32)
    counts_blk = jnp.stack([n_items, pend[-1] // ROW_ALIGN]).astype(jnp.int32)
    meta = (item_e, item_r0, item_nb, counts_blk)
    return meta, row_tok, dest, n_rows


def _combine_kernel(dest, ys_hbm, x2_ref, gate_ref, g_ref, o_ref, buf, sem, *, tm):
    i = pl.program_id(0)
    n = pl.num_programs(0)

    def row_copy(tile, slot, t, kk):
        r = dest[(tile * tm + t) * TOP_K + kk]
        return pltpu.make_async_copy(ys_hbm.at[pl.ds(r, 1)], buf.at[slot, kk, pl.ds(t, 1)], sem.at[slot])

    def issue_tile(tile, slot):
        def issue(t, c):
            for kk in range(TOP_K):
                row_copy(tile, slot, t, kk).start()
            return c

        lax.fori_loop(0, tm, issue, 0, unroll=8)

    slot = i % 2

    @pl.when(i == 0)
    def _():
        issue_tile(0, 0)

    @pl.when(i + 1 < n)
    def _():
        issue_tile(i + 1, 1 - slot)

    for kk in range(TOP_K):
        pltpu.make_async_copy(ys_hbm.at[pl.ds(0, tm)], buf.at[slot, kk], sem.at[slot]).wait()

    gates = gate_ref[...]
    y = x2_ref[...]
    for kk in range(TOP_K):
        y = y + gates[:, kk:kk + 1] * buf[slot, kk]
    o_ref[...] = (y * lax.rsqrt(jnp.mean(y * y, axis=-1, keepdims=True) + NORM_EPS)) * g_ref[...]


def _combine(dest, ys, x2, gates, g_final, *, tm):
    m, d = x2.shape
    grid_spec = pltpu.PrefetchScalarGridSpec(
        num_scalar_prefetch=1,
        grid=(m // tm,),
        in_specs=[
            pl.BlockSpec(memory_space=pl.ANY),
            pl.BlockSpec((tm, d), lambda i, dest: (i, 0)),
            pl.BlockSpec((tm, LANES), lambda i, dest: (i, 0)),
            pl.BlockSpec((1, d), lambda i, dest: (0, 0)),
        ],
        out_specs=pl.BlockSpec((tm, d), lambda i, dest: (i, 0)),
        scratch_shapes=[pltpu.VMEM((2, TOP_K, tm, d), F32), pltpu.SemaphoreType.DMA((2,))],
    )
    return pl.pallas_call(
        functools.partial(_combine_kernel, tm=tm),
        grid_spec=grid_spec,
        out_shape=jax.ShapeDtypeStruct((m, d), F32),
        compiler_params=_cparams(("arbitrary",)),
        name="combine",
    )(dest, ys, x2, gates, g_final)


def _layer(x, attn_norm, w_in, dil_out_norm, sb_out_norm, w_out, ffn_norm,
           w_router, b_router, w_gate, b_gate, w_up, b_up, w_down, b_down, *, tiles):
    b, seq, d = x.shape
    m = b * seq
    half_width = d // 2
    x2d = x.reshape(m, d)
    cos_t, sin_t = _rope_tables(seq)
    proj = _inproj(x2d, attn_norm.reshape(1, d), w_in, cos_t, sin_t,
                   seq=seq, tm=tiles["in_tm"], tn=tiles["in_tn"])
    proj3 = proj.reshape(b, seq, 3 * d)
    o_d = _dilated_attention(proj3, half_width=half_width).reshape(m, half_width)
    o_s = _stickbreak_attention(proj3, half_width=half_width, qt=tiles["sb_qt"]).reshape(m, half_width)
    x2, h2, idx_l, gate_l = _outproj_router(
        o_d, o_s, x2d, w_out, dil_out_norm.reshape(1, half_width), sb_out_norm.reshape(1, half_width),
        ffn_norm.reshape(1, d), w_router, b_router.reshape(1, N_EXPERTS),
        tm=tiles["out_tm"], tk=tiles["out_tk"])
    meta, row_tok, dest, n_rows = _routing_metadata(idx_l[:, :TOP_K], m)
    xs = _dispatch(h2, row_tok, rows_blk=tiles["disp_rows"], n_chunk=d // 2 // LANES)
    ys = _experts(xs, w_gate, b_gate, w_up, b_up, w_down, b_down, meta, n_rows=n_rows, th=tiles["moe_th"])
    return x2, ys, dest, gate_l


_TILES = dict(in_tm=1024, in_tn=512, sb_qt=512, out_tm=512, out_tk=512, disp_rows=1024, moe_th=512, cmb_tm=128)


def kernel(x, attn_norm, w_in, dil_out_norm, sb_out_norm, w_out, ffn_norm, w_router, b_router,
           w_gate, b_gate, w_up, b_up, w_down, b_down, final_norm):
    depth = attn_norm.shape[0]
    b, seq, d = x.shape
    for l in range(depth):
        x2, ys, dest, gate_l = _layer(
            x, attn_norm[l], w_in[l], dil_out_norm[l], sb_out_norm[l], w_out[l], ffn_norm[l],
            w_router[l], b_router[l], w_gate[l], b_gate[l], w_up[l], b_up[l], w_down[l], b_down[l],
            tiles=_TILES)
        assert depth == 1
        x = _combine(dest, ys, x2, gate_l, final_norm.reshape(1, d), tm=_TILES["cmb_tm"])
    return x.reshape(b, seq, d)
```

```python
import functools
import math

import jax
import jax.numpy as jnp
from jax import lax
from jax.experimental import pallas as pl
from jax.experimental.pallas import tpu as pltpu

F32 = jnp.float32
BF16 = jnp.bfloat16

HEAD_DIM = 64
LANES = 128
BLOCK = 128
DIL_PATTERNS = ((128, 1), (512, 4), (2048, 16))
ROPE_THETA = 10000.0
N_EXPERTS = 32
TOP_K = 4
SWIGLU_LIMIT = 7.0
SWIGLU_ALPHA = 1.702
NORM_EPS = 1e-5
Q_SCALE = HEAD_DIM ** -0.5 * math.log2(math.e)
SB_UNROLL = 4
DIL_UNROLL = 8
ROW_ALIGN = 128
ROW_BLOCK = 512
ITEM_ROWS = 1280
VMEM_LIMIT = 56 * 1024 * 1024


def _cparams(sem):
    return pltpu.CompilerParams(dimension_semantics=sem, vmem_limit_bytes=VMEM_LIMIT)


def _inproj_kernel(x_ref, g_ref, w_ref, cos_ref, sin_ref, o_ref, h_scr, *, tn, half_width):
    j = pl.program_id(1)

    @pl.when(j == 0)
    def _():
        x = x_ref[...]
        y = x * lax.rsqrt(jnp.mean(x * x, axis=-1, keepdims=True) + NORM_EPS)
        h_scr[...] = (y * g_ref[...]).astype(BF16)

    acc = jnp.dot(h_scr[...], w_ref[...].astype(BF16), preferred_element_type=F32)
    section = (j * tn) // half_width
    scale = jnp.where((section == 0) | (section == 3), Q_SCALE, 1.0).astype(F32)

    @pl.when(section < 2)
    def _():
        lane = lax.broadcasted_iota(jnp.int32, (acc.shape[0], LANES), 1)
        first_half = (lane % HEAD_DIM) < (HEAD_DIM // 2)
        c = cos_ref[...]
        s = sin_ref[...]
        for g in range(tn // LANES):
            xg = acc[:, g * LANES:(g + 1) * LANES]
            partner = jnp.where(first_half,
                                pltpu.roll(xg, LANES - HEAD_DIM // 2, 1),
                                pltpu.roll(xg, HEAD_DIM // 2, 1))
            o_ref[:, g * LANES:(g + 1) * LANES] = ((xg * c + partner * s) * scale).astype(BF16)

    @pl.when(section >= 2)
    def _():
        o_ref[...] = (acc * scale).astype(BF16)


def _inproj(x2d, g, w_in, cos_t, sin_t, *, seq, tm, tn):
    m, d = x2d.shape
    n = w_in.shape[1]
    half_width = d // 2
    assert half_width % tn == 0 and tn % LANES == 0 and m % tm == 0 and seq % tm == 0
    pos_tiles = seq // tm
    return pl.pallas_call(
        functools.partial(_inproj_kernel, tn=tn, half_width=half_width),
        grid=(m // tm, n // tn),
        in_specs=[
            pl.BlockSpec((tm, d), lambda i, j: (i, 0)),
            pl.BlockSpec((1, d), lambda i, j: (0, 0)),
            pl.BlockSpec((d, tn), lambda i, j: (0, j)),
            pl.BlockSpec((tm, LANES), lambda i, j: (i % pos_tiles, 0)),
            pl.BlockSpec((tm, LANES), lambda i, j: (i % pos_tiles, 0)),
        ],
        out_specs=pl.BlockSpec((tm, tn), lambda i, j: (i, j)),
        out_shape=jax.ShapeDtypeStruct((m, n), BF16),
        scratch_shapes=[pltpu.VMEM((tm, d), BF16)],
        compiler_params=_cparams(("arbitrary", "arbitrary")),
        name="inproj",
    )(x2d, g, w_in, cos_t, sin_t)


def _rope_tables(seq):
    half = HEAD_DIM // 2
    inv_freq = 1.0 / (ROPE_THETA ** (jnp.arange(half, dtype=F32) * 2.0 / HEAD_DIM))
    ang = jnp.arange(seq, dtype=F32)[:, None] * inv_freq[None, :]
    cos = jnp.cos(ang)
    sin = jnp.sin(ang)
    reps = LANES // HEAD_DIM
    cos_t = jnp.tile(jnp.concatenate([cos, cos], axis=1), (1, reps))
    sin_t = jnp.tile(jnp.concatenate([-sin, sin], axis=1), (1, reps))
    return cos_t, sin_t


def _dil_kernel(q_ref, k_ref, v_ref, o_ref, f32_scr, qp, kp, vp, operm, lperm, onat, lnat, *, seq):
    n_blocks = seq // BLOCK
    lane_k = lax.broadcasted_iota(jnp.int32, (2 * BLOCK, LANES), 1)
    head_a_k = lane_k < HEAD_DIM
    lane_q = lax.broadcasted_iota(jnp.int32, (BLOCK, LANES), 1)
    head_a_q = lane_q < HEAD_DIM
    qi = lax.broadcasted_iota(jnp.int32, (BLOCK, 2 * BLOCK), 0)
    ki = lax.broadcasted_iota(jnp.int32, (BLOCK, 2 * BLOCK), 1)
    dist = qi + BLOCK - ki
    band = (dist >= 0) & (dist <= BLOCK)
    cur_only = ki >= BLOCK
    zero_b = jnp.zeros((), BF16)
    head_ones = jnp.concatenate([jnp.where(head_a_k, 1.0, 0.0), jnp.where(head_a_k, 0.0, 1.0)], axis=0).astype(BF16)

    f32_scr[0] = q_ref[0].astype(F32)
    f32_scr[1] = k_ref[0].astype(F32)
    f32_scr[2] = v_ref[0].astype(F32)
    kp[0:BLOCK, :] = jnp.zeros((BLOCK, LANES), BF16)
    vp[0:BLOCK, :] = jnp.zeros((BLOCK, LANES), BF16)

    for p, (window, dil) in enumerate(DIL_PATTERNS):
        assert window // dil == BLOCK
        length = seq // dil
        per_stream = length // BLOCK
        assert length % BLOCK == 0
        if dil == 1:
            qp[...] = q_ref[0]
            kp[BLOCK:, :] = k_ref[0]
            vp[BLOCK:, :] = v_ref[0]
        else:
            for r in range(dil):
                rows = pl.ds(r, length, stride=dil)
                qp[r * length:(r + 1) * length, :] = f32_scr[0, rows, :].astype(BF16)
                kp[BLOCK + r * length:BLOCK + (r + 1) * length, :] = f32_scr[1, rows, :].astype(BF16)
                vp[BLOCK + r * length:BLOCK + (r + 1) * length, :] = f32_scr[2, rows, :].astype(BF16)
        o_dst = onat.at[p] if dil == 1 else operm
        l_dst = lnat.at[p] if dil == 1 else lperm

        def block_body(bi, carry, per_stream=per_stream, o_dst=o_dst, l_dst=l_dst):
            row0 = pl.multiple_of(bi * BLOCK, BLOCK)
            qb = qp[pl.ds(row0, BLOCK), :]
            kw = kp[pl.ds(row0, 2 * BLOCK), :]
            vw = vp[pl.ds(row0, 2 * BLOCK), :]
            has_prev = (bi % per_stream) != 0
            valid = band & (cur_only | has_prev)
            kk = jnp.concatenate([jnp.where(head_a_k, kw, zero_b), jnp.where(head_a_k, zero_b, kw)], axis=0)
            s2 = lax.dot_general(qb, kk, (((1,), (1,)), ((), ())), preferred_element_type=F32)
            ps, ms = [], []
            for h in range(2):
                s = jnp.where(valid, s2[:, h * 2 * BLOCK:(h + 1) * 2 * BLOCK], -jnp.inf)
                m = jnp.max(s, axis=-1, keepdims=True)
                ps.append(jnp.exp2(s - m).astype(BF16))
                ms.append(m)
            vv = jnp.concatenate([jnp.where(head_a_k, vw, zero_b), jnp.where(head_a_k, zero_b, vw)], axis=0)
            o_den = jnp.dot(jnp.concatenate(ps, axis=1), jnp.concatenate([vv, head_ones], axis=1),
                            preferred_element_type=F32)
            o = o_den[:, :LANES]
            den = o_den[:, LANES:]
            o_dst[pl.ds(row0, BLOCK), :] = o / den
            l_dst[pl.ds(row0, BLOCK), :] = jnp.where(head_a_q, ms[0], ms[1]) + jnp.log2(den)
            return carry

        lax.fori_loop(0, n_blocks, block_body, 0, unroll=DIL_UNROLL)
        if dil != 1:
            for r in range(dil):
                rows = pl.ds(r, length, stride=dil)
                onat[p, rows, :] = operm[r * length:(r + 1) * length, :]
                lnat[p, rows, :] = lperm[r * length:(r + 1) * length, :]

    def mix_body(c, carry):
        rows = pl.ds(pl.multiple_of(c * BLOCK, BLOCK), BLOCK)
        ls = [lnat[p, rows, :] for p in range(len(DIL_PATTERNS))]
        m = functools.reduce(jnp.maximum, ls)
        es = [jnp.exp2(l - m) for l in ls]
        num = functools.reduce(lambda a, b: a + b, [e * onat[p, rows, :] for p, e in enumerate(es)])
        o_ref[0, rows, :] = num / functools.reduce(lambda a, b: a + b, es)
        return carry

    lax.fori_loop(0, n_blocks, mix_body, 0)


def _dilated_attention(proj3, *, half_width):
    b, seq, _ = proj3.shape
    pairs = half_width // LANES
    n_pat = len(DIL_PATTERNS)
    spec = lambda off: pl.BlockSpec((1, seq, LANES), lambda bi, c, off=off: (bi, 0, off + c))
    return pl.pallas_call(
        functools.partial(_dil_kernel, seq=seq),
        grid=(b, pairs),
        in_specs=[spec(0), spec(pairs), spec(2 * pairs)],
        out_specs=pl.BlockSpec((1, seq, LANES), lambda bi, c: (bi, 0, c)),
        out_shape=jax.ShapeDtypeStruct((b, seq, half_width), F32),
        scratch_shapes=[
            pltpu.VMEM((3, seq, LANES), F32),
            pltpu.VMEM((seq, LANES), BF16),
            pltpu.VMEM((seq + BLOCK, LANES), BF16),
            pltpu.VMEM((seq + BLOCK, LANES), BF16),
            pltpu.VMEM((seq, LANES), F32),
            pltpu.VMEM((seq, LANES), F32),
            pltpu.VMEM((n_pat, seq, LANES), F32),
            pltpu.VMEM((n_pat, seq, LANES), F32),
        ],
        compiler_params=_cparams(("arbitrary", "arbitrary")),
        name="dilated_attention",
    )(proj3, proj3, proj3)


def _sb_kernel(q_ref, k_ref, v_ref, o_ref, kk_scr, vv_scr, tri_scr, acc_scr, carry_scr, *, seq, qt):
    n_blocks = seq // BLOCK
    per_tile = qt // BLOCK
    lane_k = lax.broadcasted_iota(jnp.int32, (BLOCK, LANES), 1)
    head_a = lane_k < HEAD_DIM
    zero_b = jnp.zeros((), BF16)

    def prep(jb, carry):
        rows = pl.ds(pl.multiple_of(jb * BLOCK, BLOCK), BLOCK)
        kb = k_ref[0, rows, :]
        vb = v_ref[0, rows, :]
        kk_scr[jb, 0:BLOCK, :] = jnp.where(head_a, kb, zero_b)
        kk_scr[jb, BLOCK:, :] = jnp.where(head_a, zero_b, kb)
        vv_scr[jb, 0:BLOCK, :] = jnp.where(head_a, vb, zero_b)
        vv_scr[jb, BLOCK:, :] = jnp.where(head_a, zero_b, vb)
        return carry

    lax.fori_loop(0, n_blocks, prep, 0)

    jj = lax.broadcasted_iota(jnp.int32, (2 * BLOCK, 2 * BLOCK), 0) % BLOCK
    ss = lax.broadcasted_iota(jnp.int32, (2 * BLOCK, 2 * BLOCK), 1)
    tri_scr[...] = jnp.where((ss >= BLOCK) | (jj > ss), 1.0, 0.0).astype(BF16)

    def key_block(qb, jb, key_off):
        rs = slice(0 if key_off is None else key_off, qt)
        z = lax.dot_general(qb[rs], kk_scr[jb], (((1,), (1,)), ((), ())), preferred_element_type=F32)
        neg_abs = lax.bitcast_convert_type(lax.bitcast_convert_type(z, jnp.uint32) | jnp.uint32(0x80000000), F32)
        ls = jnp.minimum(z, 0.0) - jnp.log2(1.0 + jnp.exp2(neg_abs))
        lr = ls - z
        if key_off is not None:
            n = qt - key_off
            causal = (lax.broadcasted_iota(jnp.int32, (n, 2 * BLOCK), 1) % BLOCK
                      < lax.broadcasted_iota(jnp.int32, (n, 2 * BLOCK), 0))
            lr = jnp.where(causal, lr, 0.0)
        hi = lr.astype(BF16)
        lo = (lr - hi.astype(F32)).astype(BF16)
        tri = tri_scr[...]
        a_heads = []
        for h in range(2):
            sl = slice(h * BLOCK, (h + 1) * BLOCK)
            lhs = jnp.concatenate([hi[:, sl], lo[:, sl]], axis=1)
            cs_tot = jnp.dot(lhs, tri, preferred_element_type=F32)
            carry = carry_scr[rs, sl]
            a = jnp.exp2(ls[:, sl] + cs_tot[:, :BLOCK] + carry)
            if key_off is not None:
                a = jnp.where(causal[:, :BLOCK], a, 0.0)
            carry_scr[rs, sl] = carry + cs_tot[:, BLOCK:]
            a_heads.append(a.astype(BF16))
        acc_scr[rs, :] += jnp.dot(jnp.concatenate(a_heads, axis=1), vv_scr[jb], preferred_element_type=F32)

    def q_tile(i, carry):
        rows = pl.ds(pl.multiple_of(i * qt, qt), qt)
        qb = q_ref[0, rows, :]
        acc_scr[...] = jnp.zeros_like(acc_scr)
        carry_scr[...] = jnp.zeros_like(carry_scr)
        for t in reversed(range(per_tile)):
            key_block(qb, i * per_tile + t, t * BLOCK)

        def inner(t, c):
            for s in range(SB_UNROLL):
                key_block(qb, i * per_tile - 1 - (t * SB_UNROLL + s), None)
            return c

        lax.fori_loop(0, i * (per_tile // SB_UNROLL), inner, 0)
        o_ref[0, rows, :] = acc_scr[...]
        return carry

    lax.fori_loop(0, seq // qt, q_tile, 0)


def _stickbreak_attention(proj3, *, half_width, qt):
    b, seq, _ = proj3.shape
    pairs = half_width // LANES
    n_blocks = seq // BLOCK
    assert seq % qt == 0 and qt % BLOCK == 0
    spec = lambda off: pl.BlockSpec((1, seq, LANES), lambda bi, c, off=off: (bi, 0, off + c))
    return pl.pallas_call(
        functools.partial(_sb_kernel, seq=seq, qt=qt),
        grid=(b, pairs),
        in_specs=[spec(3 * pairs), spec(4 * pairs), spec(5 * pairs)],
        out_specs=pl.BlockSpec((1, seq, LANES), lambda bi, c: (bi, 0, c)),
        out_shape=jax.ShapeDtypeStruct((b, seq, half_width), F32),
        scratch_shapes=[
            pltpu.VMEM((n_blocks, 2 * BLOCK, LANES), BF16),
            pltpu.VMEM((n_blocks, 2 * BLOCK, LANES), BF16),
            pltpu.VMEM((2 * BLOCK, 2 * BLOCK), BF16),
            pltpu.VMEM((qt, LANES), F32),
            pltpu.VMEM((qt, 2 * BLOCK), F32),
        ],
        compiler_params=_cparams(("arbitrary", "arbitrary")),
        name="stickbreak_attention",
    )(proj3, proj3, proj3)


def _split_bf16(x):
    hi = x.astype(BF16)
    return hi, (x - hi.astype(F32)).astype(BF16)


def _outproj_kernel(od_ref, os_ref, x_ref, w_ref, gd_ref, gs_ref, gf_ref, wr_ref, br_ref,
                    x2_ref, h2_ref, idx_ref, gate_ref, acc_d, acc_s, ssq_d, ssq_s, *, nk, half_width):
    k = pl.program_id(1)

    def accumulate(a_ref, g_ref, acc, ssq, first):
        a = a_ref[...]
        sq = jnp.sum(a * a, axis=-1, keepdims=True)
        prod = jnp.dot((a * g_ref[...]).astype(BF16), w_ref[...].astype(BF16), preferred_element_type=F32)
        if first:
            ssq[...] = sq
            acc[...] = prod
        else:
            ssq[...] += sq
            acc[...] += prod

    for kk in range(nk):
        @pl.when(k == kk)
        def _(kk=kk):
            if kk < nk // 2:
                accumulate(od_ref, gd_ref, acc_d, ssq_d, kk == 0)
            else:
                accumulate(os_ref, gs_ref, acc_s, ssq_s, kk == nk // 2)

    @pl.when(k == nk - 1)
    def _():
        r_d = lax.rsqrt(ssq_d[...] / half_width + NORM_EPS)
        r_s = lax.rsqrt(ssq_s[...] / half_width + NORM_EPS)
        x2 = x_ref[...] + (acc_d[...] * r_d + acc_s[...] * r_s)
        x2_ref[...] = x2
        h2 = (x2 * lax.rsqrt(jnp.mean(x2 * x2, axis=-1, keepdims=True) + NORM_EPS)) * gf_ref[...]
        packed = _pack_pairs(h2)
        n_chunk = packed.shape[1] // LANES
        for ch in range(n_chunk):
            h2_ref[pl.ds(ch, packed.shape[0], stride=n_chunk), :] = packed[:, ch * LANES:(ch + 1) * LANES]
        h_hi, h_lo = _split_bf16(h2)
        w_hi, w_lo = _split_bf16(wr_ref[...])
        hw = jnp.dot(h_hi, jnp.concatenate([w_hi, w_lo], axis=1), preferred_element_type=F32)
        logits = (hw[:, :N_EXPERTS] + hw[:, N_EXPERTS:]
                  + jnp.dot(h_lo, w_hi, preferred_element_type=F32)) + br_ref[...]
        tm = logits.shape[0]
        e_iota = lax.broadcasted_iota(jnp.int32, logits.shape, 1)
        out_lane = lax.broadcasted_iota(jnp.int32, (tm, LANES), 1)
        idx_out = jnp.zeros((tm, LANES), jnp.int32)
        val_out = jnp.full((tm, LANES), -jnp.inf, F32)
        top0 = None
        for kk in range(TOP_K):
            m = jnp.max(logits, axis=-1, keepdims=True)
            sel = jnp.min(jnp.where(logits == m, e_iota, N_EXPERTS), axis=-1, keepdims=True)
            idx_out = jnp.where(out_lane == kk, sel, idx_out)
            val_out = jnp.where(out_lane == kk, m, val_out)
            logits = jnp.where(e_iota == sel, -jnp.inf, logits)
            if kk == 0:
                top0 = m
        ex = jnp.exp(val_out - top0)
        gate_ref[...] = ex / jnp.sum(ex, axis=-1, keepdims=True)
        idx_ref[...] = idx_out


def _outproj_router(o_d, o_s, x2d, w_out, g_d, g_s, g_f, w_router, b_router, *, tm, tk):
    m, d = x2d.shape
    half_width = d // 2
    nk = d // tk
    assert half_width % tk == 0 and m % tm == 0
    kh = nk // 2
    return pl.pallas_call(
        functools.partial(_outproj_kernel, nk=nk, half_width=half_width),
        grid=(m // tm, nk),
        in_specs=[
            pl.BlockSpec((tm, tk), lambda i, k: (i, jnp.minimum(k, kh - 1))),
            pl.BlockSpec((tm, tk), lambda i, k: (i, jnp.maximum(k - kh, 0))),
            pl.BlockSpec((tm, d), lambda i, k: (i, 0)),
            pl.BlockSpec((tk, d), lambda i, k: (k, 0)),
            pl.BlockSpec((1, tk), lambda i, k: (0, jnp.minimum(k, kh - 1))),
            pl.BlockSpec((1, tk), lambda i, k: (0, jnp.maximum(k - kh, 0))),
            pl.BlockSpec((1, d), lambda i, k: (0, 0)),
            pl.BlockSpec((d, N_EXPERTS), lambda i, k: (0, 0)),
            pl.BlockSpec((1, N_EXPERTS), lambda i, k: (0, 0)),
        ],
        out_specs=[
            pl.BlockSpec((tm, d), lambda i, k: (i, 0)),
            pl.BlockSpec((tm * (d // 2 // LANES), LANES), lambda i, k: (i, 0)),
            pl.BlockSpec((tm, LANES), lambda i, k: (i, 0)),
            pl.BlockSpec((tm, LANES), lambda i, k: (i, 0)),
        ],
        out_shape=[
            jax.ShapeDtypeStruct((m, d), F32),
            jax.ShapeDtypeStruct((m * (d // 2 // LANES), LANES), jnp.uint32),
            jax.ShapeDtypeStruct((m, LANES), jnp.int32),
            jax.ShapeDtypeStruct((m, LANES), F32),
        ],
        scratch_shapes=[
            pltpu.VMEM((tm, d), F32),
            pltpu.VMEM((tm, d), F32),
            pltpu.VMEM((tm, 1), F32),
            pltpu.VMEM((tm, 1), F32),
        ],
        compiler_params=_cparams(("arbitrary", "arbitrary")),
        name="outproj_router",
    )(o_d, o_s, x2d, w_out, g_d, g_s, g_f, w_router, b_router)


def _unpack_pairs(words):
    hi = lax.bitcast_convert_type(words & jnp.uint32(0xFFFF0000), F32)
    lo = lax.bitcast_convert_type(words << 16, F32)
    return hi.astype(BF16), lo.astype(BF16)


def _pack_pairs(x):
    c = x.shape[1] // 2
    as_bits = lambda v: lax.bitcast_convert_type(v.astype(BF16).astype(F32), jnp.uint32)
    return as_bits(x[:, :c]) | (as_bits(x[:, c:]) >> 16)


def _dispatch_kernel(row_tok, h_hbm, o_ref, h_vmem, sem, *, rows_blk, n_chunk):
    i = pl.program_id(0)

    @pl.when(i == 0)
    def _():
        cp = pltpu.make_async_copy(h_hbm, h_vmem, sem)
        cp.start()
        cp.wait()

    base = i * rows_blk

    def body(r, c):
        tok = row_tok[base + r]
        src = pl.ds(pl.multiple_of(tok * n_chunk, n_chunk), n_chunk)
        o_ref[pl.ds(pl.multiple_of(r * n_chunk, n_chunk), n_chunk), :] = h_vmem[src, :]
        return c

    lax.fori_loop(0, rows_blk, body, 0, unroll=8)


def _dispatch(hp, row_tok, *, rows_blk, n_chunk):
    n_rows = row_tok.shape[0]
    assert n_rows % rows_blk == 0
    grid_spec = pltpu.PrefetchScalarGridSpec(
        num_scalar_prefetch=1,
        grid=(n_rows // rows_blk,),
        in_specs=[pl.BlockSpec(memory_space=pl.ANY)],
        out_specs=pl.BlockSpec((rows_blk * n_chunk, LANES), lambda i, row_tok: (i, 0)),
        scratch_shapes=[pltpu.VMEM(hp.shape, hp.dtype), pltpu.SemaphoreType.DMA],
    )
    return pl.pallas_call(
        functools.partial(_dispatch_kernel, rows_blk=rows_blk, n_chunk=n_chunk),
        grid_spec=grid_spec,
        out_shape=jax.ShapeDtypeStruct((n_rows * n_chunk, LANES), hp.dtype),
        compiler_params=_cparams(("arbitrary",)),
        name="dispatch",
    )(row_tok, hp)


def _expert_kernel(item_e, item_r0, item_nb, n_items,
                   xs_hbm, wg_ref, bg_ref, wu_ref, bu_ref, wd_ref, bd_ref,
                   ys_hbm, stage, xb16, acc, pending, sem_in, sem_out, *, nj):
    i = pl.program_id(0)
    j = pl.program_id(1)
    n_live = n_items[0]
    active = i < n_live
    r0 = item_r0[i]
    n_align = item_nb[i]
    n_rows = n_align * ROW_ALIGN
    per_trip = ROW_BLOCK // ROW_ALIGN
    n_main = n_align // per_trip
    nxt = jnp.minimum(i + 1, pl.num_programs(0) - 1)
    r0_next = item_r0[nxt]
    blocks_next = jnp.where(i + 1 < n_live, item_nb[nxt], 0)
    d = acc.shape[1]
    n_chunk = d // 2 // LANES
    blk_words = ROW_ALIGN * n_chunk

    def in_copy(base_row, blk):
        src = xs_hbm.at[pl.ds(pl.multiple_of((base_row + blk * ROW_ALIGN) * n_chunk, blk_words), blk_words)]
        return pltpu.make_async_copy(src, stage.at[pl.ds(pl.multiple_of(blk * blk_words, blk_words), blk_words)],
                                     sem_in)

    def start_in(base_row, n_blk):
        def issue(blk, c):
            in_copy(base_row, blk).start()
            return c

        lax.fori_loop(0, n_blk, issue, 0)

    def out_copy(blk):
        src = acc.at[pl.ds(pl.multiple_of(blk * ROW_ALIGN, ROW_ALIGN), ROW_ALIGN)]
        dst = ys_hbm.at[pl.ds(pl.multiple_of(r0 + blk * ROW_ALIGN, ROW_ALIGN), ROW_ALIGN)]
        return pltpu.make_async_copy(src, dst, sem_out)

    def drain():
        @pl.when(pending[0] > 0)
        def _():
            rows = pl.ds(0, pending[0] * ROW_ALIGN)
            pltpu.make_async_copy(acc.at[rows], ys_hbm.at[rows], sem_out).wait()
            pending[0] = 0

    @pl.when((i == 0) & (j == 0))
    def _():
        pending[0] = 0
        start_in(r0, n_align)

    @pl.when(active & (j == 0))
    def _():
        all_rows = pl.ds(0, n_rows * n_chunk)
        pltpu.make_async_copy(xs_hbm.at[all_rows], stage.at[all_rows], sem_in).wait()
        drain()

        def unpack(rb, c):
            rows = pl.ds(pl.multiple_of(rb * ROW_ALIGN, ROW_ALIGN), ROW_ALIGN)
            for ch in range(n_chunk):
                words = stage[pl.ds(rb * (ROW_ALIGN * n_chunk) + ch, ROW_ALIGN, stride=n_chunk), :]
                hi, lo = _unpack_pairs(words)
                xb16[rows, ch * LANES:(ch + 1) * LANES] = hi
                xb16[rows, d // 2 + ch * LANES:d // 2 + (ch + 1) * LANES] = lo
            acc[rows, :] = jnp.broadcast_to(bd_ref[0], (ROW_ALIGN, d))
            return c

        lax.fori_loop(0, n_align, unpack, 0)
        start_in(r0_next, blocks_next)

    def trip(row0, n_blk):
        rows = pl.ds(row0, n_blk * ROW_ALIGN)
        xb = xb16[rows, :]
        g = jnp.dot(xb, wg_ref[0].astype(BF16), preferred_element_type=F32) + bg_ref[0]
        u = jnp.dot(xb, wu_ref[0].astype(BF16), preferred_element_type=F32) + bu_ref[0]
        g = jnp.minimum(g, SWIGLU_LIMIT)
        u = jnp.clip(u, -SWIGLU_LIMIT, SWIGLU_LIMIT)
        a = (u + 1.0) * (g * jax.nn.sigmoid(SWIGLU_ALPHA * g))
        acc[rows, :] += jnp.dot(a.astype(BF16), wd_ref[0].astype(BF16), preferred_element_type=F32)

        @pl.when(j == nj - 1)
        def _():
            for s in range(n_blk):
                out_copy(row0 // ROW_ALIGN + s).start()
            pending[0] = pending[0] + n_blk

    @pl.when(active)
    def _():
        def main_trip(rb, c):
            trip(pl.multiple_of(rb * ROW_BLOCK, ROW_BLOCK), per_trip)
            return c

        lax.fori_loop(0, n_main, main_trip, 0)

        def tail_trip(t, c):
            trip(pl.multiple_of((n_main * per_trip + t) * ROW_ALIGN, ROW_ALIGN), 1)
            return c

        lax.fori_loop(0, n_align - n_main * per_trip, tail_trip, 0)

    @pl.when((i == pl.num_programs(0) - 1) & (j == nj - 1))
    def _():
        drain()
        acc[0:ROW_ALIGN, :] = jnp.zeros((ROW_ALIGN, d), F32)
        used = n_items[1]

        def tail_copy(t):
            dst = ys_hbm.at[pl.ds(pl.multiple_of((used + t) * ROW_ALIGN, ROW_ALIGN), ROW_ALIGN)]
            return pltpu.make_async_copy(acc.at[pl.ds(0, ROW_ALIGN)], dst, sem_out)

        def issue(t, c):
            tail_copy(t).start()
            return c

        n_tail = ys_hbm.shape[0] // ROW_ALIGN - used
        lax.fori_loop(0, n_tail, issue, 0)

        def wait(t, c):
            tail_copy(t).wait()
            return c

        lax.fori_loop(0, n_tail, wait, 0)


def _experts(xs, w_gate, b_gate, w_up, b_up, w_down, b_down, meta, *, n_rows, th):
    item_e, item_r0, item_nb, n_items = meta
    n_exp, d, dh = w_gate.shape
    nj = dh // th
    n_item_slots = item_e.shape[0]

    def w_col(i, j, item_e, item_r0, item_nb, n_items):
        return (item_e[i], 0, jnp.where(i < n_items[0], j, nj - 1))

    def w_row(i, j, item_e, item_r0, item_nb, n_items):
        return (item_e[i], jnp.where(i < n_items[0], j, nj - 1), 0)

    def b_all(i, j, item_e, item_r0, item_nb, n_items):
        return (item_e[i], 0, 0)

    grid_spec = pltpu.PrefetchScalarGridSpec(
        num_scalar_prefetch=4,
        grid=(n_item_slots, nj),
        in_specs=[
            pl.BlockSpec(memory_space=pl.ANY),
            pl.BlockSpec((1, d, th), w_col),
            pl.BlockSpec((1, 1, th), w_col),
            pl.BlockSpec((1, d, th), w_col),
            pl.BlockSpec((1, 1, th), w_col),
            pl.BlockSpec((1, th, d), w_row),
            pl.BlockSpec((1, 1, d), b_all),
        ],
        out_specs=pl.BlockSpec(memory_space=pl.ANY),
        scratch_shapes=[
            pltpu.VMEM((ITEM_ROWS * (d // 2 // LANES), LANES), jnp.uint32),
            pltpu.VMEM((ITEM_ROWS, d), BF16),
            pltpu.VMEM((ITEM_ROWS, d), F32),
            pltpu.SMEM((1,), jnp.int32),
            pltpu.SemaphoreType.DMA,
            pltpu.SemaphoreType.DMA,
        ],
    )
    return pl.pallas_call(
        functools.partial(_expert_kernel, nj=nj),
        grid_spec=grid_spec,
        out_shape=jax.ShapeDtypeStruct((n_rows, d), F32),
        compiler_params=_cparams(("arbitrary", "arbitrary")),
        name="experts",
    )(item_e, item_r0, item_nb, n_items,
      xs, w_gate, b_gate.reshape(n_exp, 1, dh), w_up, b_up.reshape(n_exp, 1, dh), w_down,
      b_down.reshape(n_exp, 1, d))


def _routing_metadata(top_idx, n_tok):
    n_assign = n_tok * TOP_K
    flat_e = top_idx.reshape(n_assign)
    onehot = (flat_e[:, None] == jnp.arange(N_EXPERTS, dtype=jnp.int32)[None, :]).astype(jnp.int32)
    rank = jnp.take_along_axis(jnp.cumsum(onehot, axis=0), flat_e[:, None], axis=1)[:, 0] - 1
    counts = jnp.sum(onehot, axis=0)
    padded = (counts + ROW_ALIGN - 1) // ROW_ALIGN * ROW_ALIGN
    pend = jnp.cumsum(padded)
    pstart = pend - padded
    dest = (pstart[flat_e] + rank).astype(jnp.int32)
    n_rows = (n_assign // ROW_ALIGN + N_EXPERTS) * ROW_ALIGN
    flat_tok = jnp.arange(n_assign, dtype=jnp.int32) // TOP_K
    row_tok = jnp.zeros((n_rows,), jnp.int32).at[dest].set(
        flat_tok, unique_indices=True, mode="promise_in_bounds")
    n_chunks = (padded + ITEM_ROWS - 1) // ITEM_ROWS
    cend = jnp.cumsum(n_chunks)
    cstart = cend - n_chunks
    n_slots = N_EXPERTS + n_assign // ITEM_ROWS + 1
    slot = jnp.arange(n_slots, dtype=jnp.int32)
    n_items = cend[-1]
    last_e = jnp.max(jnp.where(n_chunks > 0, jnp.arange(N_EXPERTS), 0))
    e_of = jnp.minimum(jnp.searchsorted(cend, slot, side='right'), N_EXPERTS - 1)
    live = slot < n_items
    item_e = jnp.where(live, e_of, last_e).astype(jnp.int32)
    chunk = slot - cstart[e_of]
    item_r0 = jnp.where(live, pstart[e_of] + chunk * ITEM_ROWS, 0).astype(jnp.int32)
    item_rows = jnp.clip(padded[e_of] - chunk * ITEM_ROWS, 0, ITEM_ROWS)
    item_nb = jnp.where(live, item_rows // ROW_ALIGN, 0).astype(jnp.int32)
    counts_blk = jnp.stack([n_items, pend[-1] // ROW_ALIGN]).astype(jnp.int32)
    meta = (item_e, item_r0, item_nb, counts_blk)
    return meta, row_tok, dest, n_rows


def _combine_kernel(dest, ys_hbm, x2_ref, gate_ref, g_ref, o_ref, buf, sem, *, tm):
    i = pl.program_id(0)
    n = pl.num_programs(0)

    def row_copy(tile, slot, t, kk):
        r = dest[(tile * tm + t) * TOP_K + kk]
        return pltpu.make_async_copy(ys_hbm.at[pl.ds(r, 1)], buf.at[slot, kk, pl.ds(t, 1)], sem.at[slot])

    def issue_tile(tile, slot):
        def issue(t, c):
            for kk in range(TOP_K):
                row_copy(tile, slot, t, kk).start()
            return c

        lax.fori_loop(0, tm, issue, 0, unroll=8)

    slot = i % 2

    @pl.when(i == 0)
    def _():
        issue_tile(0, 0)

    @pl.when(i + 1 < n)
    def _():
        issue_tile(i + 1, 1 - slot)

    for kk in range(TOP_K):
        pltpu.make_async_copy(ys_hbm.at[pl.ds(0, tm)], buf.at[slot, kk], sem.at[slot]).wait()

    gates = gate_ref[...]
    y = x2_ref[...]
    for kk in range(TOP_K):
        y = y + gates[:, kk:kk + 1] * buf[slot, kk]
    o_ref[...] = (y * lax.rsqrt(jnp.mean(y * y, axis=-1, keepdims=True) + NORM_EPS)) * g_ref[...]


def _combine(dest, ys, x2, gates, g_final, *, tm):
    m, d = x2.shape
    grid_spec = pltpu.PrefetchScalarGridSpec(
        num_scalar_prefetch=1,
        grid=(m // tm,),
        in_specs=[
            pl.BlockSpec(memory_space=pl.ANY),
            pl.BlockSpec((tm, d), lambda i, dest: (i, 0)),
            pl.BlockSpec((tm, LANES), lambda i, dest: (i, 0)),
            pl.BlockSpec((1, d), lambda i, dest: (0, 0)),
        ],
        out_specs=pl.BlockSpec((tm, d), lambda i, dest: (i, 0)),
        scratch_shapes=[pltpu.VMEM((2, TOP_K, tm, d), F32), pltpu.SemaphoreType.DMA((2,))],
    )
    return pl.pallas_call(
        functools.partial(_combine_kernel, tm=tm),
        grid_spec=grid_spec,
        out_shape=jax.ShapeDtypeStruct((m, d), F32),
        compiler_params=_cparams(("arbitrary",)),
        name="combine",
    )(dest, ys, x2, gates, g_final)


def _layer(x, attn_norm, w_in, dil_out_norm, sb_out_norm, w_out, ffn_norm,
           w_router, b_router, w_gate, b_gate, w_up, b_up, w_down, b_down, *, tiles):
    b, seq, d = x.shape
    m = b * seq
    half_width = d // 2
    x2d = x.reshape(m, d)
    cos_t, sin_t = _rope_tables(seq)
    proj = _inproj(x2d, attn_norm.reshape(1, d), w_in, cos_t, sin_t,
                   seq=seq, tm=tiles["in_tm"], tn=tiles["in_tn"])
    proj3 = proj.reshape(b, seq, 3 * d)
    o_d = _dilated_attention(proj3, half_width=half_width).reshape(m, half_width)
    o_s = _stickbreak_attention(proj3, half_width=half_width, qt=tiles["sb_qt"]).reshape(m, half_width)
    x2, h2, idx_l, gate_l = _outproj_router(
        o_d, o_s, x2d, w_out, dil_out_norm.reshape(1, half_width), sb_out_norm.reshape(1, half_width),
        ffn_norm.reshape(1, d), w_router, b_router.reshape(1, N_EXPERTS),
        tm=tiles["out_tm"], tk=tiles["out_tk"])
    meta, row_tok, dest, n_rows = _routing_metadata(idx_l[:, :TOP_K], m)
    xs = _dispatch(h2, row_tok, rows_blk=tiles["disp_rows"], n_chunk=d // 2 // LANES)
    ys = _experts(xs, w_gate, b_gate, w_up, b_up, w_down, b_down, meta, n_rows=n_rows, th=tiles["moe_th"])
    return x2, ys, dest, gate_l


_TILES = dict(in_tm=1024, in_tn=512, sb_qt=512, out_tm=512, out_tk=512, disp_rows=1024, moe_th=512, cmb_tm=128)


def kernel(x, attn_norm, w_in, dil_out_norm, sb_out_norm, w_out, ffn_norm, w_router, b_router,
           w_gate, b_gate, w_up, b_up, w_down, b_down, final_norm):
    depth = attn_norm.shape[0]
    b, seq, d = x.shape
    for l in range(depth):
        x2, ys, dest, gate_l = _layer(
            x, attn_norm[l], w_in[l], dil_out_norm[l], sb_out_norm[l], w_out[l], ffn_norm[l],
            w_router[l], b_router[l], w_gate[l], b_gate[l], w_up[l], b_up[l], w_down[l], b_down[l],
            tiles=_TILES)
        assert depth == 1
        x = _combine(dest, ys, x2, gate_l, final_norm.reshape(1, d), tm=_TILES["cmb_tm"])
    return x.reshape(b, seq, d)
```

```python
import functools
import math

import jax
import jax.numpy as jnp
from jax import lax
from jax.experimental import pallas as pl
from jax.experimental.pallas import tpu as pltpu

F32 = jnp.float32
BF16 = jnp.bfloat16

HEAD_DIM = 64
LANES = 128
BLOCK = 128
DIL_PATTERNS = ((128, 1), (512, 4), (2048, 16))
ROPE_THETA = 10000.0
N_EXPERTS = 32
TOP_K = 4
SWIGLU_LIMIT = 7.0
SWIGLU_ALPHA = 1.702
NORM_EPS = 1e-5
Q_SCALE = HEAD_DIM ** -0.5 * math.log2(math.e)
SB_UNROLL = 4
DIL_UNROLL = 8
ROW_ALIGN = 128
ROW_BLOCK = 512
ITEM_ROWS = 1280
VMEM_LIMIT = 56 * 1024 * 1024


def _cparams(sem):
    return pltpu.CompilerParams(dimension_semantics=sem, vmem_limit_bytes=VMEM_LIMIT)


def _inproj_kernel(x_ref, g_ref, w_ref, cos_ref, sin_ref, o_ref, h_scr, *, tn, half_width):
    j = pl.program_id(1)

    @pl.when(j == 0)
    def _():
        x = x_ref[...]
        y = x * lax.rsqrt(jnp.mean(x * x, axis=-1, keepdims=True) + NORM_EPS)
        h_scr[...] = (y * g_ref[...]).astype(BF16)

    acc = jnp.dot(h_scr[...], w_ref[...].astype(BF16), preferred_element_type=F32)
    section = (j * tn) // half_width
    scale = jnp.where((section == 0) | (section == 3), Q_SCALE, 1.0).astype(F32)

    @pl.when(section < 2)
    def _():
        lane = lax.broadcasted_iota(jnp.int32, (acc.shape[0], LANES), 1)
        first_half = (lane % HEAD_DIM) < (HEAD_DIM // 2)
        c = cos_ref[...]
        s = sin_ref[...]
        for g in range(tn // LANES):
            xg = acc[:, g * LANES:(g + 1) * LANES]
            partner = jnp.where(first_half,
                                pltpu.roll(xg, LANES - HEAD_DIM // 2, 1),
                                pltpu.roll(xg, HEAD_DIM // 2, 1))
            o_ref[:, g * LANES:(g + 1) * LANES] = ((xg * c + partner * s) * scale).astype(BF16)

    @pl.when(section >= 2)
    def _():
        o_ref[...] = (acc * scale).astype(BF16)


def _inproj(x2d, g, w_in, cos_t, sin_t, *, seq, tm, tn):
    m, d = x2d.shape
    n = w_in.shape[1]
    half_width = d // 2
    assert half_width % tn == 0 and tn % LANES == 0 and m % tm == 0 and seq % tm == 0
    pos_tiles = seq // tm
    return pl.pallas_call(
        functools.partial(_inproj_kernel, tn=tn, half_width=half_width),
        grid=(m // tm, n // tn),
        in_specs=[
            pl.BlockSpec((tm, d), lambda i, j: (i, 0)),
            pl.BlockSpec((1, d), lambda i, j: (0, 0)),
            pl.BlockSpec((d, tn), lambda i, j: (0, j)),
            pl.BlockSpec((tm, LANES), lambda i, j: (i % pos_tiles, 0)),
            pl.BlockSpec((tm, LANES), lambda i, j: (i % pos_tiles, 0)),
        ],
        out_specs=pl.BlockSpec((tm, tn), lambda i, j: (i, j)),
        out_shape=jax.ShapeDtypeStruct((m, n), BF16),
        scratch_shapes=[pltpu.VMEM((tm, d), BF16)],
        compiler_params=_cparams(("arbitrary", "arbitrary")),
        name="inproj",
    )(x2d, g, w_in, cos_t, sin_t)


def _rope_tables(seq):
    half = HEAD_DIM // 2
    inv_freq = 1.0 / (ROPE_THETA ** (jnp.arange(half, dtype=F32) * 2.0 / HEAD_DIM))
    ang = jnp.arange(seq, dtype=F32)[:, None] * inv_freq[None, :]
    cos = jnp.cos(ang)
    sin = jnp.sin(ang)
    reps = LANES // HEAD_DIM
    cos_t = jnp.tile(jnp.concatenate([cos, cos], axis=1), (1, reps))
    sin_t = jnp.tile(jnp.concatenate([-sin, sin], axis=1), (1, reps))
    return cos_t, sin_t


def _dil_kernel(q_ref, k_ref, v_ref, o_ref, f32_scr, qp, kp, vp, operm, lperm, onat, lnat, *, seq):
    n_blocks = seq // BLOCK
    lane_k = lax.broadcasted_iota(jnp.int32, (2 * BLOCK, LANES), 1)
    head_a_k = lane_k < HEAD_DIM
    lane_q = lax.broadcasted_iota(jnp.int32, (BLOCK, LANES), 1)
    head_a_q = lane_q < HEAD_DIM
    qi = lax.broadcasted_iota(jnp.int32, (BLOCK, 2 * BLOCK), 0)
    ki = lax.broadcasted_iota(jnp.int32, (BLOCK, 2 * BLOCK), 1)
    dist = qi + BLOCK - ki
    band = (dist >= 0) & (dist <= BLOCK)
    cur_only = ki >= BLOCK
    zero_b = jnp.zeros((), BF16)
    head_ones = jnp.concatenate([jnp.where(head_a_k, 1.0, 0.0), jnp.where(head_a_k, 0.0, 1.0)], axis=0).astype(BF16)

    f32_scr[0] = q_ref[0].astype(F32)
    f32_scr[1] = k_ref[0].astype(F32)
    f32_scr[2] = v_ref[0].astype(F32)
    kp[0:BLOCK, :] = jnp.zeros((BLOCK, LANES), BF16)
    vp[0:BLOCK, :] = jnp.zeros((BLOCK, LANES), BF16)

    for p, (window, dil) in enumerate(DIL_PATTERNS):
        assert window // dil == BLOCK
        length = seq // dil
        per_stream = length // BLOCK
        assert length % BLOCK == 0
        if dil == 1:
            qp[...] = q_ref[0]
            kp[BLOCK:, :] = k_ref[0]
            vp[BLOCK:, :] = v_ref[0]
        else:
            for r in range(dil):
                rows = pl.ds(r, length, stride=dil)
                qp[r * length:(r + 1) * length, :] = f32_scr[0, rows, :].astype(BF16)
                kp[BLOCK + r * length:BLOCK + (r + 1) * length, :] = f32_scr[1, rows, :].astype(BF16)
                vp[BLOCK + r * length:BLOCK + (r + 1) * length, :] = f32_scr[2, rows, :].astype(BF16)
        o_dst = onat.at[p] if dil == 1 else operm
        l_dst = lnat.at[p] if dil == 1 else lperm

        def block_body(bi, carry, per_stream=per_stream, o_dst=o_dst, l_dst=l_dst):
            row0 = pl.multiple_of(bi * BLOCK, BLOCK)
            qb = qp[pl.ds(row0, BLOCK), :]
            kw = kp[pl.ds(row0, 2 * BLOCK), :]
            vw = vp[pl.ds(row0, 2 * BLOCK), :]
            has_prev = (bi % per_stream) != 0
            valid = band & (cur_only | has_prev)
            kk = jnp.concatenate([jnp.where(head_a_k, kw, zero_b), jnp.where(head_a_k, zero_b, kw)], axis=0)
            s2 = lax.dot_general(qb, kk, (((1,), (1,)), ((), ())), preferred_element_type=F32)
            ps, ms = [], []
            for h in range(2):
                s = jnp.where(valid, s2[:, h * 2 * BLOCK:(h + 1) * 2 * BLOCK], -jnp.inf)
                m = jnp.max(s, axis=-1, keepdims=True)
                ps.append(jnp.exp2(s - m).astype(BF16))
                ms.append(m)
            vv = jnp.concatenate([jnp.where(head_a_k, vw, zero_b), jnp.where(head_a_k, zero_b, vw)], axis=0)
            o_den = jnp.dot(jnp.concatenate(ps, axis=1), jnp.concatenate([vv, head_ones], axis=1),
                            preferred_element_type=F32)
            o = o_den[:, :LANES]
            den = o_den[:, LANES:]
            o_dst[pl.ds(row0, BLOCK), :] = o / den
            l_dst[pl.ds(row0, BLOCK), :] = jnp.where(head_a_q, ms[0], ms[1]) + jnp.log2(den)
            return carry

        lax.fori_loop(0, n_blocks, block_body, 0, unroll=DIL_UNROLL)
        if dil != 1:
            for r in range(dil):
                rows = pl.ds(r, length, stride=dil)
                onat[p, rows, :] = operm[r * length:(r + 1) * length, :]
                lnat[p, rows, :] = lperm[r * length:(r + 1) * length, :]

    def mix_body(c, carry):
        rows = pl.ds(pl.multiple_of(c * BLOCK, BLOCK), BLOCK)
        ls = [lnat[p, rows, :] for p in range(len(DIL_PATTERNS))]
        m = functools.reduce(jnp.maximum, ls)
        es = [jnp.exp2(l - m) for l in ls]
        num = functools.reduce(lambda a, b: a + b, [e * onat[p, rows, :] for p, e in enumerate(es)])
        o_ref[0, rows, :] = num / functools.reduce(lambda a, b: a + b, es)
        return carry

    lax.fori_loop(0, n_blocks, mix_body, 0)


def _dilated_attention(proj3, *, half_width):
    b, seq, _ = proj3.shape
    pairs = half_width // LANES
    n_pat = len(DIL_PATTERNS)
    spec = lambda off: pl.BlockSpec((1, seq, LANES), lambda bi, c, off=off: (bi, 0, off + c))
    return pl.pallas_call(
        functools.partial(_dil_kernel, seq=seq),
        grid=(b, pairs),
        in_specs=[spec(0), spec(pairs), spec(2 * pairs)],
        out_specs=pl.BlockSpec((1, seq, LANES), lambda bi, c: (bi, 0, c)),
        out_shape=jax.ShapeDtypeStruct((b, seq, half_width), F32),
        scratch_shapes=[
            pltpu.VMEM((3, seq, LANES), F32),
            pltpu.VMEM((seq, LANES), BF16),
            pltpu.VMEM((seq + BLOCK, LANES), BF16),
            pltpu.VMEM((seq + BLOCK, LANES), BF16),
            pltpu.VMEM((seq, LANES), F32),
            pltpu.VMEM((seq, LANES), F32),
            pltpu.VMEM((n_pat, seq, LANES), F32),
            pltpu.VMEM((n_pat, seq, LANES), F32),
        ],
        compiler_params=_cparams(("arbitrary", "arbitrary")),
        name="dilated_attention",
    )(proj3, proj3, proj3)


def _sb_kernel(q_ref, k_ref, v_ref, o_ref, kk_scr, vv_scr, tri_scr, acc_scr, carry_scr, *, seq, qt):
    n_blocks = seq // BLOCK
    per_tile = qt // BLOCK
    lane_k = lax.broadcasted_iota(jnp.int32, (BLOCK, LANES), 1)
    head_a = lane_k < HEAD_DIM
    zero_b = jnp.zeros((), BF16)

    def prep(jb, carry):
        rows = pl.ds(pl.multiple_of(jb * BLOCK, BLOCK), BLOCK)
        kb = k_ref[0, rows, :]
        vb = v_ref[0, rows, :]
        kk_scr[jb, 0:BLOCK, :] = jnp.where(head_a, kb, zero_b)
        kk_scr[jb, BLOCK:, :] = jnp.where(head_a, zero_b, kb)
        vv_scr[jb, 0:BLOCK, :] = jnp.where(head_a, vb, zero_b)
        vv_scr[jb, BLOCK:, :] = jnp.where(head_a, zero_b, vb)
        return carry

    lax.fori_loop(0, n_blocks, prep, 0)

    jj = lax.broadcasted_iota(jnp.int32, (2 * BLOCK, 2 * BLOCK), 0) % BLOCK
    ss = lax.broadcasted_iota(jnp.int32, (2 * BLOCK, 2 * BLOCK), 1)
    tri_scr[...] = jnp.where((ss >= BLOCK) | (jj > ss), 1.0, 0.0).astype(BF16)

    def key_block(qb, jb, key_off):
        rs = slice(0 if key_off is None else key_off, qt)
        z = lax.dot_general(qb[rs], kk_scr[jb], (((1,), (1,)), ((), ())), preferred_element_type=F32)
        neg_abs = lax.bitcast_convert_type(lax.bitcast_convert_type(z, jnp.uint32) | jnp.uint32(0x80000000), F32)
        ls = jnp.minimum(z, 0.0) - jnp.log2(1.0 + jnp.exp2(neg_abs))
        lr = ls - z
        if key_off is not None:
            n = qt - key_off
            causal = (lax.broadcasted_iota(jnp.int32, (n, 2 * BLOCK), 1) % BLOCK
                      < lax.broadcasted_iota(jnp.int32, (n, 2 * BLOCK), 0))
            lr = jnp.where(causal, lr, 0.0)
        hi = lr.astype(BF16)
        lo = (lr - hi.astype(F32)).astype(BF16)
        tri = tri_scr[...]
        a_heads = []
        for h in range(2):
            sl = slice(h * BLOCK, (h + 1) * BLOCK)
            lhs = jnp.concatenate([hi[:, sl], lo[:, sl]], axis=1)
            cs_tot = jnp.dot(lhs, tri, preferred_element_type=F32)
            carry = carry_scr[rs, sl]
            a = jnp.exp2(ls[:, sl] + cs_tot[:, :BLOCK] + carry)
            if key_off is not None:
                a = jnp.where(causal[:, :BLOCK], a, 0.0)
            carry_scr[rs, sl] = carry + cs_tot[:, BLOCK:]
            a_heads.append(a.astype(BF16))
        acc_scr[rs, :] += jnp.dot(jnp.concatenate(a_heads, axis=1), vv_scr[jb], preferred_element_type=F32)

    def q_tile(i, carry):
        rows = pl.ds(pl.multiple_of(i * qt, qt), qt)
        qb = q_ref[0, rows, :]
        acc_scr[...] = jnp.zeros_like(acc_scr)
        carry_scr[...] = jnp.zeros_like(carry_scr)
        for t in reversed(range(per_tile)):
            key_block(qb, i * per_tile + t, t * BLOCK)

        def inner(t, c):
            for s in range(SB_UNROLL):
                key_block(qb, i * per_tile - 1 - (t * SB_UNROLL + s), None)
            return c

        lax.fori_loop(0, i * (per_tile // SB_UNROLL), inner, 0)
        o_ref[0, rows, :] = acc_scr[...]
        return carry

    lax.fori_loop(0, seq // qt, q_tile, 0)


def _stickbreak_attention(proj3, *, half_width, qt):
    b, seq, _ = proj3.shape
    pairs = half_width // LANES
    n_blocks = seq // BLOCK
    assert seq % qt == 0 and qt % BLOCK == 0
    spec = lambda off: pl.BlockSpec((1, seq, LANES), lambda bi, c, off=off: (bi, 0, off + c))
    return pl.pallas_call(
        functools.partial(_sb_kernel, seq=seq, qt=qt),
        grid=(b, pairs),
        in_specs=[spec(3 * pairs), spec(4 * pairs), spec(5 * pairs)],
        out_specs=pl.BlockSpec((1, seq, LANES), lambda bi, c: (bi, 0, c)),
        out_shape=jax.ShapeDtypeStruct((b, seq, half_width), F32),
        scratch_shapes=[
            pltpu.VMEM((n_blocks, 2 * BLOCK, LANES), BF16),
            pltpu.VMEM((n_blocks, 2 * BLOCK, LANES), BF16),
            pltpu.VMEM((2 * BLOCK, 2 * BLOCK), BF16),
            pltpu.VMEM((qt, LANES), F32),
            pltpu.VMEM((qt, 2 * BLOCK), F32),
        ],
        compiler_params=_cparams(("arbitrary", "arbitrary")),
        name="stickbreak_attention",
    )(proj3, proj3, proj3)


def _split_bf16(x):
    hi = x.astype(BF16)
    return hi, (x - hi.astype(F32)).astype(BF16)


def _outproj_kernel(od_ref, os_ref, x_ref, w_ref, gd_ref, gs_ref, gf_ref, wr_ref, br_ref,
                    x2_ref, h2_ref, idx_ref, gate_ref, rank_ref, cnt_ref,
                    acc_d, acc_s, ssq_d, ssq_s, cnt_scr, *, nk, half_width):
    k = pl.program_id(1)

    def accumulate(a_ref, g_ref, acc, ssq, first):
        a = a_ref[...]
        sq = jnp.sum(a * a, axis=-1, keepdims=True)
        prod = jnp.dot((a * g_ref[...]).astype(BF16), w_ref[...].astype(BF16), preferred_element_type=F32)
        if first:
            ssq[...] = sq
            acc[...] = prod
        else:
            ssq[...] += sq
            acc[...] += prod

    for kk in range(nk):
        @pl.when(k == kk)
        def _(kk=kk):
            if kk < nk // 2:
                accumulate(od_ref, gd_ref, acc_d, ssq_d, kk == 0)
            else:
                accumulate(os_ref, gs_ref, acc_s, ssq_s, kk == nk // 2)

    @pl.when(k == nk - 1)
    def _():
        r_d = lax.rsqrt(ssq_d[...] / half_width + NORM_EPS)
        r_s = lax.rsqrt(ssq_s[...] / half_width + NORM_EPS)
        x2 = x_ref[...] + (acc_d[...] * r_d + acc_s[...] * r_s)
        x2_ref[...] = x2
        h2 = (x2 * lax.rsqrt(jnp.mean(x2 * x2, axis=-1, keepdims=True) + NORM_EPS)) * gf_ref[...]
        packed = _pack_pairs(h2)
        n_chunk = packed.shape[1] // LANES
        for ch in range(n_chunk):
            h2_ref[pl.ds(ch, packed.shape[0], stride=n_chunk), :] = packed[:, ch * LANES:(ch + 1) * LANES]
        h_hi, h_lo = _split_bf16(h2)
        w_hi, w_lo = _split_bf16(wr_ref[...])
        hw = jnp.dot(h_hi, jnp.concatenate([w_hi, w_lo], axis=1), preferred_element_type=F32)
        logits = (hw[:, :N_EXPERTS] + hw[:, N_EXPERTS:]
                  + jnp.dot(h_lo, w_hi, preferred_element_type=F32)) + br_ref[...]
        tm = logits.shape[0]
        e_iota = lax.broadcasted_iota(jnp.int32, logits.shape, 1)
        out_lane = lax.broadcasted_iota(jnp.int32, (tm, LANES), 1)
        idx_out = jnp.zeros((tm, LANES), jnp.int32)
        val_out = jnp.full((tm, LANES), -jnp.inf, F32)
        top0 = None
        picks = []
        for kk in range(TOP_K):
            m = jnp.max(logits, axis=-1, keepdims=True)
            sel = jnp.min(jnp.where(logits == m, e_iota, N_EXPERTS), axis=-1, keepdims=True)
            idx_out = jnp.where(out_lane == kk, sel, idx_out)
            val_out = jnp.where(out_lane == kk, m, val_out)
            picks.append(e_iota == sel)
            logits = jnp.where(picks[-1], -jnp.inf, logits)
            if kk == 0:
                top0 = m
        ex = jnp.exp(val_out - top0)
        gate_ref[...] = ex / jnp.sum(ex, axis=-1, keepdims=True)
        idx_ref[...] = idx_out

        @pl.when(pl.program_id(0) == 0)
        def _():
            cnt_scr[...] = jnp.zeros_like(cnt_scr)

        chosen = functools.reduce(jnp.logical_or, picks)
        t_row = lax.broadcasted_iota(jnp.int32, (tm, tm), 0)
        t_col = lax.broadcasted_iota(jnp.int32, (tm, tm), 1)
        before = jnp.where(t_col < t_row, 1.0, 0.0).astype(BF16)
        multi_hot = jnp.where(chosen, 1.0, 0.0)
        prior = jnp.dot(before, multi_hot.astype(BF16), preferred_element_type=F32) + cnt_scr[...]
        rank_out = jnp.zeros((tm, LANES), jnp.int32)
        for kk in range(TOP_K):
            r = jnp.sum(jnp.where(picks[kk], prior, 0.0), axis=-1, keepdims=True)
            rank_out = jnp.where(out_lane == kk, r.astype(jnp.int32), rank_out)
        rank_ref[...] = rank_out
        cnt_scr[...] += jnp.sum(multi_hot, axis=0, keepdims=True)
        cnt_ref[...] = cnt_scr[...].astype(jnp.int32)


def _outproj_router(o_d, o_s, x2d, w_out, g_d, g_s, g_f, w_router, b_router, *, tm, tk):
    m, d = x2d.shape
    half_width = d // 2
    nk = d // tk
    assert half_width % tk == 0 and m % tm == 0
    kh = nk // 2
    return pl.pallas_call(
        functools.partial(_outproj_kernel, nk=nk, half_width=half_width),
        grid=(m // tm, nk),
        in_specs=[
            pl.BlockSpec((tm, tk), lambda i, k: (i, jnp.minimum(k, kh - 1))),
            pl.BlockSpec((tm, tk), lambda i, k: (i, jnp.maximum(k - kh, 0))),
            pl.BlockSpec((tm, d), lambda i, k: (i, 0)),
            pl.BlockSpec((tk, d), lambda i, k: (k, 0)),
            pl.BlockSpec((1, tk), lambda i, k: (0, jnp.minimum(k, kh - 1))),
            pl.BlockSpec((1, tk), lambda i, k: (0, jnp.maximum(k - kh, 0))),
            pl.BlockSpec((1, d), lambda i, k: (0, 0)),
            pl.BlockSpec((d, N_EXPERTS), lambda i, k: (0, 0)),
            pl.BlockSpec((1, N_EXPERTS), lambda i, k: (0, 0)),
        ],
        out_specs=[
            pl.BlockSpec((tm, d), lambda i, k: (i, 0)),
            pl.BlockSpec((tm * (d // 2 // LANES), LANES), lambda i, k: (i, 0)),
            pl.BlockSpec((tm, LANES), lambda i, k: (i, 0)),
            pl.BlockSpec((tm, LANES), lambda i, k: (i, 0)),
            pl.BlockSpec((tm, LANES), lambda i, k: (i, 0)),
            pl.BlockSpec((1, N_EXPERTS), lambda i, k: (0, 0)),
        ],
        out_shape=[
            jax.ShapeDtypeStruct((m, d), F32),
            jax.ShapeDtypeStruct((m * (d // 2 // LANES), LANES), jnp.uint32),
            jax.ShapeDtypeStruct((m, LANES), jnp.int32),
            jax.ShapeDtypeStruct((m, LANES), F32),
            jax.ShapeDtypeStruct((m, LANES), jnp.int32),
            jax.ShapeDtypeStruct((1, N_EXPERTS), jnp.int32),
        ],
        scratch_shapes=[
            pltpu.VMEM((tm, d), F32),
            pltpu.VMEM((tm, d), F32),
            pltpu.VMEM((tm, 1), F32),
            pltpu.VMEM((tm, 1), F32),
            pltpu.VMEM((1, N_EXPERTS), F32),
        ],
        compiler_params=_cparams(("arbitrary", "arbitrary")),
        name="outproj_router",
    )(o_d, o_s, x2d, w_out, g_d, g_s, g_f, w_router, b_router)


def _unpack_pairs(words):
    hi = lax.bitcast_convert_type(words & jnp.uint32(0xFFFF0000), F32)
    lo = lax.bitcast_convert_type(words << 16, F32)
    return hi.astype(BF16), lo.astype(BF16)


def _pack_pairs(x):
    c = x.shape[1] // 2
    as_bits = lambda v: lax.bitcast_convert_type(v.astype(BF16).astype(F32), jnp.uint32)
    return as_bits(x[:, :c]) | (as_bits(x[:, c:]) >> 16)


def _dispatch_kernel(row_tok, h_hbm, o_ref, h_vmem, sem, *, rows_blk, n_chunk):
    i = pl.program_id(0)

    @pl.when(i == 0)
    def _():
        cp = pltpu.make_async_copy(h_hbm, h_vmem, sem)
        cp.start()
        cp.wait()

    base = i * rows_blk

    def body(r, c):
        tok = row_tok[base + r]
        src = pl.ds(pl.multiple_of(tok * n_chunk, n_chunk), n_chunk)
        o_ref[pl.ds(pl.multiple_of(r * n_chunk, n_chunk), n_chunk), :] = h_vmem[src, :]
        return c

    lax.fori_loop(0, rows_blk, body, 0, unroll=8)


def _dispatch(hp, row_tok, *, rows_blk, n_chunk):
    n_rows = row_tok.shape[0]
    assert n_rows % rows_blk == 0
    grid_spec = pltpu.PrefetchScalarGridSpec(
        num_scalar_prefetch=1,
        grid=(n_rows // rows_blk,),
        in_specs=[pl.BlockSpec(memory_space=pl.ANY)],
        out_specs=pl.BlockSpec((rows_blk * n_chunk, LANES), lambda i, row_tok: (i, 0)),
        scratch_shapes=[pltpu.VMEM(hp.shape, hp.dtype), pltpu.SemaphoreType.DMA],
    )
    return pl.pallas_call(
        functools.partial(_dispatch_kernel, rows_blk=rows_blk, n_chunk=n_chunk),
        grid_spec=grid_spec,
        out_shape=jax.ShapeDtypeStruct((n_rows * n_chunk, LANES), hp.dtype),
        compiler_params=_cparams(("arbitrary",)),
        name="dispatch",
    )(row_tok, hp)


def _expert_kernel(item_e, item_r0, item_nb, n_items,
                   xs_hbm, wg_ref, bg_ref, wu_ref, bu_ref, wd_ref, bd_ref,
                   ys_hbm, stage, xb16, acc, pending, sem_in, sem_out, *, nj):
    i = pl.program_id(0)
    j = pl.program_id(1)
    n_live = n_items[0]
    active = i < n_live
    r0 = item_r0[i]
    n_align = item_nb[i]
    n_rows = n_align * ROW_ALIGN
    per_trip = ROW_BLOCK // ROW_ALIGN
    n_main = n_align // per_trip
    nxt = jnp.minimum(i + 1, pl.num_programs(0) - 1)
    r0_next = item_r0[nxt]
    blocks_next = jnp.where(i + 1 < n_live, item_nb[nxt], 0)
    d = acc.shape[1]
    n_chunk = d // 2 // LANES
    blk_words = ROW_ALIGN * n_chunk

    def in_copy(base_row, blk):
        src = xs_hbm.at[pl.ds(pl.multiple_of((base_row + blk * ROW_ALIGN) * n_chunk, blk_words), blk_words)]
        return pltpu.make_async_copy(src, stage.at[pl.ds(pl.multiple_of(blk * blk_words, blk_words), blk_words)],
                                     sem_in)

    def start_in(base_row, n_blk):
        def issue(blk, c):
            in_copy(base_row, blk).start()
            return c

        lax.fori_loop(0, n_blk, issue, 0)

    def out_copy(blk):
        src = acc.at[pl.ds(pl.multiple_of(blk * ROW_ALIGN, ROW_ALIGN), ROW_ALIGN)]
        dst = ys_hbm.at[pl.ds(pl.multiple_of(r0 + blk * ROW_ALIGN, ROW_ALIGN), ROW_ALIGN)]
        return pltpu.make_async_copy(src, dst, sem_out)

    def drain():
        @pl.when(pending[0] > 0)
        def _():
            rows = pl.ds(0, pending[0] * ROW_ALIGN)
            pltpu.make_async_copy(acc.at[rows], ys_hbm.at[rows], sem_out).wait()
            pending[0] = 0

    @pl.when((i == 0) & (j == 0))
    def _():
        pending[0] = 0
        start_in(r0, n_align)

    @pl.when(active & (j == 0))
    def _():
        all_rows = pl.ds(0, n_rows * n_chunk)
        pltpu.make_async_copy(xs_hbm.at[all_rows], stage.at[all_rows], sem_in).wait()
        drain()

        def unpack(rb, c):
            rows = pl.ds(pl.multiple_of(rb * ROW_ALIGN, ROW_ALIGN), ROW_ALIGN)
            for ch in range(n_chunk):
                words = stage[pl.ds(rb * (ROW_ALIGN * n_chunk) + ch, ROW_ALIGN, stride=n_chunk), :]
                hi, lo = _unpack_pairs(words)
                xb16[rows, ch * LANES:(ch + 1) * LANES] = hi
                xb16[rows, d // 2 + ch * LANES:d // 2 + (ch + 1) * LANES] = lo
            acc[rows, :] = jnp.broadcast_to(bd_ref[0], (ROW_ALIGN, d))
            return c

        lax.fori_loop(0, n_align, unpack, 0)
        start_in(r0_next, blocks_next)

    def trip(row0, n_blk):
        rows = pl.ds(row0, n_blk * ROW_ALIGN)
        xb = xb16[rows, :]
        g = jnp.dot(xb, wg_ref[0].astype(BF16), preferred_element_type=F32) + bg_ref[0]
        u = jnp.dot(xb, wu_ref[0].astype(BF16), preferred_element_type=F32) + bu_ref[0]
        g = jnp.minimum(g, SWIGLU_LIMIT)
        u = jnp.clip(u, -SWIGLU_LIMIT, SWIGLU_LIMIT)
        a = (u + 1.0) * (g * jax.nn.sigmoid(SWIGLU_ALPHA * g))
        acc[rows, :] += jnp.dot(a.astype(BF16), wd_ref[0].astype(BF16), preferred_element_type=F32)

        @pl.when(j == nj - 1)
        def _():
            for s in range(n_blk):
                out_copy(row0 // ROW_ALIGN + s).start()
            pending[0] = pending[0] + n_blk

    @pl.when(active)
    def _():
        def main_trip(rb, c):
            trip(pl.multiple_of(rb * ROW_BLOCK, ROW_BLOCK), per_trip)
            return c

        lax.fori_loop(0, n_main, main_trip, 0)

        def tail_trip(t, c):
            trip(pl.multiple_of((n_main * per_trip + t) * ROW_ALIGN, ROW_ALIGN), 1)
            return c

        lax.fori_loop(0, n_align - n_main * per_trip, tail_trip, 0)

    @pl.when((i == pl.num_programs(0) - 1) & (j == nj - 1))
    def _():
        drain()
        acc[0:ROW_ALIGN, :] = jnp.zeros((ROW_ALIGN, d), F32)
        used = n_items[1]

        def tail_copy(t):
            dst = ys_hbm.at[pl.ds(pl.multiple_of((used + t) * ROW_ALIGN, ROW_ALIGN), ROW_ALIGN)]
            return pltpu.make_async_copy(acc.at[pl.ds(0, ROW_ALIGN)], dst, sem_out)

        def issue(t, c):
            tail_copy(t).start()
            return c

        n_tail = ys_hbm.shape[0] // ROW_ALIGN - used
        lax.fori_loop(0, n_tail, issue, 0)

        def wait(t, c):
            tail_copy(t).wait()
            return c

        lax.fori_loop(0, n_tail, wait, 0)


def _experts(xs, w_gate, b_gate, w_up, b_up, w_down, b_down, meta, *, n_rows, th):
    item_e, item_r0, item_nb, n_items = meta
    n_exp, d, dh = w_gate.shape
    nj = dh // th
    n_item_slots = item_e.shape[0]

    def w_col(i, j, item_e, item_r0, item_nb, n_items):
        return (item_e[i], 0, jnp.where(i < n_items[0], j, nj - 1))

    def w_row(i, j, item_e, item_r0, item_nb, n_items):
        return (item_e[i], jnp.where(i < n_items[0], j, nj - 1), 0)

    def b_all(i, j, item_e, item_r0, item_nb, n_items):
        return (item_e[i], 0, 0)

    grid_spec = pltpu.PrefetchScalarGridSpec(
        num_scalar_prefetch=4,
        grid=(n_item_slots, nj),
        in_specs=[
            pl.BlockSpec(memory_space=pl.ANY),
            pl.BlockSpec((1, d, th), w_col),
            pl.BlockSpec((1, 1, th), w_col),
            pl.BlockSpec((1, d, th), w_col),
            pl.BlockSpec((1, 1, th), w_col),
            pl.BlockSpec((1, th, d), w_row),
            pl.BlockSpec((1, 1, d), b_all),
        ],
        out_specs=pl.BlockSpec(memory_space=pl.ANY),
        scratch_shapes=[
            pltpu.VMEM((ITEM_ROWS * (d // 2 // LANES), LANES), jnp.uint32),
            pltpu.VMEM((ITEM_ROWS, d), BF16),
            pltpu.VMEM((ITEM_ROWS, d), F32),
            pltpu.SMEM((1,), jnp.int32),
            pltpu.SemaphoreType.DMA,
            pltpu.SemaphoreType.DMA,
        ],
    )
    return pl.pallas_call(
        functools.partial(_expert_kernel, nj=nj),
        grid_spec=grid_spec,
        out_shape=jax.ShapeDtypeStruct((n_rows, d), F32),
        compiler_params=_cparams(("arbitrary", "arbitrary")),
        name="experts",
    )(item_e, item_r0, item_nb, n_items,
      xs, w_gate, b_gate.reshape(n_exp, 1, dh), w_up, b_up.reshape(n_exp, 1, dh), w_down,
      b_down.reshape(n_exp, 1, d))


def _routing_metadata(top_idx, top_rank, counts, n_tok):
    n_assign = n_tok * TOP_K
    flat_e = top_idx.reshape(n_assign)
    rank = top_rank.reshape(n_assign)
    padded = (counts + ROW_ALIGN - 1) // ROW_ALIGN * ROW_ALIGN
    pend = jnp.cumsum(padded)
    pstart = pend - padded
    dest = (pstart[flat_e] + rank).astype(jnp.int32)
    n_rows = (n_assign // ROW_ALIGN + N_EXPERTS) * ROW_ALIGN
    flat_tok = jnp.arange(n_assign, dtype=jnp.int32) // TOP_K
    row_tok = jnp.zeros((n_rows,), jnp.int32).at[dest].set(
        flat_tok, unique_indices=True, mode="promise_in_bounds")
    n_chunks = (padded + ITEM_ROWS - 1) // ITEM_ROWS
    cend = jnp.cumsum(n_chunks)
    cstart = cend - n_chunks
    n_slots = N_EXPERTS + n_assign // ITEM_ROWS + 1
    slot = jnp.arange(n_slots, dtype=jnp.int32)
    n_items = cend[-1]
    last_e = jnp.max(jnp.where(n_chunks > 0, jnp.arange(N_EXPERTS), 0))
    e_of = jnp.minimum(jnp.searchsorted(cend, slot, side='right'), N_EXPERTS - 1)
    live = slot < n_items
    item_e = jnp.where(live, e_of, last_e).astype(jnp.int32)
    chunk = slot - cstart[e_of]
    item_r0 = jnp.where(live, pstart[e_of] + chunk * ITEM_ROWS, 0).astype(jnp.int32)
    item_rows = jnp.clip(padded[e_of] - chunk * ITEM_ROWS, 0, ITEM_ROWS)
    item_nb = jnp.where(live, item_rows // ROW_ALIGN, 0).astype(jnp.int32)
    counts_blk = jnp.stack([n_items, pend[-1] // ROW_ALIGN]).astype(jnp.int32)
    meta = (item_e, item_r0, item_nb, counts_blk)
    return meta, row_tok, dest, n_rows


def _combine_kernel(dest, ys_hbm, x2_ref, gate_ref, g_ref, o_ref, buf, sem, *, tm):
    i = pl.program_id(0)
    n = pl.num_programs(0)

    def row_copy(tile, slot, t, kk):
        r = dest[(tile * tm + t) * TOP_K + kk]
        return pltpu.make_async_copy(ys_hbm.at[pl.ds(r, 1)], buf.at[slot, kk, pl.ds(t, 1)], sem.at[slot])

    def issue_tile(tile, slot):
        def issue(t, c):
            for kk in range(TOP_K):
                row_copy(tile, slot, t, kk).start()
            return c

        lax.fori_loop(0, tm, issue, 0, unroll=8)

    slot = i % 2

    @pl.when(i == 0)
    def _():
        issue_tile(0, 0)

    @pl.when(i + 1 < n)
    def _():
        issue_tile(i + 1, 1 - slot)

    for kk in range(TOP_K):
        pltpu.make_async_copy(ys_hbm.at[pl.ds(0, tm)], buf.at[slot, kk], sem.at[slot]).wait()

    gates = gate_ref[...]
    y = x2_ref[...]
    for kk in range(TOP_K):
        y = y + gates[:, kk:kk + 1] * buf[slot, kk]
    o_ref[...] = (y * lax.rsqrt(jnp.mean(y * y, axis=-1, keepdims=True) + NORM_EPS)) * g_ref[...]


def _combine(dest, ys, x2, gates, g_final, *, tm):
    m, d = x2.shape
    grid_spec = pltpu.PrefetchScalarGridSpec(
        num_scalar_prefetch=1,
        grid=(m // tm,),
        in_specs=[
            pl.BlockSpec(memory_space=pl.ANY),
            pl.BlockSpec((tm, d), lambda i, dest: (i, 0)),
            pl.BlockSpec((tm, LANES), lambda i, dest: (i, 0)),
            pl.BlockSpec((1, d), lambda i, dest: (0, 0)),
        ],
        out_specs=pl.BlockSpec((tm, d), lambda i, dest: (i, 0)),
        scratch_shapes=[pltpu.VMEM((2, TOP_K, tm, d), F32), pltpu.SemaphoreType.DMA((2,))],
    )
    return pl.pallas_call(
        functools.partial(_combine_kernel, tm=tm),
        grid_spec=grid_spec,
        out_shape=jax.ShapeDtypeStruct((m, d), F32),
        compiler_params=_cparams(("arbitrary",)),
        name="combine",
    )(dest, ys, x2, gates, g_final)


def _layer(x, attn_norm, w_in, dil_out_norm, sb_out_norm, w_out, ffn_norm,
           w_router, b_router, w_gate, b_gate, w_up, b_up, w_down, b_down, *, tiles):
    b, seq, d = x.shape
    m = b * seq
    half_width = d // 2
    x2d = x.reshape(m, d)
    cos_t, sin_t = _rope_tables(seq)
    proj = _inproj(x2d, attn_norm.reshape(1, d), w_in, cos_t, sin_t,
                   seq=seq, tm=tiles["in_tm"], tn=tiles["in_tn"])
    proj3 = proj.reshape(b, seq, 3 * d)
    o_d = _dilated_attention(proj3, half_width=half_width).reshape(m, half_width)
    o_s = _stickbreak_attention(proj3, half_width=half_width, qt=tiles["sb_qt"]).reshape(m, half_width)
    x2, h2, idx_l, gate_l, rank_l, counts = _outproj_router(
        o_d, o_s, x2d, w_out, dil_out_norm.reshape(1, half_width), sb_out_norm.reshape(1, half_width),
        ffn_norm.reshape(1, d), w_router, b_router.reshape(1, N_EXPERTS),
        tm=tiles["out_tm"], tk=tiles["out_tk"])
    meta, row_tok, dest, n_rows = _routing_metadata(idx_l[:, :TOP_K], rank_l[:, :TOP_K], counts[0], m)
    xs = _dispatch(h2, row_tok, rows_blk=tiles["disp_rows"], n_chunk=d // 2 // LANES)
    ys = _experts(xs, w_gate, b_gate, w_up, b_up, w_down, b_down, meta, n_rows=n_rows, th=tiles["moe_th"])
    return x2, ys, dest, gate_l


_TILES = dict(in_tm=1024, in_tn=512, sb_qt=512, out_tm=512, out_tk=512, disp_rows=1024, moe_th=512, cmb_tm=128)


def kernel(x, attn_norm, w_in, dil_out_norm, sb_out_norm, w_out, ffn_norm, w_router, b_router,
           w_gate, b_gate, w_up, b_up, w_down, b_down, final_norm):
    depth = attn_norm.shape[0]
    b, seq, d = x.shape
    for l in range(depth):
        x2, ys, dest, gate_l = _layer(
            x, attn_norm[l], w_in[l], dil_out_norm[l], sb_out_norm[l], w_out[l], ffn_norm[l],
            w_router[l], b_router[l], w_gate[l], b_gate[l], w_up[l], b_up[l], w_down[l], b_down[l],
            tiles=_TILES)
        assert depth == 1
        x = _combine(dest, ys, x2, gate_l, final_norm.reshape(1, d), tm=_TILES["cmb_tm"])
    return x.reshape(b, seq, d)
```

```python
import functools
import math

import jax
import jax.numpy as jnp
from jax import lax
from jax.experimental import pallas as pl
from jax.experimental.pallas import tpu as pltpu

F32 = jnp.float32
BF16 = jnp.bfloat16

HEAD_DIM = 64
LANES = 128
BLOCK = 128
DIL_PATTERNS = ((128, 1), (512, 4), (2048, 16))
ROPE_THETA = 10000.0
N_EXPERTS = 32
TOP_K = 4
SWIGLU_LIMIT = 7.0
SWIGLU_ALPHA = 1.702
NORM_EPS = 1e-5
Q_SCALE = HEAD_DIM ** -0.5 * math.log2(math.e)
SB_UNROLL = 4
DIL_UNROLL = 8
ROW_ALIGN = 128
ROW_BLOCK = 1024
ITEM_ROWS = 1280
VMEM_LIMIT = 56 * 1024 * 1024


def _cparams(sem):
    return pltpu.CompilerParams(dimension_semantics=sem, vmem_limit_bytes=VMEM_LIMIT)


def _inproj_kernel(x_ref, g_ref, w_ref, cos_ref, sin_ref, o_ref, h_scr, *, tn, half_width):
    j = pl.program_id(1)

    @pl.when(j == 0)
    def _():
        x = x_ref[...]
        y = x * lax.rsqrt(jnp.mean(x * x, axis=-1, keepdims=True) + NORM_EPS)
        h_scr[...] = (y * g_ref[...]).astype(BF16)

    acc = jnp.dot(h_scr[...], w_ref[...].astype(BF16), preferred_element_type=F32)
    section = (j * tn) // half_width
    scale = jnp.where((section == 0) | (section == 3), Q_SCALE, 1.0).astype(F32)

    @pl.when(section < 2)
    def _():
        lane = lax.broadcasted_iota(jnp.int32, (acc.shape[0], LANES), 1)
        first_half = (lane % HEAD_DIM) < (HEAD_DIM // 2)
        c = cos_ref[...]
        s = sin_ref[...]
        for g in range(tn // LANES):
            xg = acc[:, g * LANES:(g + 1) * LANES]
            partner = jnp.where(first_half,
                                pltpu.roll(xg, LANES - HEAD_DIM // 2, 1),
                                pltpu.roll(xg, HEAD_DIM // 2, 1))
            o_ref[:, g * LANES:(g + 1) * LANES] = ((xg * c + partner * s) * scale).astype(BF16)

    @pl.when(section >= 2)
    def _():
        o_ref[...] = (acc * scale).astype(BF16)


def _inproj(x2d, g, w_in, cos_t, sin_t, *, seq, tm, tn):
    m, d = x2d.shape
    n = w_in.shape[1]
    half_width = d // 2
    assert half_width % tn == 0 and tn % LANES == 0 and m % tm == 0 and seq % tm == 0
    pos_tiles = seq // tm
    return pl.pallas_call(
        functools.partial(_inproj_kernel, tn=tn, half_width=half_width),
        grid=(m // tm, n // tn),
        in_specs=[
            pl.BlockSpec((tm, d), lambda i, j: (i, 0)),
            pl.BlockSpec((1, d), lambda i, j: (0, 0)),
            pl.BlockSpec((d, tn), lambda i, j: (0, j)),
            pl.BlockSpec((tm, LANES), lambda i, j: (i % pos_tiles, 0)),
            pl.BlockSpec((tm, LANES), lambda i, j: (i % pos_tiles, 0)),
        ],
        out_specs=pl.BlockSpec((tm, tn), lambda i, j: (i, j)),
        out_shape=jax.ShapeDtypeStruct((m, n), BF16),
        scratch_shapes=[pltpu.VMEM((tm, d), BF16)],
        compiler_params=_cparams(("arbitrary", "arbitrary")),
        name="inproj",
    )(x2d, g, w_in, cos_t, sin_t)


def _rope_tables(seq):
    half = HEAD_DIM // 2
    inv_freq = 1.0 / (ROPE_THETA ** (jnp.arange(half, dtype=F32) * 2.0 / HEAD_DIM))
    ang = jnp.arange(seq, dtype=F32)[:, None] * inv_freq[None, :]
    cos = jnp.cos(ang)
    sin = jnp.sin(ang)
    reps = LANES // HEAD_DIM
    cos_t = jnp.tile(jnp.concatenate([cos, cos], axis=1), (1, reps))
    sin_t = jnp.tile(jnp.concatenate([-sin, sin], axis=1), (1, reps))
    return cos_t, sin_t


def _dil_kernel(q_ref, k_ref, v_ref, o_ref, f32_scr, qp, kp, vp, operm, lperm, onat, lnat, *, seq):
    n_blocks = seq // BLOCK
    lane_q = lax.broadcasted_iota(jnp.int32, (BLOCK, LANES), 1)
    head_a_q = lane_q < HEAD_DIM
    zero_b = jnp.zeros((), BF16)

    def window_consts(n_kb):
        width = n_kb * BLOCK
        head_a_k = lax.broadcasted_iota(jnp.int32, (width, LANES), 1) < HEAD_DIM
        qi = lax.broadcasted_iota(jnp.int32, (BLOCK, width), 0)
        ki = lax.broadcasted_iota(jnp.int32, (BLOCK, width), 1)
        dist = qi + (width - BLOCK) - ki
        band = (dist >= 0) & (dist <= BLOCK)
        cur_only = ki >= width - BLOCK
        head_ones = jnp.concatenate([jnp.where(head_a_k, 1.0, 0.0), jnp.where(head_a_k, 0.0, 1.0)],
                                    axis=0).astype(BF16)
        return width, head_a_k, band, cur_only, head_ones

    f32_scr[0] = q_ref[0].astype(F32)
    f32_scr[1] = k_ref[0].astype(F32)
    f32_scr[2] = v_ref[0].astype(F32)
    kp[0:BLOCK, :] = jnp.zeros((BLOCK, LANES), BF16)
    vp[0:BLOCK, :] = jnp.zeros((BLOCK, LANES), BF16)

    for p, (window, dil) in enumerate(DIL_PATTERNS):
        assert window // dil == BLOCK
        length = seq // dil
        per_stream = length // BLOCK
        assert length % BLOCK == 0
        if dil == 1:
            qp[...] = q_ref[0]
            kp[BLOCK:, :] = k_ref[0]
            vp[BLOCK:, :] = v_ref[0]
        else:
            for r in range(dil):
                rows = pl.ds(r, length, stride=dil)
                qp[r * length:(r + 1) * length, :] = f32_scr[0, rows, :].astype(BF16)
                kp[BLOCK + r * length:BLOCK + (r + 1) * length, :] = f32_scr[1, rows, :].astype(BF16)
                vp[BLOCK + r * length:BLOCK + (r + 1) * length, :] = f32_scr[2, rows, :].astype(BF16)
        o_dst = onat.at[p] if dil == 1 else operm
        l_dst = lnat.at[p] if dil == 1 else lperm

        consts = window_consts(1 if per_stream == 1 else 2)

        def block_body(bi, carry, per_stream=per_stream, o_dst=o_dst, l_dst=l_dst, consts=consts):
            width, head_a_k, band, cur_only, head_ones = consts
            row0 = pl.multiple_of(bi * BLOCK, BLOCK)
            qb = qp[pl.ds(row0, BLOCK), :]
            win0 = pl.multiple_of(row0 + 2 * BLOCK - width, BLOCK)
            kw = kp[pl.ds(win0, width), :]
            vw = vp[pl.ds(win0, width), :]
            has_prev = (bi % per_stream) != 0
            valid = band & (cur_only | has_prev)
            kk = jnp.concatenate([jnp.where(head_a_k, kw, zero_b), jnp.where(head_a_k, zero_b, kw)], axis=0)
            s2 = lax.dot_general(qb, kk, (((1,), (1,)), ((), ())), preferred_element_type=F32)
            ps, ms = [], []
            for h in range(2):
                s = jnp.where(valid, s2[:, h * width:(h + 1) * width], -jnp.inf)
                m = jnp.max(s, axis=-1, keepdims=True)
                ps.append(jnp.exp2(s - m).astype(BF16))
                ms.append(m)
            vv = jnp.concatenate([jnp.where(head_a_k, vw, zero_b), jnp.where(head_a_k, zero_b, vw)], axis=0)
            o_den = jnp.dot(jnp.concatenate(ps, axis=1), jnp.concatenate([vv, head_ones], axis=1),
                            preferred_element_type=F32)
            o = o_den[:, :LANES]
            den = o_den[:, LANES:]
            o_dst[pl.ds(row0, BLOCK), :] = o / den
            l_dst[pl.ds(row0, BLOCK), :] = jnp.where(head_a_q, ms[0], ms[1]) + jnp.log2(den)
            return carry

        lax.fori_loop(0, n_blocks, block_body, 0, unroll=DIL_UNROLL)
        if dil != 1:
            for r in range(dil):
                rows = pl.ds(r, length, stride=dil)
                onat[p, rows, :] = operm[r * length:(r + 1) * length, :]
                lnat[p, rows, :] = lperm[r * length:(r + 1) * length, :]

    def mix_body(c, carry):
        rows = pl.ds(pl.multiple_of(c * BLOCK, BLOCK), BLOCK)
        ls = [lnat[p, rows, :] for p in range(len(DIL_PATTERNS))]
        m = functools.reduce(jnp.maximum, ls)
        es = [jnp.exp2(l - m) for l in ls]
        num = functools.reduce(lambda a, b: a + b, [e * onat[p, rows, :] for p, e in enumerate(es)])
        o_ref[0, rows, :] = num / functools.reduce(lambda a, b: a + b, es)
        return carry

    lax.fori_loop(0, n_blocks, mix_body, 0)


def _dilated_attention(proj3, *, half_width):
    b, seq, _ = proj3.shape
    pairs = half_width // LANES
    n_pat = len(DIL_PATTERNS)
    spec = lambda off: pl.BlockSpec((1, seq, LANES), lambda bi, c, off=off: (bi, 0, off + c))
    return pl.pallas_call(
        functools.partial(_dil_kernel, seq=seq),
        grid=(b, pairs),
        in_specs=[spec(0), spec(pairs), spec(2 * pairs)],
        out_specs=pl.BlockSpec((1, seq, LANES), lambda bi, c: (bi, 0, c)),
        out_shape=jax.ShapeDtypeStruct((b, seq, half_width), F32),
        scratch_shapes=[
            pltpu.VMEM((3, seq, LANES), F32),
            pltpu.VMEM((seq, LANES), BF16),
            pltpu.VMEM((seq + BLOCK, LANES), BF16),
            pltpu.VMEM((seq + BLOCK, LANES), BF16),
            pltpu.VMEM((seq, LANES), F32),
            pltpu.VMEM((seq, LANES), F32),
            pltpu.VMEM((n_pat, seq, LANES), F32),
            pltpu.VMEM((n_pat, seq, LANES), F32),
        ],
        compiler_params=_cparams(("arbitrary", "arbitrary")),
        name="dilated_attention",
    )(proj3, proj3, proj3)


def _sb_kernel(q_ref, k_ref, v_ref, o_ref, kk_scr, vv_scr, tri_scr, acc_scr, carry_scr, *, seq, qt):
    n_blocks = seq // BLOCK
    per_tile = qt // BLOCK
    lane_k = lax.broadcasted_iota(jnp.int32, (BLOCK, LANES), 1)
    head_a = lane_k < HEAD_DIM
    zero_b = jnp.zeros((), BF16)

    def prep(jb, carry):
        rows = pl.ds(pl.multiple_of(jb * BLOCK, BLOCK), BLOCK)
        kb = k_ref[0, rows, :]
        vb = v_ref[0, rows, :]
        kk_scr[jb, 0:BLOCK, :] = jnp.where(head_a, kb, zero_b)
        kk_scr[jb, BLOCK:, :] = jnp.where(head_a, zero_b, kb)
        vv_scr[jb, 0:BLOCK, :] = jnp.where(head_a, vb, zero_b)
        vv_scr[jb, BLOCK:, :] = jnp.where(head_a, zero_b, vb)
        return carry

    lax.fori_loop(0, n_blocks, prep, 0)

    jj = lax.broadcasted_iota(jnp.int32, (2 * BLOCK, 2 * BLOCK), 0) % BLOCK
    ss = lax.broadcasted_iota(jnp.int32, (2 * BLOCK, 2 * BLOCK), 1)
    tri_scr[...] = jnp.where((ss >= BLOCK) | (jj > ss), 1.0, 0.0).astype(BF16)

    def key_block(qb, jb, key_off):
        rs = slice(0 if key_off is None else key_off, qt)
        z = lax.dot_general(qb[rs], kk_scr[jb], (((1,), (1,)), ((), ())), preferred_element_type=F32)
        neg_abs = lax.bitcast_convert_type(lax.bitcast_convert_type(z, jnp.uint32) | jnp.uint32(0x80000000), F32)
        ls = jnp.minimum(z, 0.0) - jnp.log2(1.0 + jnp.exp2(neg_abs))
        lr = ls - z
        if key_off is not None:
            n = qt - key_off
            causal = (lax.broadcasted_iota(jnp.int32, (n, 2 * BLOCK), 1) % BLOCK
                      < lax.broadcasted_iota(jnp.int32, (n, 2 * BLOCK), 0))
            lr = jnp.where(causal, lr, 0.0)
        hi = lr.astype(BF16)
        lo = (lr - hi.astype(F32)).astype(BF16)
        tri = tri_scr[...]
        a_heads = []
        for h in range(2):
            sl = slice(h * BLOCK, (h + 1) * BLOCK)
            lhs = jnp.concatenate([hi[:, sl], lo[:, sl]], axis=1)
            cs_tot = jnp.dot(lhs, tri, preferred_element_type=F32)
            carry = carry_scr[rs, sl]
            a = jnp.exp2(ls[:, sl] + cs_tot[:, :BLOCK] + carry)
            if key_off is not None:
                a = jnp.where(causal[:, :BLOCK], a, 0.0)
            carry_scr[rs, sl] = carry + cs_tot[:, BLOCK:]
            a_heads.append(a.astype(BF16))
        acc_scr[rs, :] += jnp.dot(jnp.concatenate(a_heads, axis=1), vv_scr[jb], preferred_element_type=F32)

    def q_tile(i, carry):
        rows = pl.ds(pl.multiple_of(i * qt, qt), qt)
        qb = q_ref[0, rows, :]
        acc_scr[...] = jnp.zeros_like(acc_scr)
        carry_scr[...] = jnp.zeros_like(carry_scr)
        for t in reversed(range(per_tile)):
            key_block(qb, i * per_tile + t, t * BLOCK)

        def inner(t, c):
            for s in range(SB_UNROLL):
                key_block(qb, i * per_tile - 1 - (t * SB_UNROLL + s), None)
            return c

        lax.fori_loop(0, i * (per_tile // SB_UNROLL), inner, 0)
        o_ref[0, rows, :] = acc_scr[...]
        return carry

    lax.fori_loop(0, seq // qt, q_tile, 0)


def _stickbreak_attention(proj3, *, half_width, qt):
    b, seq, _ = proj3.shape
    pairs = half_width // LANES
    n_blocks = seq // BLOCK
    assert seq % qt == 0 and qt % BLOCK == 0
    spec = lambda off: pl.BlockSpec((1, seq, LANES), lambda bi, c, off=off: (bi, 0, off + c))
    return pl.pallas_call(
        functools.partial(_sb_kernel, seq=seq, qt=qt),
        grid=(b, pairs),
        in_specs=[spec(3 * pairs), spec(4 * pairs), spec(5 * pairs)],
        out_specs=pl.BlockSpec((1, seq, LANES), lambda bi, c: (bi, 0, c)),
        out_shape=jax.ShapeDtypeStruct((b, seq, half_width), F32),
        scratch_shapes=[
            pltpu.VMEM((n_blocks, 2 * BLOCK, LANES), BF16),
            pltpu.VMEM((n_blocks, 2 * BLOCK, LANES), BF16),
            pltpu.VMEM((2 * BLOCK, 2 * BLOCK), BF16),
            pltpu.VMEM((qt, LANES), F32),
            pltpu.VMEM((qt, 2 * BLOCK), F32),
        ],
        compiler_params=_cparams(("arbitrary", "arbitrary")),
        name="stickbreak_attention",
    )(proj3, proj3, proj3)


def _split_bf16(x):
    hi = x.astype(BF16)
    return hi, (x - hi.astype(F32)).astype(BF16)


def _outproj_kernel(od_ref, os_ref, x_ref, w_ref, gd_ref, gs_ref, gf_ref, wr_ref, br_ref,
                    x2_ref, h2_ref, idx_ref, gate_ref, rank_ref, cnt_ref,
                    acc_d, acc_s, ssq_d, ssq_s, cnt_scr, *, nk, half_width):
    k = pl.program_id(1)

    def accumulate(a_ref, g_ref, acc, ssq, first):
        a = a_ref[...]
        sq = jnp.sum(a * a, axis=-1, keepdims=True)
        prod = jnp.dot((a * g_ref[...]).astype(BF16), w_ref[...].astype(BF16), preferred_element_type=F32)
        if first:
            ssq[...] = sq
            acc[...] = prod
        else:
            ssq[...] += sq
            acc[...] += prod

    for kk in range(nk):
        @pl.when(k == kk)
        def _(kk=kk):
            if kk < nk // 2:
                accumulate(od_ref, gd_ref, acc_d, ssq_d, kk == 0)
            else:
                accumulate(os_ref, gs_ref, acc_s, ssq_s, kk == nk // 2)

    @pl.when(k == nk - 1)
    def _():
        r_d = lax.rsqrt(ssq_d[...] / half_width + NORM_EPS)
        r_s = lax.rsqrt(ssq_s[...] / half_width + NORM_EPS)
        x2 = x_ref[...] + (acc_d[...] * r_d + acc_s[...] * r_s)
        x2_ref[...] = x2
        h2 = (x2 * lax.rsqrt(jnp.mean(x2 * x2, axis=-1, keepdims=True) + NORM_EPS)) * gf_ref[...]
        packed = _pack_pairs(h2)
        n_chunk = packed.shape[1] // LANES
        for ch in range(n_chunk):
            h2_ref[pl.ds(ch, packed.shape[0], stride=n_chunk), :] = packed[:, ch * LANES:(ch + 1) * LANES]
        h_hi, h_lo = _split_bf16(h2)
        w_hi, w_lo = _split_bf16(wr_ref[...])
        hw = jnp.dot(h_hi, jnp.concatenate([w_hi, w_lo], axis=1), preferred_element_type=F32)
        logits = (hw[:, :N_EXPERTS] + hw[:, N_EXPERTS:]
                  + jnp.dot(h_lo, w_hi, preferred_element_type=F32)) + br_ref[...]
        tm = logits.shape[0]
        e_iota = lax.broadcasted_iota(jnp.int32, logits.shape, 1)
        out_lane = lax.broadcasted_iota(jnp.int32, (tm, LANES), 1)
        idx_out = jnp.zeros((tm, LANES), jnp.int32)
        val_out = jnp.full((tm, LANES), -jnp.inf, F32)
        top0 = None
        picks = []
        for kk in range(TOP_K):
            m = jnp.max(logits, axis=-1, keepdims=True)
            sel = jnp.min(jnp.where(logits == m, e_iota, N_EXPERTS), axis=-1, keepdims=True)
            idx_out = jnp.where(out_lane == kk, sel, idx_out)
            val_out = jnp.where(out_lane == kk, m, val_out)
            picks.append(e_iota == sel)
            logits = jnp.where(picks[-1], -jnp.inf, logits)
            if kk == 0:
                top0 = m
        ex = jnp.exp(val_out - top0)
        gate_ref[...] = ex / jnp.sum(ex, axis=-1, keepdims=True)
        idx_ref[...] = idx_out

        @pl.when(pl.program_id(0) == 0)
        def _():
            cnt_scr[...] = jnp.zeros_like(cnt_scr)

        chosen = functools.reduce(jnp.logical_or, picks)
        t_row = lax.broadcasted_iota(jnp.int32, (tm, tm), 0)
        t_col = lax.broadcasted_iota(jnp.int32, (tm, tm), 1)
        before = jnp.where(t_col < t_row, 1.0, 0.0).astype(BF16)
        multi_hot = jnp.where(chosen, 1.0, 0.0)
        prior = jnp.dot(before, multi_hot.astype(BF16), preferred_element_type=F32) + cnt_scr[...]
        rank_out = jnp.zeros((tm, LANES), jnp.int32)
        for kk in range(TOP_K):
            r = jnp.sum(jnp.where(picks[kk], prior, 0.0), axis=-1, keepdims=True)
            rank_out = jnp.where(out_lane == kk, r.astype(jnp.int32), rank_out)
        rank_ref[...] = rank_out
        cnt_scr[...] += jnp.sum(multi_hot, axis=0, keepdims=True)
        cnt_ref[...] = cnt_scr[...].astype(jnp.int32)


def _outproj_router(o_d, o_s, x2d, w_out, g_d, g_s, g_f, w_router, b_router, *, tm, tk):
    m, d = x2d.shape
    half_width = d // 2
    nk = d // tk
    assert half_width % tk == 0 and m % tm == 0
    kh = nk // 2
    return pl.pallas_call(
        functools.partial(_outproj_kernel, nk=nk, half_width=half_width),
        grid=(m // tm, nk),
        in_specs=[
            pl.BlockSpec((tm, tk), lambda i, k: (i, jnp.minimum(k, kh - 1))),
            pl.BlockSpec((tm, tk), lambda i, k: (i, jnp.maximum(k - kh, 0))),
            pl.BlockSpec((tm, d), lambda i, k: (i, 0)),
            pl.BlockSpec((tk, d), lambda i, k: (k, 0)),
            pl.BlockSpec((1, tk), lambda i, k: (0, jnp.minimum(k, kh - 1))),
            pl.BlockSpec((1, tk), lambda i, k: (0, jnp.maximum(k - kh, 0))),
            pl.BlockSpec((1, d), lambda i, k: (0, 0)),
            pl.BlockSpec((d, N_EXPERTS), lambda i, k: (0, 0)),
            pl.BlockSpec((1, N_EXPERTS), lambda i, k: (0, 0)),
        ],
        out_specs=[
            pl.BlockSpec((tm, d), lambda i, k: (i, 0)),
            pl.BlockSpec((tm * (d // 2 // LANES), LANES), lambda i, k: (i, 0)),
            pl.BlockSpec((tm, LANES), lambda i, k: (i, 0)),
            pl.BlockSpec((tm, LANES), lambda i, k: (i, 0)),
            pl.BlockSpec((tm, LANES), lambda i, k: (i, 0)),
            pl.BlockSpec((1, N_EXPERTS), lambda i, k: (0, 0)),
        ],
        out_shape=[
            jax.ShapeDtypeStruct((m, d), F32),
            jax.ShapeDtypeStruct((m * (d // 2 // LANES), LANES), jnp.uint32),
            jax.ShapeDtypeStruct((m, LANES), jnp.int32),
            jax.ShapeDtypeStruct((m, LANES), F32),
            jax.ShapeDtypeStruct((m, LANES), jnp.int32),
            jax.ShapeDtypeStruct((1, N_EXPERTS), jnp.int32),
        ],
        scratch_shapes=[
            pltpu.VMEM((tm, d), F32),
            pltpu.VMEM((tm, d), F32),
            pltpu.VMEM((tm, 1), F32),
            pltpu.VMEM((tm, 1), F32),
            pltpu.VMEM((1, N_EXPERTS), F32),
        ],
        compiler_params=_cparams(("arbitrary", "arbitrary")),
        name="outproj_router",
    )(o_d, o_s, x2d, w_out, g_d, g_s, g_f, w_router, b_router)


def _unpack_pairs(words):
    hi = lax.bitcast_convert_type(words & jnp.uint32(0xFFFF0000), F32)
    lo = lax.bitcast_convert_type(words << 16, F32)
    return hi.astype(BF16), lo.astype(BF16)


def _pack_pairs(x):
    c = x.shape[1] // 2
    as_bits = lambda v: lax.bitcast_convert_type(v.astype(BF16).astype(F32), jnp.uint32)
    return as_bits(x[:, :c]) | (as_bits(x[:, c:]) >> 16)


def _dispatch_kernel(row_tok, h_hbm, o_ref, h_vmem, sem, *, rows_blk, n_chunk):
    i = pl.program_id(0)

    @pl.when(i == 0)
    def _():
        cp = pltpu.make_async_copy(h_hbm, h_vmem, sem)
        cp.start()
        cp.wait()

    base = i * rows_blk

    def body(r, c):
        tok = row_tok[base + r]
        src = pl.ds(pl.multiple_of(tok * n_chunk, n_chunk), n_chunk)
        o_ref[pl.ds(pl.multiple_of(r * n_chunk, n_chunk), n_chunk), :] = h_vmem[src, :]
        return c

    lax.fori_loop(0, rows_blk, body, 0, unroll=8)


def _dispatch(hp, row_tok, *, rows_blk, n_chunk):
    n_rows = row_tok.shape[0]
    assert n_rows % rows_blk == 0
    grid_spec = pltpu.PrefetchScalarGridSpec(
        num_scalar_prefetch=1,
        grid=(n_rows // rows_blk,),
        in_specs=[pl.BlockSpec(memory_space=pl.ANY)],
        out_specs=pl.BlockSpec((rows_blk * n_chunk, LANES), lambda i, row_tok: (i, 0)),
        scratch_shapes=[pltpu.VMEM(hp.shape, hp.dtype), pltpu.SemaphoreType.DMA],
    )
    return pl.pallas_call(
        functools.partial(_dispatch_kernel, rows_blk=rows_blk, n_chunk=n_chunk),
        grid_spec=grid_spec,
        out_shape=jax.ShapeDtypeStruct((n_rows * n_chunk, LANES), hp.dtype),
        compiler_params=_cparams(("arbitrary",)),
        name="dispatch",
    )(row_tok, hp)


def _expert_kernel(item_e, item_r0, item_nb, n_items,
                   xs_hbm, wg_ref, bg_ref, wu_ref, bu_ref, wd_ref, bd_ref,
                   ys_hbm, stage, xb16, acc, pending, sem_in, sem_out, *, nj):
    i = pl.program_id(0)
    j = pl.program_id(1)
    n_live = n_items[0]
    active = i < n_live
    r0 = item_r0[i]
    n_align = item_nb[i]
    n_rows = n_align * ROW_ALIGN
    per_trip = ROW_BLOCK // ROW_ALIGN
    n_main = n_align // per_trip
    nxt = jnp.minimum(i + 1, pl.num_programs(0) - 1)
    r0_next = item_r0[nxt]
    blocks_next = jnp.where(i + 1 < n_live, item_nb[nxt], 0)
    d = acc.shape[1]
    n_chunk = d // 2 // LANES
    blk_words = ROW_ALIGN * n_chunk

    def in_copy(base_row, blk):
        src = xs_hbm.at[pl.ds(pl.multiple_of((base_row + blk * ROW_ALIGN) * n_chunk, blk_words), blk_words)]
        return pltpu.make_async_copy(src, stage.at[pl.ds(pl.multiple_of(blk * blk_words, blk_words), blk_words)],
                                     sem_in)

    def start_in(base_row, n_blk):
        def issue(blk, c):
            in_copy(base_row, blk).start()
            return c

        lax.fori_loop(0, n_blk, issue, 0)

    def out_copy(blk):
        src = acc.at[pl.ds(pl.multiple_of(blk * ROW_ALIGN, ROW_ALIGN), ROW_ALIGN)]
        dst = ys_hbm.at[pl.ds(pl.multiple_of(r0 + blk * ROW_ALIGN, ROW_ALIGN), ROW_ALIGN)]
        return pltpu.make_async_copy(src, dst, sem_out)

    def drain():
        @pl.when(pending[0] > 0)
        def _():
            rows = pl.ds(0, pending[0] * ROW_ALIGN)
            pltpu.make_async_copy(acc.at[rows], ys_hbm.at[rows], sem_out).wait()
            pending[0] = 0

    @pl.when((i == 0) & (j == 0))
    def _():
        pending[0] = 0
        start_in(r0, n_align)

    @pl.when(active & (j == 0))
    def _():
        all_rows = pl.ds(0, n_rows * n_chunk)
        pltpu.make_async_copy(xs_hbm.at[all_rows], stage.at[all_rows], sem_in).wait()
        drain()

        def unpack(rb, c):
            rows = pl.ds(pl.multiple_of(rb * ROW_ALIGN, ROW_ALIGN), ROW_ALIGN)
            for ch in range(n_chunk):
                words = stage[pl.ds(rb * (ROW_ALIGN * n_chunk) + ch, ROW_ALIGN, stride=n_chunk), :]
                hi, lo = _unpack_pairs(words)
                xb16[rows, ch * LANES:(ch + 1) * LANES] = hi
                xb16[rows, d // 2 + ch * LANES:d // 2 + (ch + 1) * LANES] = lo
            acc[rows, :] = jnp.broadcast_to(bd_ref[0], (ROW_ALIGN, d))
            return c

        lax.fori_loop(0, n_align, unpack, 0)
        start_in(r0_next, blocks_next)

    def trip(row0, n_blk):
        rows = pl.ds(row0, n_blk * ROW_ALIGN)
        xb = xb16[rows, :]
        g = jnp.dot(xb, wg_ref[0].astype(BF16), preferred_element_type=F32) + bg_ref[0]
        u = jnp.dot(xb, wu_ref[0].astype(BF16), preferred_element_type=F32) + bu_ref[0]
        g = jnp.minimum(g, SWIGLU_LIMIT)
        u = jnp.clip(u, -SWIGLU_LIMIT, SWIGLU_LIMIT)
        a = (u + 1.0) * (g * jax.nn.sigmoid(SWIGLU_ALPHA * g))
        acc[rows, :] += jnp.dot(a.astype(BF16), wd_ref[0].astype(BF16), preferred_element_type=F32)

        @pl.when(j == nj - 1)
        def _():
            for s in range(n_blk):
                out_copy(row0 // ROW_ALIGN + s).start()
            pending[0] = pending[0] + n_blk

    @pl.when(active)
    def _():
        def main_trip(rb, c):
            trip(pl.multiple_of(rb * ROW_BLOCK, ROW_BLOCK), per_trip)
            return c

        lax.fori_loop(0, n_main, main_trip, 0)

        def tail_trip(t, c):
            trip(pl.multiple_of((n_main * per_trip + t) * ROW_ALIGN, ROW_ALIGN), 1)
            return c

        lax.fori_loop(0, n_align - n_main * per_trip, tail_trip, 0)

    @pl.when((i == pl.num_programs(0) - 1) & (j == nj - 1))
    def _():
        drain()
        acc[0:ROW_ALIGN, :] = jnp.zeros((ROW_ALIGN, d), F32)
        used = n_items[1]

        def tail_copy(t):
            dst = ys_hbm.at[pl.ds(pl.multiple_of((used + t) * ROW_ALIGN, ROW_ALIGN), ROW_ALIGN)]
            return pltpu.make_async_copy(acc.at[pl.ds(0, ROW_ALIGN)], dst, sem_out)

        def issue(t, c):
            tail_copy(t).start()
            return c

        n_tail = ys_hbm.shape[0] // ROW_ALIGN - used
        lax.fori_loop(0, n_tail, issue, 0)

        def wait(t, c):
            tail_copy(t).wait()
            return c

        lax.fori_loop(0, n_tail, wait, 0)


def _experts(xs, w_gate, b_gate, w_up, b_up, w_down, b_down, meta, *, n_rows, th):
    item_e, item_r0, item_nb, n_items = meta
    n_exp, d, dh = w_gate.shape
    nj = dh // th
    n_item_slots = item_e.shape[0]

    def w_col(i, j, item_e, item_r0, item_nb, n_items):
        return (item_e[i], 0, jnp.where(i < n_items[0], j, nj - 1))

    def w_row(i, j, item_e, item_r0, item_nb, n_items):
        return (item_e[i], jnp.where(i < n_items[0], j, nj - 1), 0)

    def b_all(i, j, item_e, item_r0, item_nb, n_items):
        return (item_e[i], 0, 0)

    grid_spec = pltpu.PrefetchScalarGridSpec(
        num_scalar_prefetch=4,
        grid=(n_item_slots, nj),
        in_specs=[
            pl.BlockSpec(memory_space=pl.ANY),
            pl.BlockSpec((1, d, th), w_col),
            pl.BlockSpec((1, 1, th), w_col),
            pl.BlockSpec((1, d, th), w_col),
            pl.BlockSpec((1, 1, th), w_col),
            pl.BlockSpec((1, th, d), w_row),
            pl.BlockSpec((1, 1, d), b_all),
        ],
        out_specs=pl.BlockSpec(memory_space=pl.ANY),
        scratch_shapes=[
            pltpu.VMEM((ITEM_ROWS * (d // 2 // LANES), LANES), jnp.uint32),
            pltpu.VMEM((ITEM_ROWS, d), BF16),
            pltpu.VMEM((ITEM_ROWS, d), F32),
            pltpu.SMEM((1,), jnp.int32),
            pltpu.SemaphoreType.DMA,
            pltpu.SemaphoreType.DMA,
        ],
    )
    return pl.pallas_call(
        functools.partial(_expert_kernel, nj=nj),
        grid_spec=grid_spec,
        out_shape=jax.ShapeDtypeStruct((n_rows, d), F32),
        compiler_params=_cparams(("arbitrary", "arbitrary")),
        name="experts",
    )(item_e, item_r0, item_nb, n_items,
      xs, w_gate, b_gate.reshape(n_exp, 1, dh), w_up, b_up.reshape(n_exp, 1, dh), w_down,
      b_down.reshape(n_exp, 1, d))


def _routing_metadata(top_idx, top_rank, counts, n_tok):
    n_assign = n_tok * TOP_K
    flat_e = top_idx.reshape(n_assign)
    rank = top_rank.reshape(n_assign)
    padded = (counts + ROW_ALIGN - 1) // ROW_ALIGN * ROW_ALIGN
    pend = jnp.cumsum(padded)
    pstart = pend - padded
    dest = (pstart[flat_e] + rank).astype(jnp.int32)
    n_rows = (n_assign // ROW_ALIGN + N_EXPERTS) * ROW_ALIGN
    flat_tok = jnp.arange(n_assign, dtype=jnp.int32) // TOP_K
    row_tok = jnp.zeros((n_rows,), jnp.int32).at[dest].set(
        flat_tok, unique_indices=True, mode="promise_in_bounds")
    n_chunks = (padded + ITEM_ROWS - 1) // ITEM_ROWS
    cend = jnp.cumsum(n_chunks)
    cstart = cend - n_chunks
    n_slots = N_EXPERTS + n_assign // ITEM_ROWS + 1
    slot = jnp.arange(n_slots, dtype=jnp.int32)
    n_items = cend[-1]
    last_e = jnp.max(jnp.where(n_chunks > 0, jnp.arange(N_EXPERTS), 0))
    e_of = jnp.minimum(jnp.searchsorted(cend, slot, side='right'), N_EXPERTS - 1)
    live = slot < n_items
    item_e = jnp.where(live, e_of, last_e).astype(jnp.int32)
    chunk = slot - cstart[e_of]
    item_r0 = jnp.where(live, pstart[e_of] + chunk * ITEM_ROWS, 0).astype(jnp.int32)
    item_rows = jnp.clip(padded[e_of] - chunk * ITEM_ROWS, 0, ITEM_ROWS)
    item_nb = jnp.where(live, item_rows // ROW_ALIGN, 0).astype(jnp.int32)
    counts_blk = jnp.stack([n_items, pend[-1] // ROW_ALIGN]).astype(jnp.int32)
    meta = (item_e, item_r0, item_nb, counts_blk)
    return meta, row_tok, dest, n_rows


def _combine_kernel(dest, ys_hbm, x2_ref, gate_ref, g_ref, o_ref, buf, sem, *, tm):
    i = pl.program_id(0)
    n = pl.num_programs(0)

    def row_copy(tile, slot, t, kk):
        r = dest[(tile * tm + t) * TOP_K + kk]
        return pltpu.make_async_copy(ys_hbm.at[pl.ds(r, 1)], buf.at[slot, kk, pl.ds(t, 1)], sem.at[slot])

    def issue_tile(tile, slot):
        def issue(t, c):
            for kk in range(TOP_K):
                row_copy(tile, slot, t, kk).start()
            return c

        lax.fori_loop(0, tm, issue, 0, unroll=8)

    slot = i % 2

    @pl.when(i == 0)
    def _():
        issue_tile(0, 0)

    @pl.when(i + 1 < n)
    def _():
        issue_tile(i + 1, 1 - slot)

    for kk in range(TOP_K):
        pltpu.make_async_copy(ys_hbm.at[pl.ds(0, tm)], buf.at[slot, kk], sem.at[slot]).wait()

    gates = gate_ref[...]
    y = x2_ref[...]
    for kk in range(TOP_K):
        y = y + gates[:, kk:kk + 1] * buf[slot, kk]
    o_ref[...] = (y * lax.rsqrt(jnp.mean(y * y, axis=-1, keepdims=True) + NORM_EPS)) * g_ref[...]


def _combine(dest, ys, x2, gates, g_final, *, tm):
    m, d = x2.shape
    grid_spec = pltpu.PrefetchScalarGridSpec(
        num_scalar_prefetch=1,
        grid=(m // tm,),
        in_specs=[
            pl.BlockSpec(memory_space=pl.ANY),
            pl.BlockSpec((tm, d), lambda i, dest: (i, 0)),
            pl.BlockSpec((tm, LANES), lambda i, dest: (i, 0)),
            pl.BlockSpec((1, d), lambda i, dest: (0, 0)),
        ],
        out_specs=pl.BlockSpec((tm, d), lambda i, dest: (i, 0)),
        scratch_shapes=[pltpu.VMEM((2, TOP_K, tm, d), F32), pltpu.SemaphoreType.DMA((2,))],
    )
    return pl.pallas_call(
        functools.partial(_combine_kernel, tm=tm),
        grid_spec=grid_spec,
        out_shape=jax.ShapeDtypeStruct((m, d), F32),
        compiler_params=_cparams(("arbitrary",)),
        name="combine",
    )(dest, ys, x2, gates, g_final)


def _layer(x, attn_norm, w_in, dil_out_norm, sb_out_norm, w_out, ffn_norm,
           w_router, b_router, w_gate, b_gate, w_up, b_up, w_down, b_down, *, tiles):
    b, seq, d = x.shape
    m = b * seq
    half_width = d // 2
    x2d = x.reshape(m, d)
    cos_t, sin_t = _rope_tables(seq)
    proj = _inproj(x2d, attn_norm.reshape(1, d), w_in, cos_t, sin_t,
                   seq=seq, tm=tiles["in_tm"], tn=tiles["in_tn"])
    proj3 = proj.reshape(b, seq, 3 * d)
    o_d = _dilated_attention(proj3, half_width=half_width).reshape(m, half_width)
    o_s = _stickbreak_attention(proj3, half_width=half_width, qt=tiles["sb_qt"]).reshape(m, half_width)
    x2, h2, idx_l, gate_l, rank_l, counts = _outproj_router(
        o_d, o_s, x2d, w_out, dil_out_norm.reshape(1, half_width), sb_out_norm.reshape(1, half_width),
        ffn_norm.reshape(1, d), w_router, b_router.reshape(1, N_EXPERTS),
        tm=tiles["out_tm"], tk=tiles["out_tk"])
    meta, row_tok, dest, n_rows = _routing_metadata(idx_l[:, :TOP_K], rank_l[:, :TOP_K], counts[0], m)
    xs = _dispatch(h2, row_tok, rows_blk=tiles["disp_rows"], n_chunk=d // 2 // LANES)
    ys = _experts(xs, w_gate, b_gate, w_up, b_up, w_down, b_down, meta, n_rows=n_rows, th=tiles["moe_th"])
    return x2, ys, dest, gate_l


_TILES = dict(in_tm=1024, in_tn=512, sb_qt=512, out_tm=512, out_tk=512, disp_rows=1024, moe_th=512, cmb_tm=128)


def kernel(x, attn_norm, w_in, dil_out_norm, sb_out_norm, w_out, ffn_norm, w_router, b_router,
           w_gate, b_gate, w_up, b_up, w_down, b_down, final_norm):
    depth = attn_norm.shape[0]
    b, seq, d = x.shape
    for l in range(depth):
        x2, ys, dest, gate_l = _layer(
            x, attn_norm[l], w_in[l], dil_out_norm[l], sb_out_norm[l], w_out[l], ffn_norm[l],
            w_router[l], b_router[l], w_gate[l], b_gate[l], w_up[l], b_up[l], w_down[l], b_down[l],
            tiles=_TILES)
        assert depth == 1
        x = _combine(dest, ys, x2, gate_l, final_norm.reshape(1, d), tm=_TILES["cmb_tm"])
    return x.reshape(b, seq, d)
```

```python
import functools
import math

import jax
import jax.numpy as jnp
from jax import lax
from jax.experimental import pallas as pl
from jax.experimental.pallas import tpu as pltpu

F32 = jnp.float32
BF16 = jnp.bfloat16

HEAD_DIM = 64
LANES = 128
BLOCK = 128
DIL_PATTERNS = ((128, 1), (512, 4), (2048, 16))
ROPE_THETA = 10000.0
N_EXPERTS = 32
TOP_K = 4
SWIGLU_LIMIT = 7.0
SWIGLU_ALPHA = 1.702
NORM_EPS = 1e-5
Q_SCALE = HEAD_DIM ** -0.5 * math.log2(math.e)
SB_UNROLL = 4
DIL_UNROLL = 8
ROW_ALIGN = 128
ROW_BLOCK = 512
ITEM_ROWS = 1280
VMEM_LIMIT = 56 * 1024 * 1024


def _cparams(sem):
    return pltpu.CompilerParams(dimension_semantics=sem, vmem_limit_bytes=VMEM_LIMIT)


def _inproj_kernel(x_ref, g_ref, w_ref, cos_ref, sin_ref, o_ref, h_scr, *, tn, half_width):
    j = pl.program_id(1)

    @pl.when(j == 0)
    def _():
        x = x_ref[...]
        y = x * lax.rsqrt(jnp.mean(x * x, axis=-1, keepdims=True) + NORM_EPS)
        h_scr[...] = (y * g_ref[...]).astype(BF16)

    acc = jnp.dot(h_scr[...], w_ref[...].astype(BF16), preferred_element_type=F32)
    section = (j * tn) // half_width
    scale = jnp.where((section == 0) | (section == 3), Q_SCALE, 1.0).astype(F32)

    @pl.when(section < 2)
    def _():
        lane = lax.broadcasted_iota(jnp.int32, (acc.shape[0], LANES), 1)
        first_half = (lane % HEAD_DIM) < (HEAD_DIM // 2)
        c = cos_ref[...]
        s = sin_ref[...]
        for g in range(tn // LANES):
            xg = acc[:, g * LANES:(g + 1) * LANES]
            partner = jnp.where(first_half,
                                pltpu.roll(xg, LANES - HEAD_DIM // 2, 1),
                                pltpu.roll(xg, HEAD_DIM // 2, 1))
            o_ref[:, g * LANES:(g + 1) * LANES] = ((xg * c + partner * s) * scale).astype(BF16)

    @pl.when(section >= 2)
    def _():
        o_ref[...] = (acc * scale).astype(BF16)


def _inproj(x2d, g, w_in, cos_t, sin_t, *, seq, tm, tn):
    m, d = x2d.shape
    n = w_in.shape[1]
    half_width = d // 2
    assert half_width % tn == 0 and tn % LANES == 0 and m % tm == 0 and seq % tm == 0
    pos_tiles = seq // tm
    return pl.pallas_call(
        functools.partial(_inproj_kernel, tn=tn, half_width=half_width),
        grid=(m // tm, n // tn),
        in_specs=[
            pl.BlockSpec((tm, d), lambda i, j: (i, 0)),
            pl.BlockSpec((1, d), lambda i, j: (0, 0)),
            pl.BlockSpec((d, tn), lambda i, j: (0, j)),
            pl.BlockSpec((tm, LANES), lambda i, j: (i % pos_tiles, 0)),
            pl.BlockSpec((tm, LANES), lambda i, j: (i % pos_tiles, 0)),
        ],
        out_specs=pl.BlockSpec((tm, tn), lambda i, j: (i, j)),
        out_shape=jax.ShapeDtypeStruct((m, n), BF16),
        scratch_shapes=[pltpu.VMEM((tm, d), BF16)],
        compiler_params=_cparams(("arbitrary", "arbitrary")),
        name="inproj",
    )(x2d, g, w_in, cos_t, sin_t)


def _rope_tables(seq):
    half = HEAD_DIM // 2
    inv_freq = 1.0 / (ROPE_THETA ** (jnp.arange(half, dtype=F32) * 2.0 / HEAD_DIM))
    ang = jnp.arange(seq, dtype=F32)[:, None] * inv_freq[None, :]
    cos = jnp.cos(ang)
    sin = jnp.sin(ang)
    reps = LANES // HEAD_DIM
    cos_t = jnp.tile(jnp.concatenate([cos, cos], axis=1), (1, reps))
    sin_t = jnp.tile(jnp.concatenate([-sin, sin], axis=1), (1, reps))
    return cos_t, sin_t


def _dil_kernel(q_ref, k_ref, v_ref, o_ref, f32_scr, qp, kp, vp, operm, lperm, onat, lnat, *, seq):
    n_blocks = seq // BLOCK
    lane_q = lax.broadcasted_iota(jnp.int32, (BLOCK, LANES), 1)
    head_a_q = lane_q < HEAD_DIM
    zero_b = jnp.zeros((), BF16)

    def window_consts(n_kb):
        width = n_kb * BLOCK
        head_a_k = lax.broadcasted_iota(jnp.int32, (width, LANES), 1) < HEAD_DIM
        qi = lax.broadcasted_iota(jnp.int32, (BLOCK, width), 0)
        ki = lax.broadcasted_iota(jnp.int32, (BLOCK, width), 1)
        dist = qi + (width - BLOCK) - ki
        band = (dist >= 0) & (dist <= BLOCK)
        cur_only = ki >= width - BLOCK
        head_ones = jnp.concatenate([jnp.where(head_a_k, 1.0, 0.0), jnp.where(head_a_k, 0.0, 1.0)],
                                    axis=0).astype(BF16)
        return width, head_a_k, band, cur_only, head_ones

    f32_scr[0] = q_ref[0].astype(F32)
    f32_scr[1] = k_ref[0].astype(F32)
    f32_scr[2] = v_ref[0].astype(F32)
    kp[0:BLOCK, :] = jnp.zeros((BLOCK, LANES), BF16)
    vp[0:BLOCK, :] = jnp.zeros((BLOCK, LANES), BF16)

    for p, (window, dil) in enumerate(DIL_PATTERNS):
        assert window // dil == BLOCK
        length = seq // dil
        per_stream = length // BLOCK
        assert length % BLOCK == 0
        if dil == 1:
            qp[...] = q_ref[0]
            kp[BLOCK:, :] = k_ref[0]
            vp[BLOCK:, :] = v_ref[0]
        else:
            for r in range(dil):
                rows = pl.ds(r, length, stride=dil)
                qp[r * length:(r + 1) * length, :] = f32_scr[0, rows, :].astype(BF16)
                kp[BLOCK + r * length:BLOCK + (r + 1) * length, :] = f32_scr[1, rows, :].astype(BF16)
                vp[BLOCK + r * length:BLOCK + (r + 1) * length, :] = f32_scr[2, rows, :].astype(BF16)
        o_dst = onat.at[p] if dil == 1 else operm
        l_dst = lnat.at[p] if dil == 1 else lperm

        consts = window_consts(1 if per_stream == 1 else 2)

        def block_body(bi, carry, per_stream=per_stream, o_dst=o_dst, l_dst=l_dst, consts=consts):
            width, head_a_k, band, cur_only, head_ones = consts
            row0 = pl.multiple_of(bi * BLOCK, BLOCK)
            qb = qp[pl.ds(row0, BLOCK), :]
            win0 = pl.multiple_of(row0 + 2 * BLOCK - width, BLOCK)
            kw = kp[pl.ds(win0, width), :]
            vw = vp[pl.ds(win0, width), :]
            has_prev = (bi % per_stream) != 0
            valid = band & (cur_only | has_prev)
            kk = jnp.concatenate([jnp.where(head_a_k, kw, zero_b), jnp.where(head_a_k, zero_b, kw)], axis=0)
            s2 = lax.dot_general(qb, kk, (((1,), (1,)), ((), ())), preferred_element_type=F32)
            ps, ms = [], []
            for h in range(2):
                s = jnp.where(valid, s2[:, h * width:(h + 1) * width], -jnp.inf)
                m = jnp.max(s, axis=-1, keepdims=True)
                ps.append(jnp.exp2(s - m).astype(BF16))
                ms.append(m)
            vv = jnp.concatenate([jnp.where(head_a_k, vw, zero_b), jnp.where(head_a_k, zero_b, vw)], axis=0)
            o_den = jnp.dot(jnp.concatenate(ps, axis=1), jnp.concatenate([vv, head_ones], axis=1),
                            preferred_element_type=F32)
            o = o_den[:, :LANES]
            den = o_den[:, LANES:]
            o_dst[pl.ds(row0, BLOCK), :] = o / den
            l_dst[pl.ds(row0, BLOCK), :] = jnp.where(head_a_q, ms[0], ms[1]) + jnp.log2(den)
            return carry

        lax.fori_loop(0, n_blocks, block_body, 0, unroll=DIL_UNROLL)
        if dil != 1:
            for r in range(dil):
                rows = pl.ds(r, length, stride=dil)
                onat[p, rows, :] = operm[r * length:(r + 1) * length, :]
                lnat[p, rows, :] = lperm[r * length:(r + 1) * length, :]

    def mix_body(c, carry):
        rows = pl.ds(pl.multiple_of(c * BLOCK, BLOCK), BLOCK)
        ls = [lnat[p, rows, :] for p in range(len(DIL_PATTERNS))]
        m = functools.reduce(jnp.maximum, ls)
        es = [jnp.exp2(l - m) for l in ls]
        num = functools.reduce(lambda a, b: a + b, [e * onat[p, rows, :] for p, e in enumerate(es)])
        o_ref[0, rows, :] = num / functools.reduce(lambda a, b: a + b, es)
        return carry

    lax.fori_loop(0, n_blocks, mix_body, 0)


def _dilated_attention(proj3, *, half_width):
    b, seq, _ = proj3.shape
    pairs = half_width // LANES
    n_pat = len(DIL_PATTERNS)
    spec = lambda off: pl.BlockSpec((1, seq, LANES), lambda bi, c, off=off: (bi, 0, off + c))
    return pl.pallas_call(
        functools.partial(_dil_kernel, seq=seq),
        grid=(b, pairs),
        in_specs=[spec(0), spec(pairs), spec(2 * pairs)],
        out_specs=pl.BlockSpec((1, seq, LANES), lambda bi, c: (bi, 0, c)),
        out_shape=jax.ShapeDtypeStruct((b, seq, half_width), F32),
        scratch_shapes=[
            pltpu.VMEM((3, seq, LANES), F32),
            pltpu.VMEM((seq, LANES), BF16),
            pltpu.VMEM((seq + BLOCK, LANES), BF16),
            pltpu.VMEM((seq + BLOCK, LANES), BF16),
            pltpu.VMEM((seq, LANES), F32),
            pltpu.VMEM((seq, LANES), F32),
            pltpu.VMEM((n_pat, seq, LANES), F32),
            pltpu.VMEM((n_pat, seq, LANES), F32),
        ],
        compiler_params=_cparams(("arbitrary", "arbitrary")),
        name="dilated_attention",
    )(proj3, proj3, proj3)


def _sb_kernel(q_ref, k_ref, v_ref, o_ref, kk_scr, vv_scr, tri_scr, acc_scr, carry_scr, *, seq, qt):
    n_blocks = seq // BLOCK
    per_tile = qt // BLOCK
    lane_k = lax.broadcasted_iota(jnp.int32, (BLOCK, LANES), 1)
    head_a = lane_k < HEAD_DIM
    zero_b = jnp.zeros((), BF16)

    def prep(jb, carry):
        rows = pl.ds(pl.multiple_of(jb * BLOCK, BLOCK), BLOCK)
        kb = k_ref[0, rows, :]
        vb = v_ref[0, rows, :]
        kk_scr[jb, 0:BLOCK, :] = jnp.where(head_a, kb, zero_b)
        kk_scr[jb, BLOCK:, :] = jnp.where(head_a, zero_b, kb)
        vv_scr[jb, 0:BLOCK, :] = jnp.where(head_a, vb, zero_b)
        vv_scr[jb, BLOCK:, :] = jnp.where(head_a, zero_b, vb)
        return carry

    lax.fori_loop(0, n_blocks, prep, 0)

    jj = lax.broadcasted_iota(jnp.int32, (2 * BLOCK, 2 * BLOCK), 0) % BLOCK
    ss = lax.broadcasted_iota(jnp.int32, (2 * BLOCK, 2 * BLOCK), 1)
    tri_scr[...] = jnp.where((ss >= BLOCK) | (jj > ss), 1.0, 0.0).astype(BF16)

    def key_block(qb, jb, key_off):
        rs = slice(0 if key_off is None else key_off, qt)
        z = lax.dot_general(qb[rs], kk_scr[jb], (((1,), (1,)), ((), ())), preferred_element_type=F32)
        neg_abs = lax.bitcast_convert_type(lax.bitcast_convert_type(z, jnp.uint32) | jnp.uint32(0x80000000), F32)
        ls = jnp.minimum(z, 0.0) - jnp.log2(1.0 + jnp.exp2(neg_abs))
        lr = ls - z
        if key_off is not None:
            n = qt - key_off
            causal = (lax.broadcasted_iota(jnp.int32, (n, 2 * BLOCK), 1) % BLOCK
                      < lax.broadcasted_iota(jnp.int32, (n, 2 * BLOCK), 0))
            lr = jnp.where(causal, lr, 0.0)
        hi = lr.astype(BF16)
        lo = (lr - hi.astype(F32)).astype(BF16)
        tri = tri_scr[...]
        a_heads = []
        for h in range(2):
            sl = slice(h * BLOCK, (h + 1) * BLOCK)
            lhs = jnp.concatenate([hi[:, sl], lo[:, sl]], axis=1)
            cs_tot = jnp.dot(lhs, tri, preferred_element_type=F32)
            carry = carry_scr[rs, sl]
            a = jnp.exp2(ls[:, sl] + cs_tot[:, :BLOCK] + carry)
            if key_off is not None:
                a = jnp.where(causal[:, :BLOCK], a, 0.0)
            carry_scr[rs, sl] = carry + cs_tot[:, BLOCK:]
            a_heads.append(a.astype(BF16))
        acc_scr[rs, :] += jnp.dot(jnp.concatenate(a_heads, axis=1), vv_scr[jb], preferred_element_type=F32)

    def q_tile(i, carry):
        rows = pl.ds(pl.multiple_of(i * qt, qt), qt)
        qb = q_ref[0, rows, :]
        acc_scr[...] = jnp.zeros_like(acc_scr)
        carry_scr[...] = jnp.zeros_like(carry_scr)
        for t in reversed(range(per_tile)):
            key_block(qb, i * per_tile + t, t * BLOCK)

        def inner(t, c):
            for s in range(SB_UNROLL):
                key_block(qb, i * per_tile - 1 - (t * SB_UNROLL + s), None)
            return c

        lax.fori_loop(0, i * (per_tile // SB_UNROLL), inner, 0)
        o_ref[0, rows, :] = acc_scr[...]
        return carry

    lax.fori_loop(0, seq // qt, q_tile, 0)


def _stickbreak_attention(proj3, *, half_width, qt):
    b, seq, _ = proj3.shape
    pairs = half_width // LANES
    n_blocks = seq // BLOCK
    assert seq % qt == 0 and qt % BLOCK == 0
    spec = lambda off: pl.BlockSpec((1, seq, LANES), lambda bi, c, off=off: (bi, 0, off + c))
    return pl.pallas_call(
        functools.partial(_sb_kernel, seq=seq, qt=qt),
        grid=(b, pairs),
        in_specs=[spec(3 * pairs), spec(4 * pairs), spec(5 * pairs)],
        out_specs=pl.BlockSpec((1, seq, LANES), lambda bi, c: (bi, 0, c)),
        out_shape=jax.ShapeDtypeStruct((b, seq, half_width), F32),
        scratch_shapes=[
            pltpu.VMEM((n_blocks, 2 * BLOCK, LANES), BF16),
            pltpu.VMEM((n_blocks, 2 * BLOCK, LANES), BF16),
            pltpu.VMEM((2 * BLOCK, 2 * BLOCK), BF16),
            pltpu.VMEM((qt, LANES), F32),
            pltpu.VMEM((qt, 2 * BLOCK), F32),
        ],
        compiler_params=_cparams(("arbitrary", "arbitrary")),
        name="stickbreak_attention",
    )(proj3, proj3, proj3)


def _split_bf16(x):
    hi = x.astype(BF16)
    return hi, (x - hi.astype(F32)).astype(BF16)


def _outproj_kernel(od_ref, os_ref, x_ref, w_ref, gd_ref, gs_ref, gf_ref, wr_ref, br_ref,
                    x2_ref, h2_ref, idx_ref, gate_ref, rank_ref, cnt_ref,
                    acc_d, acc_s, ssq_d, ssq_s, cnt_scr, *, nk, half_width):
    k = pl.program_id(1)

    def accumulate(a_ref, g_ref, acc, ssq, first):
        a = a_ref[...]
        sq = jnp.sum(a * a, axis=-1, keepdims=True)
        prod = jnp.dot((a * g_ref[...]).astype(BF16), w_ref[...].astype(BF16), preferred_element_type=F32)
        if first:
            ssq[...] = sq
            acc[...] = prod
        else:
            ssq[...] += sq
            acc[...] += prod

    for kk in range(nk):
        @pl.when(k == kk)
        def _(kk=kk):
            if kk < nk // 2:
                accumulate(od_ref, gd_ref, acc_d, ssq_d, kk == 0)
            else:
                accumulate(os_ref, gs_ref, acc_s, ssq_s, kk == nk // 2)

    @pl.when(k == nk - 1)
    def _():
        r_d = lax.rsqrt(ssq_d[...] / half_width + NORM_EPS)
        r_s = lax.rsqrt(ssq_s[...] / half_width + NORM_EPS)
        x2 = x_ref[...] + (acc_d[...] * r_d + acc_s[...] * r_s)
        x2_ref[...] = x2
        h2 = (x2 * lax.rsqrt(jnp.mean(x2 * x2, axis=-1, keepdims=True) + NORM_EPS)) * gf_ref[...]
        packed = _pack_pairs(h2)
        n_chunk = packed.shape[1] // LANES
        for ch in range(n_chunk):
            h2_ref[pl.ds(ch, packed.shape[0], stride=n_chunk), :] = packed[:, ch * LANES:(ch + 1) * LANES]
        h_hi, h_lo = _split_bf16(h2)
        w_hi, w_lo = _split_bf16(wr_ref[...])
        hw = jnp.dot(h_hi, jnp.concatenate([w_hi, w_lo], axis=1), preferred_element_type=F32)
        logits = (hw[:, :N_EXPERTS] + hw[:, N_EXPERTS:]
                  + jnp.dot(h_lo, w_hi, preferred_element_type=F32)) + br_ref[...]
        tm = logits.shape[0]
        e_iota = lax.broadcasted_iota(jnp.int32, logits.shape, 1)
        out_lane = lax.broadcasted_iota(jnp.int32, (tm, LANES), 1)
        idx_out = jnp.zeros((tm, LANES), jnp.int32)
        val_out = jnp.full((tm, LANES), -jnp.inf, F32)
        top0 = None
        picks = []
        for kk in range(TOP_K):
            m = jnp.max(logits, axis=-1, keepdims=True)
            sel = jnp.min(jnp.where(logits == m, e_iota, N_EXPERTS), axis=-1, keepdims=True)
            idx_out = jnp.where(out_lane == kk, sel, idx_out)
            val_out = jnp.where(out_lane == kk, m, val_out)
            picks.append(e_iota == sel)
            logits = jnp.where(picks[-1], -jnp.inf, logits)
            if kk == 0:
                top0 = m
        ex = jnp.exp(val_out - top0)
        gate_ref[...] = ex / jnp.sum(ex, axis=-1, keepdims=True)
        idx_ref[...] = idx_out

        @pl.when(pl.program_id(0) == 0)
        def _():
            cnt_scr[...] = jnp.zeros_like(cnt_scr)

        chosen = functools.reduce(jnp.logical_or, picks)
        t_row = lax.broadcasted_iota(jnp.int32, (tm, tm), 0)
        t_col = lax.broadcasted_iota(jnp.int32, (tm, tm), 1)
        before = jnp.where(t_col < t_row, 1.0, 0.0).astype(BF16)
        multi_hot = jnp.where(chosen, 1.0, 0.0)
        prior = jnp.dot(before, multi_hot.astype(BF16), preferred_element_type=F32) + cnt_scr[...]
        rank_out = jnp.zeros((tm, LANES), jnp.int32)
        for kk in range(TOP_K):
            r = jnp.sum(jnp.where(picks[kk], prior, 0.0), axis=-1, keepdims=True)
            rank_out = jnp.where(out_lane == kk, r.astype(jnp.int32), rank_out)
        rank_ref[...] = rank_out
        cnt_scr[...] += jnp.sum(multi_hot, axis=0, keepdims=True)
        cnt_ref[...] = cnt_scr[...].astype(jnp.int32)


def _outproj_router(o_d, o_s, x2d, w_out, g_d, g_s, g_f, w_router, b_router, *, tm, tk):
    m, d = x2d.shape
    half_width = d // 2
    nk = d // tk
    assert half_width % tk == 0 and m % tm == 0
    kh = nk // 2
    return pl.pallas_call(
        functools.partial(_outproj_kernel, nk=nk, half_width=half_width),
        grid=(m // tm, nk),
        in_specs=[
            pl.BlockSpec((tm, tk), lambda i, k: (i, jnp.minimum(k, kh - 1))),
            pl.BlockSpec((tm, tk), lambda i, k: (i, jnp.maximum(k - kh, 0))),
            pl.BlockSpec((tm, d), lambda i, k: (i, 0)),
            pl.BlockSpec((tk, d), lambda i, k: (k, 0)),
            pl.BlockSpec((1, tk), lambda i, k: (0, jnp.minimum(k, kh - 1))),
            pl.BlockSpec((1, tk), lambda i, k: (0, jnp.maximum(k - kh, 0))),
            pl.BlockSpec((1, d), lambda i, k: (0, 0)),
            pl.BlockSpec((d, N_EXPERTS), lambda i, k: (0, 0)),
            pl.BlockSpec((1, N_EXPERTS), lambda i, k: (0, 0)),
        ],
        out_specs=[
            pl.BlockSpec((tm, d), lambda i, k: (i, 0)),
            pl.BlockSpec((tm * (d // 2 // LANES), LANES), lambda i, k: (i, 0)),
            pl.BlockSpec((tm, LANES), lambda i, k: (i, 0)),
            pl.BlockSpec((tm, LANES), lambda i, k: (i, 0)),
            pl.BlockSpec((tm, LANES), lambda i, k: (i, 0)),
            pl.BlockSpec((1, N_EXPERTS), lambda i, k: (0, 0)),
        ],
        out_shape=[
            jax.ShapeDtypeStruct((m, d), F32),
            jax.ShapeDtypeStruct((m * (d // 2 // LANES), LANES), jnp.uint32),
            jax.ShapeDtypeStruct((m, LANES), jnp.int32),
            jax.ShapeDtypeStruct((m, LANES), F32),
            jax.ShapeDtypeStruct((m, LANES), jnp.int32),
            jax.ShapeDtypeStruct((1, N_EXPERTS), jnp.int32),
        ],
        scratch_shapes=[
            pltpu.VMEM((tm, d), F32),
            pltpu.VMEM((tm, d), F32),
            pltpu.VMEM((tm, 1), F32),
            pltpu.VMEM((tm, 1), F32),
            pltpu.VMEM((1, N_EXPERTS), F32),
        ],
        compiler_params=_cparams(("arbitrary", "arbitrary")),
        name="outproj_router",
    )(o_d, o_s, x2d, w_out, g_d, g_s, g_f, w_router, b_router)


def _unpack_pairs(words):
    hi = lax.bitcast_convert_type(words & jnp.uint32(0xFFFF0000), F32)
    lo = lax.bitcast_convert_type(words << 16, F32)
    return hi.astype(BF16), lo.astype(BF16)


def _pack_pairs(x):
    c = x.shape[1] // 2
    as_bits = lambda v: lax.bitcast_convert_type(v.astype(BF16).astype(F32), jnp.uint32)
    return as_bits(x[:, :c]) | (as_bits(x[:, c:]) >> 16)


def _dispatch_kernel(row_tok, h_hbm, o_ref, h_vmem, sem, *, rows_blk, n_chunk):
    i = pl.program_id(0)

    @pl.when(i == 0)
    def _():
        cp = pltpu.make_async_copy(h_hbm, h_vmem, sem)
        cp.start()
        cp.wait()

    base = i * rows_blk

    def body(r, c):
        tok = row_tok[base + r]
        src = pl.ds(pl.multiple_of(tok * n_chunk, n_chunk), n_chunk)
        o_ref[pl.ds(pl.multiple_of(r * n_chunk, n_chunk), n_chunk), :] = h_vmem[src, :]
        return c

    lax.fori_loop(0, rows_blk, body, 0, unroll=8)


def _dispatch(hp, row_tok, *, rows_blk, n_chunk):
    n_rows = row_tok.shape[0]
    assert n_rows % rows_blk == 0
    grid_spec = pltpu.PrefetchScalarGridSpec(
        num_scalar_prefetch=1,
        grid=(n_rows // rows_blk,),
        in_specs=[pl.BlockSpec(memory_space=pl.ANY)],
        out_specs=pl.BlockSpec((rows_blk * n_chunk, LANES), lambda i, row_tok: (i, 0)),
        scratch_shapes=[pltpu.VMEM(hp.shape, hp.dtype), pltpu.SemaphoreType.DMA],
    )
    return pl.pallas_call(
        functools.partial(_dispatch_kernel, rows_blk=rows_blk, n_chunk=n_chunk),
        grid_spec=grid_spec,
        out_shape=jax.ShapeDtypeStruct((n_rows * n_chunk, LANES), hp.dtype),
        compiler_params=_cparams(("arbitrary",)),
        name="dispatch",
    )(row_tok, hp)


def _expert_kernel(item_e, item_r0, item_nb, n_items,
                   xs_hbm, wg_ref, bg_ref, wu_ref, bu_ref, wd_ref, bd_ref,
                   ys_hbm, stage, xb16, acc, pending, sem_in, sem_out, *, nj):
    i = pl.program_id(0)
    j = pl.program_id(1)
    n_live = n_items[0]
    active = i < n_live
    r0 = item_r0[i]
    n_align = item_nb[i]
    n_rows = n_align * ROW_ALIGN
    per_trip = ROW_BLOCK // ROW_ALIGN
    n_main = n_align // per_trip
    nxt = jnp.minimum(i + 1, pl.num_programs(0) - 1)
    r0_next = item_r0[nxt]
    blocks_next = jnp.where(i + 1 < n_live, item_nb[nxt], 0)
    d = acc.shape[1]
    n_chunk = d // 2 // LANES
    blk_words = ROW_ALIGN * n_chunk

    def in_copy(base_row, blk):
        src = xs_hbm.at[pl.ds(pl.multiple_of((base_row + blk * ROW_ALIGN) * n_chunk, blk_words), blk_words)]
        return pltpu.make_async_copy(src, stage.at[pl.ds(pl.multiple_of(blk * blk_words, blk_words), blk_words)],
                                     sem_in)

    def start_in(base_row, n_blk):
        def issue(blk, c):
            in_copy(base_row, blk).start()
            return c

        lax.fori_loop(0, n_blk, issue, 0)

    def out_copy(blk):
        src = acc.at[pl.ds(pl.multiple_of(blk * ROW_ALIGN, ROW_ALIGN), ROW_ALIGN)]
        dst = ys_hbm.at[pl.ds(pl.multiple_of(r0 + blk * ROW_ALIGN, ROW_ALIGN), ROW_ALIGN)]
        return pltpu.make_async_copy(src, dst, sem_out)

    def drain():
        @pl.when(pending[0] > 0)
        def _():
            rows = pl.ds(0, pending[0] * ROW_ALIGN)
            pltpu.make_async_copy(acc.at[rows], ys_hbm.at[rows], sem_out).wait()
            pending[0] = 0

    @pl.when((i == 0) & (j == 0))
    def _():
        pending[0] = 0
        start_in(r0, n_align)

    @pl.when(active & (j == 0))
    def _():
        all_rows = pl.ds(0, n_rows * n_chunk)
        pltpu.make_async_copy(xs_hbm.at[all_rows], stage.at[all_rows], sem_in).wait()
        drain()

        def unpack(rb, c):
            rows = pl.ds(pl.multiple_of(rb * ROW_ALIGN, ROW_ALIGN), ROW_ALIGN)
            for ch in range(n_chunk):
                words = stage[pl.ds(rb * (ROW_ALIGN * n_chunk) + ch, ROW_ALIGN, stride=n_chunk), :]
                hi, lo = _unpack_pairs(words)
                xb16[rows, ch * LANES:(ch + 1) * LANES] = hi
                xb16[rows, d // 2 + ch * LANES:d // 2 + (ch + 1) * LANES] = lo
            acc[rows, :] = jnp.broadcast_to(bd_ref[0], (ROW_ALIGN, d))
            return c

        lax.fori_loop(0, n_align, unpack, 0)
        start_in(r0_next, blocks_next)

    def trip(row0, n_blk):
        rows = pl.ds(row0, n_blk * ROW_ALIGN)
        xb = xb16[rows, :]
        g = jnp.dot(xb, wg_ref[0].astype(BF16), preferred_element_type=F32) + bg_ref[0]
        u = jnp.dot(xb, wu_ref[0].astype(BF16), preferred_element_type=F32) + bu_ref[0]
        g = jnp.minimum(g, SWIGLU_LIMIT)
        u = jnp.clip(u, -SWIGLU_LIMIT, SWIGLU_LIMIT)
        a = (u + 1.0) * (g * jax.nn.sigmoid(SWIGLU_ALPHA * g))
        acc[rows, :] += jnp.dot(a.astype(BF16), wd_ref[0].astype(BF16), preferred_element_type=F32)

        @pl.when(j == nj - 1)
        def _():
            for s in range(n_blk):
                out_copy(row0 // ROW_ALIGN + s).start()
            pending[0] = pending[0] + n_blk

    @pl.when(active)
    def _():
        def main_trip(rb, c):
            trip(pl.multiple_of(rb * ROW_BLOCK, ROW_BLOCK), per_trip)
            return c

        lax.fori_loop(0, n_main, main_trip, 0)

        def tail_trip(t, c):
            trip(pl.multiple_of((n_main * per_trip + t) * ROW_ALIGN, ROW_ALIGN), 1)
            return c

        lax.fori_loop(0, n_align - n_main * per_trip, tail_trip, 0)

    @pl.when((i == pl.num_programs(0) - 1) & (j == nj - 1))
    def _():
        drain()
        acc[0:ROW_ALIGN, :] = jnp.zeros((ROW_ALIGN, d), F32)
        used = n_items[1]

        def tail_copy(t):
            dst = ys_hbm.at[pl.ds(pl.multiple_of((used + t) * ROW_ALIGN, ROW_ALIGN), ROW_ALIGN)]
            return pltpu.make_async_copy(acc.at[pl.ds(0, ROW_ALIGN)], dst, sem_out)

        def issue(t, c):
            tail_copy(t).start()
            return c

        n_tail = ys_hbm.shape[0] // ROW_ALIGN - used
        lax.fori_loop(0, n_tail, issue, 0)

        def wait(t, c):
            tail_copy(t).wait()
            return c

        lax.fori_loop(0, n_tail, wait, 0)


def _experts(xs, w_gate, b_gate, w_up, b_up, w_down, b_down, meta, *, n_rows, th):
    item_e, item_r0, item_nb, n_items = meta
    n_exp, d, dh = w_gate.shape
    nj = dh // th
    n_item_slots = item_e.shape[0]

    def w_col(i, j, item_e, item_r0, item_nb, n_items):
        return (item_e[i], 0, jnp.where(i < n_items[0], j, nj - 1))

    def w_row(i, j, item_e, item_r0, item_nb, n_items):
        return (item_e[i], jnp.where(i < n_items[0], j, nj - 1), 0)

    def b_all(i, j, item_e, item_r0, item_nb, n_items):
        return (item_e[i], 0, 0)

    grid_spec = pltpu.PrefetchScalarGridSpec(
        num_scalar_prefetch=4,
        grid=(n_item_slots, nj),
        in_specs=[
            pl.BlockSpec(memory_space=pl.ANY),
            pl.BlockSpec((1, d, th), w_col),
            pl.BlockSpec((1, 1, th), w_col),
            pl.BlockSpec((1, d, th), w_col),
            pl.BlockSpec((1, 1, th), w_col),
            pl.BlockSpec((1, th, d), w_row),
            pl.BlockSpec((1, 1, d), b_all),
        ],
        out_specs=pl.BlockSpec(memory_space=pl.ANY),
        scratch_shapes=[
            pltpu.VMEM((ITEM_ROWS * (d // 2 // LANES), LANES), jnp.uint32),
            pltpu.VMEM((ITEM_ROWS, d), BF16),
            pltpu.VMEM((ITEM_ROWS, d), F32),
            pltpu.SMEM((1,), jnp.int32),
            pltpu.SemaphoreType.DMA,
            pltpu.SemaphoreType.DMA,
        ],
    )
    return pl.pallas_call(
        functools.partial(_expert_kernel, nj=nj),
        grid_spec=grid_spec,
        out_shape=jax.ShapeDtypeStruct((n_rows, d), F32),
        compiler_params=_cparams(("arbitrary", "arbitrary")),
        name="experts",
    )(item_e, item_r0, item_nb, n_items,
      xs, w_gate, b_gate.reshape(n_exp, 1, dh), w_up, b_up.reshape(n_exp, 1, dh), w_down,
      b_down.reshape(n_exp, 1, d))


def _routing_metadata(top_idx, top_rank, counts, n_tok):
    n_assign = n_tok * TOP_K
    flat_e = top_idx.reshape(n_assign)
    rank = top_rank.reshape(n_assign)
    padded = (counts + ROW_ALIGN - 1) // ROW_ALIGN * ROW_ALIGN
    pend = jnp.cumsum(padded)
    pstart = pend - padded
    dest = (pstart[flat_e] + rank).astype(jnp.int32)
    n_rows = (n_assign // ROW_ALIGN + N_EXPERTS) * ROW_ALIGN
    flat_tok = jnp.arange(n_assign, dtype=jnp.int32) // TOP_K
    row_tok = jnp.zeros((n_rows,), jnp.int32).at[dest].set(
        flat_tok, unique_indices=True, mode="promise_in_bounds")
    n_chunks = (padded + ITEM_ROWS - 1) // ITEM_ROWS
    cend = jnp.cumsum(n_chunks)
    cstart = cend - n_chunks
    n_slots = N_EXPERTS + n_assign // ITEM_ROWS + 1
    slot = jnp.arange(n_slots, dtype=jnp.int32)
    n_items = cend[-1]
    last_e = jnp.max(jnp.where(n_chunks > 0, jnp.arange(N_EXPERTS), 0))
    e_of = jnp.minimum(jnp.searchsorted(cend, slot, side='right'), N_EXPERTS - 1)
    live = slot < n_items
    item_e = jnp.where(live, e_of, last_e).astype(jnp.int32)
    chunk = slot - cstart[e_of]
    item_r0 = jnp.where(live, pstart[e_of] + chunk * ITEM_ROWS, 0).astype(jnp.int32)
    item_rows = jnp.clip(padded[e_of] - chunk * ITEM_ROWS, 0, ITEM_ROWS)
    item_nb = jnp.where(live, item_rows // ROW_ALIGN, 0).astype(jnp.int32)
    counts_blk = jnp.stack([n_items, pend[-1] // ROW_ALIGN]).astype(jnp.int32)
    meta = (item_e, item_r0, item_nb, counts_blk)
    return meta, row_tok, dest, n_rows


def _combine_kernel(dest, ys_hbm, x2_ref, gate_ref, g_ref, o_ref, buf, sem, *, tm):
    i = pl.program_id(0)
    n = pl.num_programs(0)

    def row_copy(tile, slot, t, kk):
        r = dest[(tile * tm + t) * TOP_K + kk]
        return pltpu.make_async_copy(ys_hbm.at[pl.ds(r, 1)], buf.at[slot, kk, pl.ds(t, 1)], sem.at[slot])

    def issue_tile(tile, slot):
        def issue(t, c):
            for kk in range(TOP_K):
                row_copy(tile, slot, t, kk).start()
            return c

        lax.fori_loop(0, tm, issue, 0, unroll=8)

    slot = i % 2

    @pl.when(i == 0)
    def _():
        issue_tile(0, 0)

    @pl.when(i + 1 < n)
    def _():
        issue_tile(i + 1, 1 - slot)

    for kk in range(TOP_K):
        pltpu.make_async_copy(ys_hbm.at[pl.ds(0, tm)], buf.at[slot, kk], sem.at[slot]).wait()

    gates = gate_ref[...]
    y = x2_ref[...]
    for kk in range(TOP_K):
        y = y + gates[:, kk:kk + 1] * buf[slot, kk]
    o_ref[...] = (y * lax.rsqrt(jnp.mean(y * y, axis=-1, keepdims=True) + NORM_EPS)) * g_ref[...]


def _combine(dest, ys, x2, gates, g_final, *, tm):
    m, d = x2.shape
    grid_spec = pltpu.PrefetchScalarGridSpec(
        num_scalar_prefetch=1,
        grid=(m // tm,),
        in_specs=[
            pl.BlockSpec(memory_space=pl.ANY),
            pl.BlockSpec((tm, d), lambda i, dest: (i, 0)),
            pl.BlockSpec((tm, LANES), lambda i, dest: (i, 0)),
            pl.BlockSpec((1, d), lambda i, dest: (0, 0)),
        ],
        out_specs=pl.BlockSpec((tm, d), lambda i, dest: (i, 0)),
        scratch_shapes=[pltpu.VMEM((2, TOP_K, tm, d), F32), pltpu.SemaphoreType.DMA((2,))],
    )
    return pl.pallas_call(
        functools.partial(_combine_kernel, tm=tm),
        grid_spec=grid_spec,
        out_shape=jax.ShapeDtypeStruct((m, d), F32),
        compiler_params=_cparams(("arbitrary",)),
        name="combine",
    )(dest, ys, x2, gates, g_final)


def _layer(x, attn_norm, w_in, dil_out_norm, sb_out_norm, w_out, ffn_norm,
           w_router, b_router, w_gate, b_gate, w_up, b_up, w_down, b_down, *, tiles):
    b, seq, d = x.shape
    m = b * seq
    half_width = d // 2
    x2d = x.reshape(m, d)
    cos_t, sin_t = _rope_tables(seq)
    proj = _inproj(x2d, attn_norm.reshape(1, d), w_in, cos_t, sin_t,
                   seq=seq, tm=tiles["in_tm"], tn=tiles["in_tn"])
    proj3 = proj.reshape(b, seq, 3 * d)
    o_d = _dilated_attention(proj3, half_width=half_width).reshape(m, half_width)
    o_s = _stickbreak_attention(proj3, half_width=half_width, qt=tiles["sb_qt"]).reshape(m, half_width)
    x2, h2, idx_l, gate_l, rank_l, counts = _outproj_router(
        o_d, o_s, x2d, w_out, dil_out_norm.reshape(1, half_width), sb_out_norm.reshape(1, half_width),
        ffn_norm.reshape(1, d), w_router, b_router.reshape(1, N_EXPERTS),
        tm=tiles["out_tm"], tk=tiles["out_tk"])
    meta, row_tok, dest, n_rows = _routing_metadata(idx_l[:, :TOP_K], rank_l[:, :TOP_K], counts[0], m)
    xs = _dispatch(h2, row_tok, rows_blk=tiles["disp_rows"], n_chunk=d // 2 // LANES)
    ys = _experts(xs, w_gate, b_gate, w_up, b_up, w_down, b_down, meta, n_rows=n_rows, th=tiles["moe_th"])
    return x2, ys, dest, gate_l


_TILES = dict(in_tm=1024, in_tn=512, sb_qt=512, out_tm=512, out_tk=512, disp_rows=1024, moe_th=512, cmb_tm=128)


def kernel(x, attn_norm, w_in, dil_out_norm, sb_out_norm, w_out, ffn_norm, w_router, b_router,
           w_gate, b_gate, w_up, b_up, w_down, b_down, final_norm):
    depth = attn_norm.shape[0]
    b, seq, d = x.shape
    for l in range(depth):
        x2, ys, dest, gate_l = _layer(
            x, attn_norm[l], w_in[l], dil_out_norm[l], sb_out_norm[l], w_out[l], ffn_norm[l],
            w_router[l], b_router[l], w_gate[l], b_gate[l], w_up[l], b_up[l], w_down[l], b_down[l],
            tiles=_TILES)
        assert depth == 1
        x = _combine(dest, ys, x2, gate_l, final_norm.reshape(1, d), tm=_TILES["cmb_tm"])
    return x.reshape(b, seq, d)
```

```python
import functools
import math

import jax
import jax.numpy as jnp
from jax import lax
from jax.experimental import pallas as pl
from jax.experimental.pallas import tpu as pltpu

F32 = jnp.float32
BF16 = jnp.bfloat16

HEAD_DIM = 64
LANES = 128
BLOCK = 128
DIL_PATTERNS = ((128, 1), (512, 4), (2048, 16))
ROPE_THETA = 10000.0
N_EXPERTS = 32
TOP_K = 4
SWIGLU_LIMIT = 7.0
SWIGLU_ALPHA = 1.702
NORM_EPS = 1e-5
Q_SCALE = HEAD_DIM ** -0.5 * math.log2(math.e)
SB_UNROLL = 4
DIL_UNROLL = 16
ROW_ALIGN = 128
ROW_BLOCK = 512
ITEM_ROWS = 1280
VMEM_LIMIT = 56 * 1024 * 1024


def _cparams(sem):
    return pltpu.CompilerParams(dimension_semantics=sem, vmem_limit_bytes=VMEM_LIMIT)


def _inproj_kernel(x_ref, g_ref, w_ref, cos_ref, sin_ref, o_ref, h_scr, *, tn, half_width):
    j = pl.program_id(1)

    @pl.when(j == 0)
    def _():
        x = x_ref[...]
        y = x * lax.rsqrt(jnp.mean(x * x, axis=-1, keepdims=True) + NORM_EPS)
        h_scr[...] = (y * g_ref[...]).astype(BF16)

    acc = jnp.dot(h_scr[...], w_ref[...].astype(BF16), preferred_element_type=F32)
    section = (j * tn) // half_width
    scale = jnp.where((section == 0) | (section == 3), Q_SCALE, 1.0).astype(F32)

    @pl.when(section < 2)
    def _():
        lane = lax.broadcasted_iota(jnp.int32, (acc.shape[0], LANES), 1)
        first_half = (lane % HEAD_DIM) < (HEAD_DIM // 2)
        c = cos_ref[...]
        s = sin_ref[...]
        for g in range(tn // LANES):
            xg = acc[:, g * LANES:(g + 1) * LANES]
            partner = jnp.where(first_half,
                                pltpu.roll(xg, LANES - HEAD_DIM // 2, 1),
                                pltpu.roll(xg, HEAD_DIM // 2, 1))
            o_ref[:, g * LANES:(g + 1) * LANES] = ((xg * c + partner * s) * scale).astype(BF16)

    @pl.when(section >= 2)
    def _():
        o_ref[...] = (acc * scale).astype(BF16)


def _inproj(x2d, g, w_in, cos_t, sin_t, *, seq, tm, tn):
    m, d = x2d.shape
    n = w_in.shape[1]
    half_width = d // 2
    assert half_width % tn == 0 and tn % LANES == 0 and m % tm == 0 and seq % tm == 0
    pos_tiles = seq // tm
    return pl.pallas_call(
        functools.partial(_inproj_kernel, tn=tn, half_width=half_width),
        grid=(m // tm, n // tn),
        in_specs=[
            pl.BlockSpec((tm, d), lambda i, j: (i, 0)),
            pl.BlockSpec((1, d), lambda i, j: (0, 0)),
            pl.BlockSpec((d, tn), lambda i, j: (0, j)),
            pl.BlockSpec((tm, LANES), lambda i, j: (i % pos_tiles, 0)),
            pl.BlockSpec((tm, LANES), lambda i, j: (i % pos_tiles, 0)),
        ],
        out_specs=pl.BlockSpec((tm, tn), lambda i, j: (i, j)),
        out_shape=jax.ShapeDtypeStruct((m, n), BF16),
        scratch_shapes=[pltpu.VMEM((tm, d), BF16)],
        compiler_params=_cparams(("arbitrary", "arbitrary")),
        name="inproj",
    )(x2d, g, w_in, cos_t, sin_t)


def _rope_tables(seq):
    half = HEAD_DIM // 2
    inv_freq = 1.0 / (ROPE_THETA ** (jnp.arange(half, dtype=F32) * 2.0 / HEAD_DIM))
    ang = jnp.arange(seq, dtype=F32)[:, None] * inv_freq[None, :]
    cos = jnp.cos(ang)
    sin = jnp.sin(ang)
    reps = LANES // HEAD_DIM
    cos_t = jnp.tile(jnp.concatenate([cos, cos], axis=1), (1, reps))
    sin_t = jnp.tile(jnp.concatenate([-sin, sin], axis=1), (1, reps))
    return cos_t, sin_t


def _dil_kernel(q_ref, k_ref, v_ref, o_ref, f32_scr, qp, kp, vp, operm, lperm, onat, lnat, *, seq):
    n_blocks = seq // BLOCK
    lane_q = lax.broadcasted_iota(jnp.int32, (BLOCK, LANES), 1)
    head_a_q = lane_q < HEAD_DIM
    zero_b = jnp.zeros((), BF16)

    def window_consts(n_kb):
        width = n_kb * BLOCK
        head_a_k = lax.broadcasted_iota(jnp.int32, (width, LANES), 1) < HEAD_DIM
        qi = lax.broadcasted_iota(jnp.int32, (BLOCK, width), 0)
        ki = lax.broadcasted_iota(jnp.int32, (BLOCK, width), 1)
        dist = qi + (width - BLOCK) - ki
        band = (dist >= 0) & (dist <= BLOCK)
        cur_only = ki >= width - BLOCK
        head_ones = jnp.concatenate([jnp.where(head_a_k, 1.0, 0.0), jnp.where(head_a_k, 0.0, 1.0)],
                                    axis=0).astype(BF16)
        return width, head_a_k, band, cur_only, head_ones

    f32_scr[0] = q_ref[0].astype(F32)
    f32_scr[1] = k_ref[0].astype(F32)
    f32_scr[2] = v_ref[0].astype(F32)
    kp[0:BLOCK, :] = jnp.zeros((BLOCK, LANES), BF16)
    vp[0:BLOCK, :] = jnp.zeros((BLOCK, LANES), BF16)

    for p, (window, dil) in enumerate(DIL_PATTERNS):
        assert window // dil == BLOCK
        length = seq // dil
        per_stream = length // BLOCK
        assert length % BLOCK == 0
        if dil == 1:
            qp[...] = q_ref[0]
            kp[BLOCK:, :] = k_ref[0]
            vp[BLOCK:, :] = v_ref[0]
        else:
            for r in range(dil):
                rows = pl.ds(r, length, stride=dil)
                qp[r * length:(r + 1) * length, :] = f32_scr[0, rows, :].astype(BF16)
                kp[BLOCK + r * length:BLOCK + (r + 1) * length, :] = f32_scr[1, rows, :].astype(BF16)
                vp[BLOCK + r * length:BLOCK + (r + 1) * length, :] = f32_scr[2, rows, :].astype(BF16)
        o_dst = onat.at[p] if dil == 1 else operm
        l_dst = lnat.at[p] if dil == 1 else lperm

        consts = window_consts(1 if per_stream == 1 else 2)

        def block_body(bi, carry, per_stream=per_stream, o_dst=o_dst, l_dst=l_dst, consts=consts):
            width, head_a_k, band, cur_only, head_ones = consts
            row0 = pl.multiple_of(bi * BLOCK, BLOCK)
            qb = qp[pl.ds(row0, BLOCK), :]
            win0 = pl.multiple_of(row0 + 2 * BLOCK - width, BLOCK)
            kw = kp[pl.ds(win0, width), :]
            vw = vp[pl.ds(win0, width), :]
            has_prev = (bi % per_stream) != 0
            valid = band & (cur_only | has_prev)
            kk = jnp.concatenate([jnp.where(head_a_k, kw, zero_b), jnp.where(head_a_k, zero_b, kw)], axis=0)
            s2 = lax.dot_general(qb, kk, (((1,), (1,)), ((), ())), preferred_element_type=F32)
            ps, ms = [], []
            for h in range(2):
                s = jnp.where(valid, s2[:, h * width:(h + 1) * width], -jnp.inf)
                m = jnp.max(s, axis=-1, keepdims=True)
                ps.append(jnp.exp2(s - m).astype(BF16))
                ms.append(m)
            vv = jnp.concatenate([jnp.where(head_a_k, vw, zero_b), jnp.where(head_a_k, zero_b, vw)], axis=0)
            o_den = jnp.dot(jnp.concatenate(ps, axis=1), jnp.concatenate([vv, head_ones], axis=1),
                            preferred_element_type=F32)
            o = o_den[:, :LANES]
            den = o_den[:, LANES:]
            o_dst[pl.ds(row0, BLOCK), :] = o / den
            l_dst[pl.ds(row0, BLOCK), :] = jnp.where(head_a_q, ms[0], ms[1]) + jnp.log2(den)
            return carry

        lax.fori_loop(0, n_blocks, block_body, 0, unroll=DIL_UNROLL)
        if dil != 1:
            for r in range(dil):
                rows = pl.ds(r, length, stride=dil)
                onat[p, rows, :] = operm[r * length:(r + 1) * length, :]
                lnat[p, rows, :] = lperm[r * length:(r + 1) * length, :]

    def mix_body(c, carry):
        rows = pl.ds(pl.multiple_of(c * BLOCK, BLOCK), BLOCK)
        ls = [lnat[p, rows, :] for p in range(len(DIL_PATTERNS))]
        m = functools.reduce(jnp.maximum, ls)
        es = [jnp.exp2(l - m) for l in ls]
        num = functools.reduce(lambda a, b: a + b, [e * onat[p, rows, :] for p, e in enumerate(es)])
        o_ref[0, rows, :] = num / functools.reduce(lambda a, b: a + b, es)
        return carry

    lax.fori_loop(0, n_blocks, mix_body, 0, unroll=DIL_UNROLL)


def _dilated_attention(proj3, *, half_width):
    b, seq, _ = proj3.shape
    pairs = half_width // LANES
    n_pat = len(DIL_PATTERNS)
    spec = lambda off: pl.BlockSpec((1, seq, LANES), lambda bi, c, off=off: (bi, 0, off + c))
    return pl.pallas_call(
        functools.partial(_dil_kernel, seq=seq),
        grid=(b, pairs),
        in_specs=[spec(0), spec(pairs), spec(2 * pairs)],
        out_specs=pl.BlockSpec((1, seq, LANES), lambda bi, c: (bi, 0, c)),
        out_shape=jax.ShapeDtypeStruct((b, seq, half_width), F32),
        scratch_shapes=[
            pltpu.VMEM((3, seq, LANES), F32),
            pltpu.VMEM((seq, LANES), BF16),
            pltpu.VMEM((seq + BLOCK, LANES), BF16),
            pltpu.VMEM((seq + BLOCK, LANES), BF16),
            pltpu.VMEM((seq, LANES), F32),
            pltpu.VMEM((seq, LANES), F32),
            pltpu.VMEM((n_pat, seq, LANES), F32),
            pltpu.VMEM((n_pat, seq, LANES), F32),
        ],
        compiler_params=_cparams(("arbitrary", "arbitrary")),
        name="dilated_attention",
    )(proj3, proj3, proj3)


def _sb_kernel(q_ref, k_ref, v_ref, o_ref, kk_scr, vv_scr, tri_scr, acc_scr, carry_scr, *, seq, qt):
    n_blocks = seq // BLOCK
    per_tile = qt // BLOCK
    lane_k = lax.broadcasted_iota(jnp.int32, (BLOCK, LANES), 1)
    head_a = lane_k < HEAD_DIM
    zero_b = jnp.zeros((), BF16)

    def prep(jb, carry):
        rows = pl.ds(pl.multiple_of(jb * BLOCK, BLOCK), BLOCK)
        kb = k_ref[0, rows, :]
        vb = v_ref[0, rows, :]
        kk_scr[jb, 0:BLOCK, :] = jnp.where(head_a, kb, zero_b)
        kk_scr[jb, BLOCK:, :] = jnp.where(head_a, zero_b, kb)
        vv_scr[jb, 0:BLOCK, :] = jnp.where(head_a, vb, zero_b)
        vv_scr[jb, BLOCK:, :] = jnp.where(head_a, zero_b, vb)
        return carry

    lax.fori_loop(0, n_blocks, prep, 0)

    jj = lax.broadcasted_iota(jnp.int32, (2 * BLOCK, 2 * BLOCK), 0) % BLOCK
    ss = lax.broadcasted_iota(jnp.int32, (2 * BLOCK, 2 * BLOCK), 1)
    tri_scr[...] = jnp.where((ss >= BLOCK) | (jj > ss), 1.0, 0.0).astype(BF16)

    def key_block(qb, jb, key_off):
        rs = slice(0 if key_off is None else key_off, qt)
        z = lax.dot_general(qb[rs], kk_scr[jb], (((1,), (1,)), ((), ())), preferred_element_type=F32)
        neg_abs = lax.bitcast_convert_type(lax.bitcast_convert_type(z, jnp.uint32) | jnp.uint32(0x80000000), F32)
        ls = jnp.minimum(z, 0.0) - jnp.log2(1.0 + jnp.exp2(neg_abs))
        lr = ls - z
        if key_off is not None:
            n = qt - key_off
            causal = (lax.broadcasted_iota(jnp.int32, (n, 2 * BLOCK), 1) % BLOCK
                      < lax.broadcasted_iota(jnp.int32, (n, 2 * BLOCK), 0))
            lr = jnp.where(causal, lr, 0.0)
        hi = lr.astype(BF16)
        lo = (lr - hi.astype(F32)).astype(BF16)
        tri = tri_scr[...]
        a_heads = []
        for h in range(2):
            sl = slice(h * BLOCK, (h + 1) * BLOCK)
            lhs = jnp.concatenate([hi[:, sl], lo[:, sl]], axis=1)
            cs_tot = jnp.dot(lhs, tri, preferred_element_type=F32)
            carry = carry_scr[rs, sl]
            a = jnp.exp2(ls[:, sl] + cs_tot[:, :BLOCK] + carry)
            if key_off is not None:
                a = jnp.where(causal[:, :BLOCK], a, 0.0)
            carry_scr[rs, sl] = carry + cs_tot[:, BLOCK:]
            a_heads.append(a.astype(BF16))
        acc_scr[rs, :] += jnp.dot(jnp.concatenate(a_heads, axis=1), vv_scr[jb], preferred_element_type=F32)

    def q_tile(i, carry):
        rows = pl.ds(pl.multiple_of(i * qt, qt), qt)
        qb = q_ref[0, rows, :]
        acc_scr[...] = jnp.zeros_like(acc_scr)
        carry_scr[...] = jnp.zeros_like(carry_scr)
        for t in reversed(range(per_tile)):
            key_block(qb, i * per_tile + t, t * BLOCK)

        def inner(t, c):
            for s in range(SB_UNROLL):
                key_block(qb, i * per_tile - 1 - (t * SB_UNROLL + s), None)
            return c

        lax.fori_loop(0, i * (per_tile // SB_UNROLL), inner, 0)
        o_ref[0, rows, :] = acc_scr[...]
        return carry

    lax.fori_loop(0, seq // qt, q_tile, 0)


def _stickbreak_attention(proj3, *, half_width, qt):
    b, seq, _ = proj3.shape
    pairs = half_width // LANES
    n_blocks = seq // BLOCK
    assert seq % qt == 0 and qt % BLOCK == 0
    spec = lambda off: pl.BlockSpec((1, seq, LANES), lambda bi, c, off=off: (bi, 0, off + c))
    return pl.pallas_call(
        functools.partial(_sb_kernel, seq=seq, qt=qt),
        grid=(b, pairs),
        in_specs=[spec(3 * pairs), spec(4 * pairs), spec(5 * pairs)],
        out_specs=pl.BlockSpec((1, seq, LANES), lambda bi, c: (bi, 0, c)),
        out_shape=jax.ShapeDtypeStruct((b, seq, half_width), F32),
        scratch_shapes=[
            pltpu.VMEM((n_blocks, 2 * BLOCK, LANES), BF16),
            pltpu.VMEM((n_blocks, 2 * BLOCK, LANES), BF16),
            pltpu.VMEM((2 * BLOCK, 2 * BLOCK), BF16),
            pltpu.VMEM((qt, LANES), F32),
            pltpu.VMEM((qt, 2 * BLOCK), F32),
        ],
        compiler_params=_cparams(("arbitrary", "arbitrary")),
        name="stickbreak_attention",
    )(proj3, proj3, proj3)


def _split_bf16(x):
    hi = x.astype(BF16)
    return hi, (x - hi.astype(F32)).astype(BF16)


def _outproj_kernel(od_ref, os_ref, x_ref, w_ref, gd_ref, gs_ref, gf_ref, wr_ref, br_ref,
                    x2_ref, h2_ref, idx_ref, gate_ref, rank_ref, cnt_ref,
                    acc_d, acc_s, ssq_d, ssq_s, cnt_scr, *, nk, half_width):
    k = pl.program_id(1)

    def accumulate(a_ref, g_ref, acc, ssq, first):
        a = a_ref[...]
        sq = jnp.sum(a * a, axis=-1, keepdims=True)
        prod = jnp.dot((a * g_ref[...]).astype(BF16), w_ref[...].astype(BF16), preferred_element_type=F32)
        if first:
            ssq[...] = sq
            acc[...] = prod
        else:
            ssq[...] += sq
            acc[...] += prod

    for kk in range(nk):
        @pl.when(k == kk)
        def _(kk=kk):
            if kk < nk // 2:
                accumulate(od_ref, gd_ref, acc_d, ssq_d, kk == 0)
            else:
                accumulate(os_ref, gs_ref, acc_s, ssq_s, kk == nk // 2)

    @pl.when(k == nk - 1)
    def _():
        r_d = lax.rsqrt(ssq_d[...] / half_width + NORM_EPS)
        r_s = lax.rsqrt(ssq_s[...] / half_width + NORM_EPS)
        x2 = x_ref[...] + (acc_d[...] * r_d + acc_s[...] * r_s)
        x2_ref[...] = x2
        h2 = (x2 * lax.rsqrt(jnp.mean(x2 * x2, axis=-1, keepdims=True) + NORM_EPS)) * gf_ref[...]
        packed = _pack_pairs(h2)
        n_chunk = packed.shape[1] // LANES
        for ch in range(n_chunk):
            h2_ref[pl.ds(ch, packed.shape[0], stride=n_chunk), :] = packed[:, ch * LANES:(ch + 1) * LANES]
        h_hi, h_lo = _split_bf16(h2)
        w_hi, w_lo = _split_bf16(wr_ref[...])
        hw = jnp.dot(h_hi, jnp.concatenate([w_hi, w_lo], axis=1), preferred_element_type=F32)
        logits = (hw[:, :N_EXPERTS] + hw[:, N_EXPERTS:]
                  + jnp.dot(h_lo, w_hi, preferred_element_type=F32)) + br_ref[...]
        tm = logits.shape[0]
        e_iota = lax.broadcasted_iota(jnp.int32, logits.shape, 1)
        out_lane = lax.broadcasted_iota(jnp.int32, (tm, LANES), 1)
        idx_out = jnp.zeros((tm, LANES), jnp.int32)
        val_out = jnp.full((tm, LANES), -jnp.inf, F32)
        top0 = None
        picks = []
        for kk in range(TOP_K):
            m = jnp.max(logits, axis=-1, keepdims=True)
            sel = jnp.min(jnp.where(logits == m, e_iota, N_EXPERTS), axis=-1, keepdims=True)
            idx_out = jnp.where(out_lane == kk, sel, idx_out)
            val_out = jnp.where(out_lane == kk, m, val_out)
            picks.append(e_iota == sel)
            logits = jnp.where(picks[-1], -jnp.inf, logits)
            if kk == 0:
                top0 = m
        ex = jnp.exp(val_out - top0)
        gate_ref[...] = ex / jnp.sum(ex, axis=-1, keepdims=True)
        idx_ref[...] = idx_out

        @pl.when(pl.program_id(0) == 0)
        def _():
            cnt_scr[...] = jnp.zeros_like(cnt_scr)

        chosen = functools.reduce(jnp.logical_or, picks)
        t_row = lax.broadcasted_iota(jnp.int32, (tm, tm), 0)
        t_col = lax.broadcasted_iota(jnp.int32, (tm, tm), 1)
        before = jnp.where(t_col < t_row, 1.0, 0.0).astype(BF16)
        multi_hot = jnp.where(chosen, 1.0, 0.0)
        prior = jnp.dot(before, multi_hot.astype(BF16), preferred_element_type=F32) + cnt_scr[...]
        rank_out = jnp.zeros((tm, LANES), jnp.int32)
        for kk in range(TOP_K):
            r = jnp.sum(jnp.where(picks[kk], prior, 0.0), axis=-1, keepdims=True)
            rank_out = jnp.where(out_lane == kk, r.astype(jnp.int32), rank_out)
        rank_ref[...] = rank_out
        cnt_scr[...] += jnp.sum(multi_hot, axis=0, keepdims=True)
        cnt_ref[...] = cnt_scr[...].astype(jnp.int32)


def _outproj_router(o_d, o_s, x2d, w_out, g_d, g_s, g_f, w_router, b_router, *, tm, tk):
    m, d = x2d.shape
    half_width = d // 2
    nk = d // tk
    assert half_width % tk == 0 and m % tm == 0
    kh = nk // 2
    return pl.pallas_call(
        functools.partial(_outproj_kernel, nk=nk, half_width=half_width),
        grid=(m // tm, nk),
        in_specs=[
            pl.BlockSpec((tm, tk), lambda i, k: (i, jnp.minimum(k, kh - 1))),
            pl.BlockSpec((tm, tk), lambda i, k: (i, jnp.maximum(k - kh, 0))),
            pl.BlockSpec((tm, d), lambda i, k: (i, 0)),
            pl.BlockSpec((tk, d), lambda i, k: (k, 0)),
            pl.BlockSpec((1, tk), lambda i, k: (0, jnp.minimum(k, kh - 1))),
            pl.BlockSpec((1, tk), lambda i, k: (0, jnp.maximum(k - kh, 0))),
            pl.BlockSpec((1, d), lambda i, k: (0, 0)),
            pl.BlockSpec((d, N_EXPERTS), lambda i, k: (0, 0)),
            pl.BlockSpec((1, N_EXPERTS), lambda i, k: (0, 0)),
        ],
        out_specs=[
            pl.BlockSpec((tm, d), lambda i, k: (i, 0)),
            pl.BlockSpec((tm * (d // 2 // LANES), LANES), lambda i, k: (i, 0)),
            pl.BlockSpec((tm, LANES), lambda i, k: (i, 0)),
            pl.BlockSpec((tm, LANES), lambda i, k: (i, 0)),
            pl.BlockSpec((tm, LANES), lambda i, k: (i, 0)),
            pl.BlockSpec((1, N_EXPERTS), lambda i, k: (0, 0)),
        ],
        out_shape=[
            jax.ShapeDtypeStruct((m, d), F32),
            jax.ShapeDtypeStruct((m * (d // 2 // LANES), LANES), jnp.uint32),
            jax.ShapeDtypeStruct((m, LANES), jnp.int32),
            jax.ShapeDtypeStruct((m, LANES), F32),
            jax.ShapeDtypeStruct((m, LANES), jnp.int32),
            jax.ShapeDtypeStruct((1, N_EXPERTS), jnp.int32),
        ],
        scratch_shapes=[
            pltpu.VMEM((tm, d), F32),
            pltpu.VMEM((tm, d), F32),
            pltpu.VMEM((tm, 1), F32),
            pltpu.VMEM((tm, 1), F32),
            pltpu.VMEM((1, N_EXPERTS), F32),
        ],
        compiler_params=_cparams(("arbitrary", "arbitrary")),
        name="outproj_router",
    )(o_d, o_s, x2d, w_out, g_d, g_s, g_f, w_router, b_router)


def _unpack_pairs(words):
    hi = lax.bitcast_convert_type(words & jnp.uint32(0xFFFF0000), F32)
    lo = lax.bitcast_convert_type(words << 16, F32)
    return hi.astype(BF16), lo.astype(BF16)


def _pack_pairs(x):
    c = x.shape[1] // 2
    as_bits = lambda v: lax.bitcast_convert_type(v.astype(BF16).astype(F32), jnp.uint32)
    return as_bits(x[:, :c]) | (as_bits(x[:, c:]) >> 16)


def _dispatch_kernel(row_tok, h_hbm, o_ref, h_vmem, sem, *, rows_blk, n_chunk):
    i = pl.program_id(0)

    @pl.when(i == 0)
    def _():
        cp = pltpu.make_async_copy(h_hbm, h_vmem, sem)
        cp.start()
        cp.wait()

    base = i * rows_blk

    def body(r, c):
        tok = row_tok[base + r]
        src = pl.ds(pl.multiple_of(tok * n_chunk, n_chunk), n_chunk)
        o_ref[pl.ds(pl.multiple_of(r * n_chunk, n_chunk), n_chunk), :] = h_vmem[src, :]
        return c

    lax.fori_loop(0, rows_blk, body, 0, unroll=8)


def _dispatch(hp, row_tok, *, rows_blk, n_chunk):
    n_rows = row_tok.shape[0]
    assert n_rows % rows_blk == 0
    grid_spec = pltpu.PrefetchScalarGridSpec(
        num_scalar_prefetch=1,
        grid=(n_rows // rows_blk,),
        in_specs=[pl.BlockSpec(memory_space=pl.ANY)],
        out_specs=pl.BlockSpec((rows_blk * n_chunk, LANES), lambda i, row_tok: (i, 0)),
        scratch_shapes=[pltpu.VMEM(hp.shape, hp.dtype), pltpu.SemaphoreType.DMA],
    )
    return pl.pallas_call(
        functools.partial(_dispatch_kernel, rows_blk=rows_blk, n_chunk=n_chunk),
        grid_spec=grid_spec,
        out_shape=jax.ShapeDtypeStruct((n_rows * n_chunk, LANES), hp.dtype),
        compiler_params=_cparams(("arbitrary",)),
        name="dispatch",
    )(row_tok, hp)


def _expert_kernel(item_e, item_r0, item_nb, n_items,
                   xs_hbm, wg_ref, bg_ref, wu_ref, bu_ref, wd_ref, bd_ref,
                   ys_hbm, stage, xb16, acc, pending, sem_in, sem_out, *, nj):
    i = pl.program_id(0)
    j = pl.program_id(1)
    n_live = n_items[0]
    active = i < n_live
    r0 = item_r0[i]
    n_align = item_nb[i]
    n_rows = n_align * ROW_ALIGN
    per_trip = ROW_BLOCK // ROW_ALIGN
    n_main = n_align // per_trip
    nxt = jnp.minimum(i + 1, pl.num_programs(0) - 1)
    r0_next = item_r0[nxt]
    blocks_next = jnp.where(i + 1 < n_live, item_nb[nxt], 0)
    d = acc.shape[1]
    n_chunk = d // 2 // LANES
    blk_words = ROW_ALIGN * n_chunk

    def in_copy(base_row, blk):
        src = xs_hbm.at[pl.ds(pl.multiple_of((base_row + blk * ROW_ALIGN) * n_chunk, blk_words), blk_words)]
        return pltpu.make_async_copy(src, stage.at[pl.ds(pl.multiple_of(blk * blk_words, blk_words), blk_words)],
                                     sem_in)

    def start_in(base_row, n_blk):
        def issue(blk, c):
            in_copy(base_row, blk).start()
            return c

        lax.fori_loop(0, n_blk, issue, 0)

    def out_copy(blk):
        src = acc.at[pl.ds(pl.multiple_of(blk * ROW_ALIGN, ROW_ALIGN), ROW_ALIGN)]
        dst = ys_hbm.at[pl.ds(pl.multiple_of(r0 + blk * ROW_ALIGN, ROW_ALIGN), ROW_ALIGN)]
        return pltpu.make_async_copy(src, dst, sem_out)

    def drain():
        @pl.when(pending[0] > 0)
        def _():
            rows = pl.ds(0, pending[0] * ROW_ALIGN)
            pltpu.make_async_copy(acc.at[rows], ys_hbm.at[rows], sem_out).wait()
            pending[0] = 0

    @pl.when((i == 0) & (j == 0))
    def _():
        pending[0] = 0
        start_in(r0, n_align)

    @pl.when(active & (j == 0))
    def _():
        all_rows = pl.ds(0, n_rows * n_chunk)
        pltpu.make_async_copy(xs_hbm.at[all_rows], stage.at[all_rows], sem_in).wait()
        drain()

        def unpack(rb, c):
            rows = pl.ds(pl.multiple_of(rb * ROW_ALIGN, ROW_ALIGN), ROW_ALIGN)
            for ch in range(n_chunk):
                words = stage[pl.ds(rb * (ROW_ALIGN * n_chunk) + ch, ROW_ALIGN, stride=n_chunk), :]
                hi, lo = _unpack_pairs(words)
                xb16[rows, ch * LANES:(ch + 1) * LANES] = hi
                xb16[rows, d // 2 + ch * LANES:d // 2 + (ch + 1) * LANES] = lo
            acc[rows, :] = jnp.broadcast_to(bd_ref[0], (ROW_ALIGN, d))
            return c

        lax.fori_loop(0, n_align, unpack, 0)
        start_in(r0_next, blocks_next)

    def trip(row0, n_blk):
        rows = pl.ds(row0, n_blk * ROW_ALIGN)
        xb = xb16[rows, :]
        g = jnp.dot(xb, wg_ref[0].astype(BF16), preferred_element_type=F32) + bg_ref[0]
        u = jnp.dot(xb, wu_ref[0].astype(BF16), preferred_element_type=F32) + bu_ref[0]
        g = jnp.minimum(g, SWIGLU_LIMIT)
        u = jnp.clip(u, -SWIGLU_LIMIT, SWIGLU_LIMIT)
        a = (u + 1.0) * (g * jax.nn.sigmoid(SWIGLU_ALPHA * g))
        acc[rows, :] += jnp.dot(a.astype(BF16), wd_ref[0].astype(BF16), preferred_element_type=F32)

        @pl.when(j == nj - 1)
        def _():
            for s in range(n_blk):
                out_copy(row0 // ROW_ALIGN + s).start()
            pending[0] = pending[0] + n_blk

    @pl.when(active)
    def _():
        def main_trip(rb, c):
            trip(pl.multiple_of(rb * ROW_BLOCK, ROW_BLOCK), per_trip)
            return c

        lax.fori_loop(0, n_main, main_trip, 0)

        def tail_trip(t, c):
            trip(pl.multiple_of((n_main * per_trip + t) * ROW_ALIGN, ROW_ALIGN), 1)
            return c

        lax.fori_loop(0, n_align - n_main * per_trip, tail_trip, 0)

    @pl.when((i == pl.num_programs(0) - 1) & (j == nj - 1))
    def _():
        drain()
        acc[0:ROW_ALIGN, :] = jnp.zeros((ROW_ALIGN, d), F32)
        used = n_items[1]

        def tail_copy(t):
            dst = ys_hbm.at[pl.ds(pl.multiple_of((used + t) * ROW_ALIGN, ROW_ALIGN), ROW_ALIGN)]
            return pltpu.make_async_copy(acc.at[pl.ds(0, ROW_ALIGN)], dst, sem_out)

        def issue(t, c):
            tail_copy(t).start()
            return c

        n_tail = ys_hbm.shape[0] // ROW_ALIGN - used
        lax.fori_loop(0, n_tail, issue, 0)

        def wait(t, c):
            tail_copy(t).wait()
            return c

        lax.fori_loop(0, n_tail, wait, 0)


def _experts(xs, w_gate, b_gate, w_up, b_up, w_down, b_down, meta, *, n_rows, th):
    item_e, item_r0, item_nb, n_items = meta
    n_exp, d, dh = w_gate.shape
    nj = dh // th
    n_item_slots = item_e.shape[0]

    def w_col(i, j, item_e, item_r0, item_nb, n_items):
        return (item_e[i], 0, jnp.where(i < n_items[0], j, nj - 1))

    def w_row(i, j, item_e, item_r0, item_nb, n_items):
        return (item_e[i], jnp.where(i < n_items[0], j, nj - 1), 0)

    def b_all(i, j, item_e, item_r0, item_nb, n_items):
        return (item_e[i], 0, 0)

    grid_spec = pltpu.PrefetchScalarGridSpec(
        num_scalar_prefetch=4,
        grid=(n_item_slots, nj),
        in_specs=[
            pl.BlockSpec(memory_space=pl.ANY),
            pl.BlockSpec((1, d, th), w_col),
            pl.BlockSpec((1, 1, th), w_col),
            pl.BlockSpec((1, d, th), w_col),
            pl.BlockSpec((1, 1, th), w_col),
            pl.BlockSpec((1, th, d), w_row),
            pl.BlockSpec((1, 1, d), b_all),
        ],
        out_specs=pl.BlockSpec(memory_space=pl.ANY),
        scratch_shapes=[
            pltpu.VMEM((ITEM_ROWS * (d // 2 // LANES), LANES), jnp.uint32),
            pltpu.VMEM((ITEM_ROWS, d), BF16),
            pltpu.VMEM((ITEM_ROWS, d), F32),
            pltpu.SMEM((1,), jnp.int32),
            pltpu.SemaphoreType.DMA,
            pltpu.SemaphoreType.DMA,
        ],
    )
    return pl.pallas_call(
        functools.partial(_expert_kernel, nj=nj),
        grid_spec=grid_spec,
        out_shape=jax.ShapeDtypeStruct((n_rows, d), F32),
        compiler_params=_cparams(("arbitrary", "arbitrary")),
        name="experts",
    )(item_e, item_r0, item_nb, n_items,
      xs, w_gate, b_gate.reshape(n_exp, 1, dh), w_up, b_up.reshape(n_exp, 1, dh), w_down,
      b_down.reshape(n_exp, 1, d))


def _routing_metadata(top_idx, top_rank, counts, n_tok):
    n_assign = n_tok * TOP_K
    flat_e = top_idx.reshape(n_assign)
    rank = top_rank.reshape(n_assign)
    padded = (counts + ROW_ALIGN - 1) // ROW_ALIGN * ROW_ALIGN
    pend = jnp.cumsum(padded)
    pstart = pend - padded
    dest = (pstart[flat_e] + rank).astype(jnp.int32)
    n_rows = (n_assign // ROW_ALIGN + N_EXPERTS) * ROW_ALIGN
    flat_tok = jnp.arange(n_assign, dtype=jnp.int32) // TOP_K
    row_tok = jnp.zeros((n_rows,), jnp.int32).at[dest].set(
        flat_tok, unique_indices=True, mode="promise_in_bounds")
    n_chunks = (padded + ITEM_ROWS - 1) // ITEM_ROWS
    cend = jnp.cumsum(n_chunks)
    cstart = cend - n_chunks
    n_slots = N_EXPERTS + n_assign // ITEM_ROWS + 1
    slot = jnp.arange(n_slots, dtype=jnp.int32)
    n_items = cend[-1]
    last_e = jnp.max(jnp.where(n_chunks > 0, jnp.arange(N_EXPERTS), 0))
    e_of = jnp.minimum(jnp.searchsorted(cend, slot, side='right'), N_EXPERTS - 1)
    live = slot < n_items
    item_e = jnp.where(live, e_of, last_e).astype(jnp.int32)
    chunk = slot - cstart[e_of]
    item_r0 = jnp.where(live, pstart[e_of] + chunk * ITEM_ROWS, 0).astype(jnp.int32)
    item_rows = jnp.clip(padded[e_of] - chunk * ITEM_ROWS, 0, ITEM_ROWS)
    item_nb = jnp.where(live, item_rows // ROW_ALIGN, 0).astype(jnp.int32)
    counts_blk = jnp.stack([n_items, pend[-1] // ROW_ALIGN]).astype(jnp.int32)
    meta = (item_e, item_r0, item_nb, counts_blk)
    return meta, row_tok, dest, n_rows


def _combine_kernel(dest, ys_hbm, x2_ref, gate_ref, g_ref, o_ref, buf, sem, *, tm):
    i = pl.program_id(0)
    n = pl.num_programs(0)

    def row_copy(tile, slot, t, kk):
        r = dest[(tile * tm + t) * TOP_K + kk]
        return pltpu.make_async_copy(ys_hbm.at[pl.ds(r, 1)], buf.at[slot, kk, pl.ds(t, 1)], sem.at[slot])

    def issue_tile(tile, slot):
        def issue(t, c):
            for kk in range(TOP_K):
                row_copy(tile, slot, t, kk).start()
            return c

        lax.fori_loop(0, tm, issue, 0, unroll=8)

    slot = i % 2

    @pl.when(i == 0)
    def _():
        issue_tile(0, 0)

    @pl.when(i + 1 < n)
    def _():
        issue_tile(i + 1, 1 - slot)

    for kk in range(TOP_K):
        pltpu.make_async_copy(ys_hbm.at[pl.ds(0, tm)], buf.at[slot, kk], sem.at[slot]).wait()

    gates = gate_ref[...]
    y = x2_ref[...]
    for kk in range(TOP_K):
        y = y + gates[:, kk:kk + 1] * buf[slot, kk]
    o_ref[...] = (y * lax.rsqrt(jnp.mean(y * y, axis=-1, keepdims=True) + NORM_EPS)) * g_ref[...]


def _combine(dest, ys, x2, gates, g_final, *, tm):
    m, d = x2.shape
    grid_spec = pltpu.PrefetchScalarGridSpec(
        num_scalar_prefetch=1,
        grid=(m // tm,),
        in_specs=[
            pl.BlockSpec(memory_space=pl.ANY),
            pl.BlockSpec((tm, d), lambda i, dest: (i, 0)),
            pl.BlockSpec((tm, LANES), lambda i, dest: (i, 0)),
            pl.BlockSpec((1, d), lambda i, dest: (0, 0)),
        ],
        out_specs=pl.BlockSpec((tm, d), lambda i, dest: (i, 0)),
        scratch_shapes=[pltpu.VMEM((2, TOP_K, tm, d), F32), pltpu.SemaphoreType.DMA((2,))],
    )
    return pl.pallas_call(
        functools.partial(_combine_kernel, tm=tm),
        grid_spec=grid_spec,
        out_shape=jax.ShapeDtypeStruct((m, d), F32),
        compiler_params=_cparams(("arbitrary",)),
        name="combine",
    )(dest, ys, x2, gates, g_final)


def _layer(x, attn_norm, w_in, dil_out_norm, sb_out_norm, w_out, ffn_norm,
           w_router, b_router, w_gate, b_gate, w_up, b_up, w_down, b_down, *, tiles):
    b, seq, d = x.shape
    m = b * seq
    half_width = d // 2
    x2d = x.reshape(m, d)
    cos_t, sin_t = _rope_tables(seq)
    proj = _inproj(x2d, attn_norm.reshape(1, d), w_in, cos_t, sin_t,
                   seq=seq, tm=tiles["in_tm"], tn=tiles["in_tn"])
    proj3 = proj.reshape(b, seq, 3 * d)
    o_d = _dilated_attention(proj3, half_width=half_width).reshape(m, half_width)
    o_s = _stickbreak_attention(proj3, half_width=half_width, qt=tiles["sb_qt"]).reshape(m, half_width)
    x2, h2, idx_l, gate_l, rank_l, counts = _outproj_router(
        o_d, o_s, x2d, w_out, dil_out_norm.reshape(1, half_width), sb_out_norm.reshape(1, half_width),
        ffn_norm.reshape(1, d), w_router, b_router.reshape(1, N_EXPERTS),
        tm=tiles["out_tm"], tk=tiles["out_tk"])
    meta, row_tok, dest, n_rows = _routing_metadata(idx_l[:, :TOP_K], rank_l[:, :TOP_K], counts[0], m)
    xs = _dispatch(h2, row_tok, rows_blk=tiles["disp_rows"], n_chunk=d // 2 // LANES)
    ys = _experts(xs, w_gate, b_gate, w_up, b_up, w_down, b_down, meta, n_rows=n_rows, th=tiles["moe_th"])
    return x2, ys, dest, gate_l


_TILES = dict(in_tm=1024, in_tn=512, sb_qt=512, out_tm=512, out_tk=512, disp_rows=1024, moe_th=512, cmb_tm=128)


def kernel(x, attn_norm, w_in, dil_out_norm, sb_out_norm, w_out, ffn_norm, w_router, b_router,
           w_gate, b_gate, w_up, b_up, w_down, b_down, final_norm):
    depth = attn_norm.shape[0]
    b, seq, d = x.shape
    for l in range(depth):
        x2, ys, dest, gate_l = _layer(
            x, attn_norm[l], w_in[l], dil_out_norm[l], sb_out_norm[l], w_out[l], ffn_norm[l],
            w_router[l], b_router[l], w_gate[l], b_gate[l], w_up[l], b_up[l], w_down[l], b_down[l],
            tiles=_TILES)
        assert depth == 1
        x = _combine(dest, ys, x2, gate_l, final_norm.reshape(1, d), tm=_TILES["cmb_tm"])
    return x.reshape(b, seq, d)
```

```python
import functools
import math

import jax
import jax.numpy as jnp
from jax import lax
from jax.experimental import pallas as pl
from jax.experimental.pallas import tpu as pltpu

F32 = jnp.float32
BF16 = jnp.bfloat16

HEAD_DIM = 64
LANES = 128
BLOCK = 128
DIL_PATTERNS = ((128, 1), (512, 4), (2048, 16))
ROPE_THETA = 10000.0
N_EXPERTS = 32
TOP_K = 4
SWIGLU_LIMIT = 7.0
SWIGLU_ALPHA = 1.702
NORM_EPS = 1e-5
Q_SCALE = HEAD_DIM ** -0.5 * math.log2(math.e)
DIL_UNROLL = 16
ROW_ALIGN = 128
ROW_BLOCK = 512
ITEM_ROWS = 1280
VMEM_LIMIT = 56 * 1024 * 1024


def _cparams(sem):
    return pltpu.CompilerParams(dimension_semantics=sem, vmem_limit_bytes=VMEM_LIMIT)


def _inproj_kernel(x_ref, g_ref, w_ref, cos_ref, sin_ref, o_ref, h_scr, *, tn, half_width):
    j = pl.program_id(1)

    @pl.when(j == 0)
    def _():
        x = x_ref[...]
        y = x * lax.rsqrt(jnp.mean(x * x, axis=-1, keepdims=True) + NORM_EPS)
        h_scr[...] = (y * g_ref[...]).astype(BF16)

    acc = jnp.dot(h_scr[...], w_ref[...].astype(BF16), preferred_element_type=F32)
    section = (j * tn) // half_width
    scale = jnp.where((section == 0) | (section == 3), Q_SCALE, 1.0).astype(F32)

    @pl.when(section < 2)
    def _():
        lane = lax.broadcasted_iota(jnp.int32, (acc.shape[0], LANES), 1)
        first_half = (lane % HEAD_DIM) < (HEAD_DIM // 2)
        c = cos_ref[...]
        s = sin_ref[...]
        for g in range(tn // LANES):
            xg = acc[:, g * LANES:(g + 1) * LANES]
            partner = jnp.where(first_half,
                                pltpu.roll(xg, LANES - HEAD_DIM // 2, 1),
                                pltpu.roll(xg, HEAD_DIM // 2, 1))
            o_ref[:, g * LANES:(g + 1) * LANES] = ((xg * c + partner * s) * scale).astype(BF16)

    @pl.when(section >= 2)
    def _():
        o_ref[...] = (acc * scale).astype(BF16)


def _inproj(x2d, g, w_in, cos_t, sin_t, *, seq, tm, tn):
    m, d = x2d.shape
    n = w_in.shape[1]
    half_width = d // 2
    assert half_width % tn == 0 and tn % LANES == 0 and m % tm == 0 and seq % tm == 0
    pos_tiles = seq // tm
    return pl.pallas_call(
        functools.partial(_inproj_kernel, tn=tn, half_width=half_width),
        grid=(m // tm, n // tn),
        in_specs=[
            pl.BlockSpec((tm, d), lambda i, j: (i, 0)),
            pl.BlockSpec((1, d), lambda i, j: (0, 0)),
            pl.BlockSpec((d, tn), lambda i, j: (0, j)),
            pl.BlockSpec((tm, LANES), lambda i, j: (i % pos_tiles, 0)),
            pl.BlockSpec((tm, LANES), lambda i, j: (i % pos_tiles, 0)),
        ],
        out_specs=pl.BlockSpec((tm, tn), lambda i, j: (i, j)),
        out_shape=jax.ShapeDtypeStruct((m, n), BF16),
        scratch_shapes=[pltpu.VMEM((tm, d), BF16)],
        compiler_params=_cparams(("arbitrary", "arbitrary")),
        name="inproj",
    )(x2d, g, w_in, cos_t, sin_t)


def _rope_tables(seq):
    half = HEAD_DIM // 2
    inv_freq = 1.0 / (ROPE_THETA ** (jnp.arange(half, dtype=F32) * 2.0 / HEAD_DIM))
    ang = jnp.arange(seq, dtype=F32)[:, None] * inv_freq[None, :]
    cos = jnp.cos(ang)
    sin = jnp.sin(ang)
    reps = LANES // HEAD_DIM
    cos_t = jnp.tile(jnp.concatenate([cos, cos], axis=1), (1, reps))
    sin_t = jnp.tile(jnp.concatenate([-sin, sin], axis=1), (1, reps))
    return cos_t, sin_t


def _dil_kernel(q_ref, k_ref, v_ref, o_ref, f32_scr, qp, kp, vp, operm, lperm, onat, lnat, *, seq):
    n_blocks = seq // BLOCK
    lane_q = lax.broadcasted_iota(jnp.int32, (BLOCK, LANES), 1)
    head_a_q = lane_q < HEAD_DIM
    zero_b = jnp.zeros((), BF16)

    def window_consts(n_kb):
        width = n_kb * BLOCK
        head_a_k = lax.broadcasted_iota(jnp.int32, (width, LANES), 1) < HEAD_DIM
        qi = lax.broadcasted_iota(jnp.int32, (BLOCK, width), 0)
        ki = lax.broadcasted_iota(jnp.int32, (BLOCK, width), 1)
        dist = qi + (width - BLOCK) - ki
        band = (dist >= 0) & (dist <= BLOCK)
        cur_only = ki >= width - BLOCK
        head_ones = jnp.concatenate([jnp.where(head_a_k, 1.0, 0.0), jnp.where(head_a_k, 0.0, 1.0)],
                                    axis=0).astype(BF16)
        return width, head_a_k, band, cur_only, head_ones

    f32_scr[0] = q_ref[0].astype(F32)
    f32_scr[1] = k_ref[0].astype(F32)
    f32_scr[2] = v_ref[0].astype(F32)
    kp[0:BLOCK, :] = jnp.zeros((BLOCK, LANES), BF16)
    vp[0:BLOCK, :] = jnp.zeros((BLOCK, LANES), BF16)

    for p, (window, dil) in enumerate(DIL_PATTERNS):
        assert window // dil == BLOCK
        length = seq // dil
        per_stream = length // BLOCK
        assert length % BLOCK == 0
        if dil == 1:
            qp[...] = q_ref[0]
            kp[BLOCK:, :] = k_ref[0]
            vp[BLOCK:, :] = v_ref[0]
        else:
            for r in range(dil):
                rows = pl.ds(r, length, stride=dil)
                qp[r * length:(r + 1) * length, :] = f32_scr[0, rows, :].astype(BF16)
                kp[BLOCK + r * length:BLOCK + (r + 1) * length, :] = f32_scr[1, rows, :].astype(BF16)
                vp[BLOCK + r * length:BLOCK + (r + 1) * length, :] = f32_scr[2, rows, :].astype(BF16)
        o_dst = onat.at[p] if dil == 1 else operm
        l_dst = lnat.at[p] if dil == 1 else lperm

        consts = window_consts(1 if per_stream == 1 else 2)

        def block_body(bi, carry, per_stream=per_stream, o_dst=o_dst, l_dst=l_dst, consts=consts):
            width, head_a_k, band, cur_only, head_ones = consts
            row0 = pl.multiple_of(bi * BLOCK, BLOCK)
            qb = qp[pl.ds(row0, BLOCK), :]
            win0 = pl.multiple_of(row0 + 2 * BLOCK - width, BLOCK)
            kw = kp[pl.ds(win0, width), :]
            vw = vp[pl.ds(win0, width), :]
            has_prev = (bi % per_stream) != 0
            valid = band & (cur_only | has_prev)
            kk = jnp.concatenate([jnp.where(head_a_k, kw, zero_b), jnp.where(head_a_k, zero_b, kw)], axis=0)
            s2 = lax.dot_general(qb, kk, (((1,), (1,)), ((), ())), preferred_element_type=F32)
            ps, ms = [], []
            for h in range(2):
                s = jnp.where(valid, s2[:, h * width:(h + 1) * width], -jnp.inf)
                m = jnp.max(s, axis=-1, keepdims=True)
                ps.append(jnp.exp2(s - m).astype(BF16))
                ms.append(m)
            vv = jnp.concatenate([jnp.where(head_a_k, vw, zero_b), jnp.where(head_a_k, zero_b, vw)], axis=0)
            o_den = jnp.dot(jnp.concatenate(ps, axis=1), jnp.concatenate([vv, head_ones], axis=1),
                            preferred_element_type=F32)
            o = o_den[:, :LANES]
            den = o_den[:, LANES:]
            o_dst[pl.ds(row0, BLOCK), :] = o / den
            l_dst[pl.ds(row0, BLOCK), :] = jnp.where(head_a_q, ms[0], ms[1]) + jnp.log2(den)
            return carry

        lax.fori_loop(0, n_blocks, block_body, 0, unroll=DIL_UNROLL)
        if dil != 1:
            for r in range(dil):
                rows = pl.ds(r, length, stride=dil)
                onat[p, rows, :] = operm[r * length:(r + 1) * length, :]
                lnat[p, rows, :] = lperm[r * length:(r + 1) * length, :]

    def mix_body(c, carry):
        rows = pl.ds(pl.multiple_of(c * BLOCK, BLOCK), BLOCK)
        ls = [lnat[p, rows, :] for p in range(len(DIL_PATTERNS))]
        m = functools.reduce(jnp.maximum, ls)
        es = [jnp.exp2(l - m) for l in ls]
        num = functools.reduce(lambda a, b: a + b, [e * onat[p, rows, :] for p, e in enumerate(es)])
        o_ref[0, rows, :] = num / functools.reduce(lambda a, b: a + b, es)
        return carry

    lax.fori_loop(0, n_blocks, mix_body, 0, unroll=DIL_UNROLL)


def _dilated_attention(proj3, *, half_width):
    b, seq, _ = proj3.shape
    pairs = half_width // LANES
    n_pat = len(DIL_PATTERNS)
    spec = lambda off: pl.BlockSpec((1, seq, LANES), lambda bi, c, off=off: (bi, 0, off + c))
    return pl.pallas_call(
        functools.partial(_dil_kernel, seq=seq),
        grid=(b, pairs),
        in_specs=[spec(0), spec(pairs), spec(2 * pairs)],
        out_specs=pl.BlockSpec((1, seq, LANES), lambda bi, c: (bi, 0, c)),
        out_shape=jax.ShapeDtypeStruct((b, seq, half_width), F32),
        scratch_shapes=[
            pltpu.VMEM((3, seq, LANES), F32),
            pltpu.VMEM((seq, LANES), BF16),
            pltpu.VMEM((seq + BLOCK, LANES), BF16),
            pltpu.VMEM((seq + BLOCK, LANES), BF16),
            pltpu.VMEM((seq, LANES), F32),
            pltpu.VMEM((seq, LANES), F32),
            pltpu.VMEM((n_pat, seq, LANES), F32),
            pltpu.VMEM((n_pat, seq, LANES), F32),
        ],
        compiler_params=_cparams(("arbitrary", "arbitrary")),
        name="dilated_attention",
    )(proj3, proj3, proj3)


def _sb_kernel(q_ref, k_ref, v_ref, o_ref, kk_scr, vv_scr, tri_scr, acc_scr, carry_scr, *, seq, qt):
    n_blocks = seq // BLOCK
    per_tile = qt // BLOCK
    lane_k = lax.broadcasted_iota(jnp.int32, (BLOCK, LANES), 1)
    head_a = lane_k < HEAD_DIM
    zero_b = jnp.zeros((), BF16)

    def prep(jb, carry):
        rows = pl.ds(pl.multiple_of(jb * BLOCK, BLOCK), BLOCK)
        kb = k_ref[0, rows, :]
        vb = v_ref[0, rows, :]
        kk_scr[jb, 0:BLOCK, :] = jnp.where(head_a, kb, zero_b)
        kk_scr[jb, BLOCK:, :] = jnp.where(head_a, zero_b, kb)
        vv_scr[jb, 0:BLOCK, :] = jnp.where(head_a, vb, zero_b)
        vv_scr[jb, BLOCK:, :] = jnp.where(head_a, zero_b, vb)
        return carry

    lax.fori_loop(0, n_blocks, prep, 0, unroll=True)

    jj = lax.broadcasted_iota(jnp.int32, (2 * BLOCK, 2 * BLOCK), 0) % BLOCK
    ss = lax.broadcasted_iota(jnp.int32, (2 * BLOCK, 2 * BLOCK), 1)
    tri_scr[...] = jnp.where((ss >= BLOCK) | (jj > ss), 1.0, 0.0).astype(BF16)

    def key_block(ti, qb, jb, key_off):
        rs = slice(0 if key_off is None else key_off, qt)
        z = lax.dot_general(qb[rs], kk_scr[jb], (((1,), (1,)), ((), ())), preferred_element_type=F32)
        neg_abs = lax.bitcast_convert_type(lax.bitcast_convert_type(z, jnp.uint32) | jnp.uint32(0x80000000), F32)
        ls = jnp.minimum(z, 0.0) - jnp.log2(1.0 + jnp.exp2(neg_abs))
        lr = ls - z
        if key_off is not None:
            n = qt - key_off
            causal = (lax.broadcasted_iota(jnp.int32, (n, 2 * BLOCK), 1) % BLOCK
                      < lax.broadcasted_iota(jnp.int32, (n, 2 * BLOCK), 0))
            lr = jnp.where(causal, lr, 0.0)
        hi = lr.astype(BF16)
        lo = (lr - hi.astype(F32)).astype(BF16)
        tri = tri_scr[...]
        a_heads = []
        for h in range(2):
            sl = slice(h * BLOCK, (h + 1) * BLOCK)
            lhs = jnp.concatenate([hi[:, sl], lo[:, sl]], axis=1)
            cs_tot = jnp.dot(lhs, tri, preferred_element_type=F32)
            carry = carry_scr[ti, rs, sl]
            a = jnp.exp2(ls[:, sl] + cs_tot[:, :BLOCK] + carry)
            if key_off is not None:
                a = jnp.where(causal[:, :BLOCK], a, 0.0)
            carry_scr[ti, rs, sl] = carry + cs_tot[:, BLOCK:]
            a_heads.append(a.astype(BF16))
        acc_scr[ti, rs, :] += jnp.dot(jnp.concatenate(a_heads, axis=1), vv_scr[jb], preferred_element_type=F32)

    for i in range(seq // qt):
        rows = slice(i * qt, (i + 1) * qt)
        qb = q_ref[0, rows, :]
        acc_scr[i] = jnp.zeros(acc_scr.shape[1:], F32)
        carry_scr[i] = jnp.zeros(carry_scr.shape[1:], F32)
        for jb in reversed(range((i + 1) * per_tile)):
            in_tile = jb - i * per_tile
            key_block(i, qb, jb, in_tile * BLOCK if in_tile >= 0 else None)
        o_ref[0, rows, :] = acc_scr[i]


def _stickbreak_attention(proj3, *, half_width, qt):
    b, seq, _ = proj3.shape
    pairs = half_width // LANES
    n_blocks = seq // BLOCK
    assert seq % qt == 0 and qt % BLOCK == 0
    spec = lambda off: pl.BlockSpec((1, seq, LANES), lambda bi, c, off=off: (bi, 0, off + c))
    return pl.pallas_call(
        functools.partial(_sb_kernel, seq=seq, qt=qt),
        grid=(b, pairs),
        in_specs=[spec(3 * pairs), spec(4 * pairs), spec(5 * pairs)],
        out_specs=pl.BlockSpec((1, seq, LANES), lambda bi, c: (bi, 0, c)),
        out_shape=jax.ShapeDtypeStruct((b, seq, half_width), F32),
        scratch_shapes=[
            pltpu.VMEM((n_blocks, 2 * BLOCK, LANES), BF16),
            pltpu.VMEM((n_blocks, 2 * BLOCK, LANES), BF16),
            pltpu.VMEM((2 * BLOCK, 2 * BLOCK), BF16),
            pltpu.VMEM((seq // qt, qt, LANES), F32),
            pltpu.VMEM((seq // qt, qt, 2 * BLOCK), F32),
        ],
        compiler_params=_cparams(("arbitrary", "arbitrary")),
        name="stickbreak_attention",
    )(proj3, proj3, proj3)


def _split_bf16(x):
    hi = x.astype(BF16)
    return hi, (x - hi.astype(F32)).astype(BF16)


def _outproj_kernel(od_ref, os_ref, x_ref, w_ref, gd_ref, gs_ref, gf_ref, wr_ref, br_ref,
                    x2_ref, h2_ref, idx_ref, gate_ref, rank_ref, cnt_ref,
                    acc_d, acc_s, ssq_d, ssq_s, cnt_scr, *, nk, half_width):
    k = pl.program_id(1)

    def accumulate(a_ref, g_ref, acc, ssq, first):
        a = a_ref[...]
        sq = jnp.sum(a * a, axis=-1, keepdims=True)
        prod = jnp.dot((a * g_ref[...]).astype(BF16), w_ref[...].astype(BF16), preferred_element_type=F32)
        if first:
            ssq[...] = sq
            acc[...] = prod
        else:
            ssq[...] += sq
            acc[...] += prod

    for kk in range(nk):
        @pl.when(k == kk)
        def _(kk=kk):
            if kk < nk // 2:
                accumulate(od_ref, gd_ref, acc_d, ssq_d, kk == 0)
            else:
                accumulate(os_ref, gs_ref, acc_s, ssq_s, kk == nk // 2)

    @pl.when(k == nk - 1)
    def _():
        r_d = lax.rsqrt(ssq_d[...] / half_width + NORM_EPS)
        r_s = lax.rsqrt(ssq_s[...] / half_width + NORM_EPS)
        x2 = x_ref[...] + (acc_d[...] * r_d + acc_s[...] * r_s)
        x2_ref[...] = x2
        h2 = (x2 * lax.rsqrt(jnp.mean(x2 * x2, axis=-1, keepdims=True) + NORM_EPS)) * gf_ref[...]
        packed = _pack_pairs(h2)
        n_chunk = packed.shape[1] // LANES
        for ch in range(n_chunk):
            h2_ref[pl.ds(ch, packed.shape[0], stride=n_chunk), :] = packed[:, ch * LANES:(ch + 1) * LANES]
        h_hi, h_lo = _split_bf16(h2)
        w_hi, w_lo = _split_bf16(wr_ref[...])
        hw = jnp.dot(h_hi, jnp.concatenate([w_hi, w_lo], axis=1), preferred_element_type=F32)
        logits = (hw[:, :N_EXPERTS] + hw[:, N_EXPERTS:]
                  + jnp.dot(h_lo, w_hi, preferred_element_type=F32)) + br_ref[...]
        tm = logits.shape[0]
        e_iota = lax.broadcasted_iota(jnp.int32, logits.shape, 1)
        out_lane = lax.broadcasted_iota(jnp.int32, (tm, LANES), 1)
        idx_out = jnp.zeros((tm, LANES), jnp.int32)
        val_out = jnp.full((tm, LANES), -jnp.inf, F32)
        top0 = None
        picks = []
        for kk in range(TOP_K):
            m = jnp.max(logits, axis=-1, keepdims=True)
            sel = jnp.min(jnp.where(logits == m, e_iota, N_EXPERTS), axis=-1, keepdims=True)
            idx_out = jnp.where(out_lane == kk, sel, idx_out)
            val_out = jnp.where(out_lane == kk, m, val_out)
            picks.append(e_iota == sel)
            logits = jnp.where(picks[-1], -jnp.inf, logits)
            if kk == 0:
                top0 = m
        ex = jnp.exp(val_out - top0)
        gate_ref[...] = ex / jnp.sum(ex, axis=-1, keepdims=True)
        idx_ref[...] = idx_out

        @pl.when(pl.program_id(0) == 0)
        def _():
            cnt_scr[...] = jnp.zeros_like(cnt_scr)

        chosen = functools.reduce(jnp.logical_or, picks)
        t_row = lax.broadcasted_iota(jnp.int32, (tm, tm), 0)
        t_col = lax.broadcasted_iota(jnp.int32, (tm, tm), 1)
        before = jnp.where(t_col < t_row, 1.0, 0.0).astype(BF16)
        multi_hot = jnp.where(chosen, 1.0, 0.0)
        prior = jnp.dot(before, multi_hot.astype(BF16), preferred_element_type=F32) + cnt_scr[...]
        rank_out = jnp.zeros((tm, LANES), jnp.int32)
        for kk in range(TOP_K):
            r = jnp.sum(jnp.where(picks[kk], prior, 0.0), axis=-1, keepdims=True)
            rank_out = jnp.where(out_lane == kk, r.astype(jnp.int32), rank_out)
        rank_ref[...] = rank_out
        cnt_scr[...] += jnp.sum(multi_hot, axis=0, keepdims=True)
        cnt_ref[...] = cnt_scr[...].astype(jnp.int32)


def _outproj_router(o_d, o_s, x2d, w_out, g_d, g_s, g_f, w_router, b_router, *, tm, tk):
    m, d = x2d.shape
    half_width = d // 2
    nk = d // tk
    assert half_width % tk == 0 and m % tm == 0
    kh = nk // 2
    return pl.pallas_call(
        functools.partial(_outproj_kernel, nk=nk, half_width=half_width),
        grid=(m // tm, nk),
        in_specs=[
            pl.BlockSpec((tm, tk), lambda i, k: (i, jnp.minimum(k, kh - 1))),
            pl.BlockSpec((tm, tk), lambda i, k: (i, jnp.maximum(k - kh, 0))),
            pl.BlockSpec((tm, d), lambda i, k: (i, 0)),
            pl.BlockSpec((tk, d), lambda i, k: (k, 0)),
            pl.BlockSpec((1, tk), lambda i, k: (0, jnp.minimum(k, kh - 1))),
            pl.BlockSpec((1, tk), lambda i, k: (0, jnp.maximum(k - kh, 0))),
            pl.BlockSpec((1, d), lambda i, k: (0, 0)),
            pl.BlockSpec((d, N_EXPERTS), lambda i, k: (0, 0)),
            pl.BlockSpec((1, N_EXPERTS), lambda i, k: (0, 0)),
        ],
        out_specs=[
            pl.BlockSpec((tm, d), lambda i, k: (i, 0)),
            pl.BlockSpec((tm * (d // 2 // LANES), LANES), lambda i, k: (i, 0)),
            pl.BlockSpec((tm, LANES), lambda i, k: (i, 0)),
            pl.BlockSpec((tm, LANES), lambda i, k: (i, 0)),
            pl.BlockSpec((tm, LANES), lambda i, k: (i, 0)),
            pl.BlockSpec((1, N_EXPERTS), lambda i, k: (0, 0)),
        ],
        out_shape=[
            jax.ShapeDtypeStruct((m, d), F32),
            jax.ShapeDtypeStruct((m * (d // 2 // LANES), LANES), jnp.uint32),
            jax.ShapeDtypeStruct((m, LANES), jnp.int32),
            jax.ShapeDtypeStruct((m, LANES), F32),
            jax.ShapeDtypeStruct((m, LANES), jnp.int32),
            jax.ShapeDtypeStruct((1, N_EXPERTS), jnp.int32),
        ],
        scratch_shapes=[
            pltpu.VMEM((tm, d), F32),
            pltpu.VMEM((tm, d), F32),
            pltpu.VMEM((tm, 1), F32),
            pltpu.VMEM((tm, 1), F32),
            pltpu.VMEM((1, N_EXPERTS), F32),
        ],
        compiler_params=_cparams(("arbitrary", "arbitrary")),
        name="outproj_router",
    )(o_d, o_s, x2d, w_out, g_d, g_s, g_f, w_router, b_router)


def _unpack_pairs(words):
    hi = lax.bitcast_convert_type(words & jnp.uint32(0xFFFF0000), F32)
    lo = lax.bitcast_convert_type(words << 16, F32)
    return hi.astype(BF16), lo.astype(BF16)


def _pack_pairs(x):
    c = x.shape[1] // 2
    as_bits = lambda v: lax.bitcast_convert_type(v.astype(BF16).astype(F32), jnp.uint32)
    return as_bits(x[:, :c]) | (as_bits(x[:, c:]) >> 16)


def _dispatch_kernel(row_tok, h_hbm, o_ref, h_vmem, sem, *, rows_blk, n_chunk):
    i = pl.program_id(0)

    @pl.when(i == 0)
    def _():
        cp = pltpu.make_async_copy(h_hbm, h_vmem, sem)
        cp.start()
        cp.wait()

    base = i * rows_blk

    def body(r, c):
        tok = row_tok[base + r]
        src = pl.ds(pl.multiple_of(tok * n_chunk, n_chunk), n_chunk)
        o_ref[pl.ds(pl.multiple_of(r * n_chunk, n_chunk), n_chunk), :] = h_vmem[src, :]
        return c

    lax.fori_loop(0, rows_blk, body, 0, unroll=8)


def _dispatch(hp, row_tok, *, rows_blk, n_chunk):
    n_rows = row_tok.shape[0]
    assert n_rows % rows_blk == 0
    grid_spec = pltpu.PrefetchScalarGridSpec(
        num_scalar_prefetch=1,
        grid=(n_rows // rows_blk,),
        in_specs=[pl.BlockSpec(memory_space=pl.ANY)],
        out_specs=pl.BlockSpec((rows_blk * n_chunk, LANES), lambda i, row_tok: (i, 0)),
        scratch_shapes=[pltpu.VMEM(hp.shape, hp.dtype), pltpu.SemaphoreType.DMA],
    )
    return pl.pallas_call(
        functools.partial(_dispatch_kernel, rows_blk=rows_blk, n_chunk=n_chunk),
        grid_spec=grid_spec,
        out_shape=jax.ShapeDtypeStruct((n_rows * n_chunk, LANES), hp.dtype),
        compiler_params=_cparams(("arbitrary",)),
        name="dispatch",
    )(row_tok, hp)


def _expert_kernel(item_e, item_r0, item_nb, n_items,
                   xs_hbm, wg_ref, bg_ref, wu_ref, bu_ref, wd_ref, bd_ref,
                   ys_hbm, stage, xb16, acc, pending, sem_in, sem_out, *, nj):
    i = pl.program_id(0)
    j = pl.program_id(1)
    n_live = n_items[0]
    active = i < n_live
    r0 = item_r0[i]
    n_align = item_nb[i]
    n_rows = n_align * ROW_ALIGN
    per_trip = ROW_BLOCK // ROW_ALIGN
    n_main = n_align // per_trip
    nxt = jnp.minimum(i + 1, pl.num_programs(0) - 1)
    r0_next = item_r0[nxt]
    blocks_next = jnp.where(i + 1 < n_live, item_nb[nxt], 0)
    d = acc.shape[1]
    n_chunk = d // 2 // LANES
    blk_words = ROW_ALIGN * n_chunk

    def in_copy(base_row, blk):
        src = xs_hbm.at[pl.ds(pl.multiple_of((base_row + blk * ROW_ALIGN) * n_chunk, blk_words), blk_words)]
        return pltpu.make_async_copy(src, stage.at[pl.ds(pl.multiple_of(blk * blk_words, blk_words), blk_words)],
                                     sem_in)

    def start_in(base_row, n_blk):
        def issue(blk, c):
            in_copy(base_row, blk).start()
            return c

        lax.fori_loop(0, n_blk, issue, 0)

    def out_copy(blk):
        src = acc.at[pl.ds(pl.multiple_of(blk * ROW_ALIGN, ROW_ALIGN), ROW_ALIGN)]
        dst = ys_hbm.at[pl.ds(pl.multiple_of(r0 + blk * ROW_ALIGN, ROW_ALIGN), ROW_ALIGN)]
        return pltpu.make_async_copy(src, dst, sem_out)

    def drain():
        @pl.when(pending[0] > 0)
        def _():
            rows = pl.ds(0, pending[0] * ROW_ALIGN)
            pltpu.make_async_copy(acc.at[rows], ys_hbm.at[rows], sem_out).wait()
            pending[0] = 0

    @pl.when((i == 0) & (j == 0))
    def _():
        pending[0] = 0
        start_in(r0, n_align)

    @pl.when(active & (j == 0))
    def _():
        all_rows = pl.ds(0, n_rows * n_chunk)
        pltpu.make_async_copy(xs_hbm.at[all_rows], stage.at[all_rows], sem_in).wait()
        drain()

        def unpack(rb, c):
            rows = pl.ds(pl.multiple_of(rb * ROW_ALIGN, ROW_ALIGN), ROW_ALIGN)
            for ch in range(n_chunk):
                words = stage[pl.ds(rb * (ROW_ALIGN * n_chunk) + ch, ROW_ALIGN, stride=n_chunk), :]
                hi, lo = _unpack_pairs(words)
                xb16[rows, ch * LANES:(ch + 1) * LANES] = hi
                xb16[rows, d // 2 + ch * LANES:d // 2 + (ch + 1) * LANES] = lo
            acc[rows, :] = jnp.broadcast_to(bd_ref[0], (ROW_ALIGN, d))
            return c

        lax.fori_loop(0, n_align, unpack, 0)
        start_in(r0_next, blocks_next)

    def trip(row0, n_blk):
        rows = pl.ds(row0, n_blk * ROW_ALIGN)
        xb = xb16[rows, :]
        g = jnp.dot(xb, wg_ref[0].astype(BF16), preferred_element_type=F32) + bg_ref[0]
        u = jnp.dot(xb, wu_ref[0].astype(BF16), preferred_element_type=F32) + bu_ref[0]
        g = jnp.minimum(g, SWIGLU_LIMIT)
        u = jnp.clip(u, -SWIGLU_LIMIT, SWIGLU_LIMIT)
        a = (u + 1.0) * (g * jax.nn.sigmoid(SWIGLU_ALPHA * g))
        acc[rows, :] += jnp.dot(a.astype(BF16), wd_ref[0].astype(BF16), preferred_element_type=F32)

        @pl.when(j == nj - 1)
        def _():
            for s in range(n_blk):
                out_copy(row0 // ROW_ALIGN + s).start()
            pending[0] = pending[0] + n_blk

    @pl.when(active)
    def _():
        def main_trip(rb, c):
            trip(pl.multiple_of(rb * ROW_BLOCK, ROW_BLOCK), per_trip)
            return c

        lax.fori_loop(0, n_main, main_trip, 0)

        def tail_trip(t, c):
            trip(pl.multiple_of((n_main * per_trip + t) * ROW_ALIGN, ROW_ALIGN), 1)
            return c

        lax.fori_loop(0, n_align - n_main * per_trip, tail_trip, 0)

    @pl.when((i == pl.num_programs(0) - 1) & (j == nj - 1))
    def _():
        drain()
        acc[0:ROW_ALIGN, :] = jnp.zeros((ROW_ALIGN, d), F32)
        used = n_items[1]

        def tail_copy(t):
            dst = ys_hbm.at[pl.ds(pl.multiple_of((used + t) * ROW_ALIGN, ROW_ALIGN), ROW_ALIGN)]
            return pltpu.make_async_copy(acc.at[pl.ds(0, ROW_ALIGN)], dst, sem_out)

        def issue(t, c):
            tail_copy(t).start()
            return c

        n_tail = ys_hbm.shape[0] // ROW_ALIGN - used
        lax.fori_loop(0, n_tail, issue, 0)

        def wait(t, c):
            tail_copy(t).wait()
            return c

        lax.fori_loop(0, n_tail, wait, 0)


def _experts(xs, w_gate, b_gate, w_up, b_up, w_down, b_down, meta, *, n_rows, th):
    item_e, item_r0, item_nb, n_items = meta
    n_exp, d, dh = w_gate.shape
    nj = dh // th
    n_item_slots = item_e.shape[0]

    def w_col(i, j, item_e, item_r0, item_nb, n_items):
        return (item_e[i], 0, jnp.where(i < n_items[0], j, nj - 1))

    def w_row(i, j, item_e, item_r0, item_nb, n_items):
        return (item_e[i], jnp.where(i < n_items[0], j, nj - 1), 0)

    def b_all(i, j, item_e, item_r0, item_nb, n_items):
        return (item_e[i], 0, 0)

    grid_spec = pltpu.PrefetchScalarGridSpec(
        num_scalar_prefetch=4,
        grid=(n_item_slots, nj),
        in_specs=[
            pl.BlockSpec(memory_space=pl.ANY),
            pl.BlockSpec((1, d, th), w_col),
            pl.BlockSpec((1, 1, th), w_col),
            pl.BlockSpec((1, d, th), w_col),
            pl.BlockSpec((1, 1, th), w_col),
            pl.BlockSpec((1, th, d), w_row),
            pl.BlockSpec((1, 1, d), b_all),
        ],
        out_specs=pl.BlockSpec(memory_space=pl.ANY),
        scratch_shapes=[
            pltpu.VMEM((ITEM_ROWS * (d // 2 // LANES), LANES), jnp.uint32),
            pltpu.VMEM((ITEM_ROWS, d), BF16),
            pltpu.VMEM((ITEM_ROWS, d), F32),
            pltpu.SMEM((1,), jnp.int32),
            pltpu.SemaphoreType.DMA,
            pltpu.SemaphoreType.DMA,
        ],
    )
    return pl.pallas_call(
        functools.partial(_expert_kernel, nj=nj),
        grid_spec=grid_spec,
        out_shape=jax.ShapeDtypeStruct((n_rows, d), F32),
        compiler_params=_cparams(("arbitrary", "arbitrary")),
        name="experts",
    )(item_e, item_r0, item_nb, n_items,
      xs, w_gate, b_gate.reshape(n_exp, 1, dh), w_up, b_up.reshape(n_exp, 1, dh), w_down,
      b_down.reshape(n_exp, 1, d))


def _routing_metadata(top_idx, top_rank, counts, n_tok):
    n_assign = n_tok * TOP_K
    flat_e = top_idx.reshape(n_assign)
    rank = top_rank.reshape(n_assign)
    padded = (counts + ROW_ALIGN - 1) // ROW_ALIGN * ROW_ALIGN
    pend = jnp.cumsum(padded)
    pstart = pend - padded
    dest = (pstart[flat_e] + rank).astype(jnp.int32)
    n_rows = (n_assign // ROW_ALIGN + N_EXPERTS) * ROW_ALIGN
    flat_tok = jnp.arange(n_assign, dtype=jnp.int32) // TOP_K
    row_tok = jnp.zeros((n_rows,), jnp.int32).at[dest].set(
        flat_tok, unique_indices=True, mode="promise_in_bounds")
    n_chunks = (padded + ITEM_ROWS - 1) // ITEM_ROWS
    cend = jnp.cumsum(n_chunks)
    cstart = cend - n_chunks
    n_slots = N_EXPERTS + n_assign // ITEM_ROWS + 1
    slot = jnp.arange(n_slots, dtype=jnp.int32)
    n_items = cend[-1]
    last_e = jnp.max(jnp.where(n_chunks > 0, jnp.arange(N_EXPERTS), 0))
    e_of = jnp.minimum(jnp.searchsorted(cend, slot, side='right'), N_EXPERTS - 1)
    live = slot < n_items
    item_e = jnp.where(live, e_of, last_e).astype(jnp.int32)
    chunk = slot - cstart[e_of]
    item_r0 = jnp.where(live, pstart[e_of] + chunk * ITEM_ROWS, 0).astype(jnp.int32)
    item_rows = jnp.clip(padded[e_of] - chunk * ITEM_ROWS, 0, ITEM_ROWS)
    item_nb = jnp.where(live, item_rows // ROW_ALIGN, 0).astype(jnp.int32)
    counts_blk = jnp.stack([n_items, pend[-1] // ROW_ALIGN]).astype(jnp.int32)
    meta = (item_e, item_r0, item_nb, counts_blk)
    return meta, row_tok, dest, n_rows


def _combine_kernel(dest, ys_hbm, x2_ref, gate_ref, g_ref, o_ref, buf, sem, *, tm):
    i = pl.program_id(0)
    n = pl.num_programs(0)

    def row_copy(tile, slot, t, kk):
        r = dest[(tile * tm + t) * TOP_K + kk]
        return pltpu.make_async_copy(ys_hbm.at[pl.ds(r, 1)], buf.at[slot, kk, pl.ds(t, 1)], sem.at[slot])

    def issue_tile(tile, slot):
        def issue(t, c):
            for kk in range(TOP_K):
                row_copy(tile, slot, t, kk).start()
            return c

        lax.fori_loop(0, tm, issue, 0, unroll=8)

    slot = i % 2

    @pl.when(i == 0)
    def _():
        issue_tile(0, 0)

    @pl.when(i + 1 < n)
    def _():
        issue_tile(i + 1, 1 - slot)

    for kk in range(TOP_K):
        pltpu.make_async_copy(ys_hbm.at[pl.ds(0, tm)], buf.at[slot, kk], sem.at[slot]).wait()

    gates = gate_ref[...]
    y = x2_ref[...]
    for kk in range(TOP_K):
        y = y + gates[:, kk:kk + 1] * buf[slot, kk]
    o_ref[...] = (y * lax.rsqrt(jnp.mean(y * y, axis=-1, keepdims=True) + NORM_EPS)) * g_ref[...]


def _combine(dest, ys, x2, gates, g_final, *, tm):
    m, d = x2.shape
    grid_spec = pltpu.PrefetchScalarGridSpec(
        num_scalar_prefetch=1,
        grid=(m // tm,),
        in_specs=[
            pl.BlockSpec(memory_space=pl.ANY),
            pl.BlockSpec((tm, d), lambda i, dest: (i, 0)),
            pl.BlockSpec((tm, LANES), lambda i, dest: (i, 0)),
            pl.BlockSpec((1, d), lambda i, dest: (0, 0)),
        ],
        out_specs=pl.BlockSpec((tm, d), lambda i, dest: (i, 0)),
        scratch_shapes=[pltpu.VMEM((2, TOP_K, tm, d), F32), pltpu.SemaphoreType.DMA((2,))],
    )
    return pl.pallas_call(
        functools.partial(_combine_kernel, tm=tm),
        grid_spec=grid_spec,
        out_shape=jax.ShapeDtypeStruct((m, d), F32),
        compiler_params=_cparams(("arbitrary",)),
        name="combine",
    )(dest, ys, x2, gates, g_final)


def _layer(x, attn_norm, w_in, dil_out_norm, sb_out_norm, w_out, ffn_norm,
           w_router, b_router, w_gate, b_gate, w_up, b_up, w_down, b_down, *, tiles):
    b, seq, d = x.shape
    m = b * seq
    half_width = d // 2
    x2d = x.reshape(m, d)
    cos_t, sin_t = _rope_tables(seq)
    proj = _inproj(x2d, attn_norm.reshape(1, d), w_in, cos_t, sin_t,
                   seq=seq, tm=tiles["in_tm"], tn=tiles["in_tn"])
    proj3 = proj.reshape(b, seq, 3 * d)
    o_d = _dilated_attention(proj3, half_width=half_width).reshape(m, half_width)
    o_s = _stickbreak_attention(proj3, half_width=half_width, qt=tiles["sb_qt"]).reshape(m, half_width)
    x2, h2, idx_l, gate_l, rank_l, counts = _outproj_router(
        o_d, o_s, x2d, w_out, dil_out_norm.reshape(1, half_width), sb_out_norm.reshape(1, half_width),
        ffn_norm.reshape(1, d), w_router, b_router.reshape(1, N_EXPERTS),
        tm=tiles["out_tm"], tk=tiles["out_tk"])
    meta, row_tok, dest, n_rows = _routing_metadata(idx_l[:, :TOP_K], rank_l[:, :TOP_K], counts[0], m)
    xs = _dispatch(h2, row_tok, rows_blk=tiles["disp_rows"], n_chunk=d // 2 // LANES)
    ys = _experts(xs, w_gate, b_gate, w_up, b_up, w_down, b_down, meta, n_rows=n_rows, th=tiles["moe_th"])
    return x2, ys, dest, gate_l


_TILES = dict(in_tm=1024, in_tn=512, sb_qt=512, out_tm=512, out_tk=512, disp_rows=1024, moe_th=512, cmb_tm=128)


def kernel(x, attn_norm, w_in, dil_out_norm, sb_out_norm, w_out, ffn_norm, w_router, b_router,
           w_gate, b_gate, w_up, b_up, w_down, b_down, final_norm):
    depth = attn_norm.shape[0]
    b, seq, d = x.shape
    for l in range(depth):
        x2, ys, dest, gate_l = _layer(
            x, attn_norm[l], w_in[l], dil_out_norm[l], sb_out_norm[l], w_out[l], ffn_norm[l],
            w_router[l], b_router[l], w_gate[l], b_gate[l], w_up[l], b_up[l], w_down[l], b_down[l],
            tiles=_TILES)
        assert depth == 1
        x = _combine(dest, ys, x2, gate_l, final_norm.reshape(1, d), tm=_TILES["cmb_tm"])
    return x.reshape(b, seq, d)
```

```python
import functools
import math

import jax
import jax.numpy as jnp
from jax import lax
from jax.experimental import pallas as pl
from jax.experimental.pallas import tpu as pltpu

F32 = jnp.float32
BF16 = jnp.bfloat16

HEAD_DIM = 64
LANES = 128
BLOCK = 128
DIL_PATTERNS = ((128, 1), (512, 4), (2048, 16))
ROPE_THETA = 10000.0
N_EXPERTS = 32
TOP_K = 4
SWIGLU_LIMIT = 7.0
SWIGLU_ALPHA = 1.702
NORM_EPS = 1e-5
Q_SCALE = HEAD_DIM ** -0.5 * math.log2(math.e)
DIL_UNROLL = 16
ROW_ALIGN = 128
ROW_BLOCK = 512
ITEM_ROWS = 1280
VMEM_LIMIT = 56 * 1024 * 1024


def _cparams(sem):
    return pltpu.CompilerParams(dimension_semantics=sem, vmem_limit_bytes=VMEM_LIMIT)


def _inproj_kernel(x_ref, g_ref, w_ref, cos_ref, sin_ref, o_ref, h_scr, *, tn, half_width):
    j = pl.program_id(1)

    @pl.when(j == 0)
    def _():
        x = x_ref[...]
        y = x * lax.rsqrt(jnp.mean(x * x, axis=-1, keepdims=True) + NORM_EPS)
        h_scr[...] = (y * g_ref[...]).astype(BF16)

    acc = jnp.dot(h_scr[...], w_ref[...].astype(BF16), preferred_element_type=F32)
    section = (j * tn) // half_width
    scale = jnp.where((section == 0) | (section == 3), Q_SCALE, 1.0).astype(F32)

    @pl.when(section < 2)
    def _():
        lane = lax.broadcasted_iota(jnp.int32, (acc.shape[0], LANES), 1)
        first_half = (lane % HEAD_DIM) < (HEAD_DIM // 2)
        c = cos_ref[...]
        s = sin_ref[...]
        for g in range(tn // LANES):
            xg = acc[:, g * LANES:(g + 1) * LANES]
            partner = jnp.where(first_half,
                                pltpu.roll(xg, LANES - HEAD_DIM // 2, 1),
                                pltpu.roll(xg, HEAD_DIM // 2, 1))
            o_ref[:, g * LANES:(g + 1) * LANES] = ((xg * c + partner * s) * scale).astype(BF16)

    @pl.when(section >= 2)
    def _():
        o_ref[...] = (acc * scale).astype(BF16)


def _inproj(x2d, g, w_in, cos_t, sin_t, *, seq, tm, tn):
    m, d = x2d.shape
    n = w_in.shape[1]
    half_width = d // 2
    assert half_width % tn == 0 and tn % LANES == 0 and m % tm == 0 and seq % tm == 0
    pos_tiles = seq // tm
    return pl.pallas_call(
        functools.partial(_inproj_kernel, tn=tn, half_width=half_width),
        grid=(m // tm, n // tn),
        in_specs=[
            pl.BlockSpec((tm, d), lambda i, j: (i, 0)),
            pl.BlockSpec((1, d), lambda i, j: (0, 0)),
            pl.BlockSpec((d, tn), lambda i, j: (0, j)),
            pl.BlockSpec((tm, LANES), lambda i, j: (i % pos_tiles, 0)),
            pl.BlockSpec((tm, LANES), lambda i, j: (i % pos_tiles, 0)),
        ],
        out_specs=pl.BlockSpec((tm, tn), lambda i, j: (i, j)),
        out_shape=jax.ShapeDtypeStruct((m, n), BF16),
        scratch_shapes=[pltpu.VMEM((tm, d), BF16)],
        compiler_params=_cparams(("arbitrary", "arbitrary")),
        name="inproj",
    )(x2d, g, w_in, cos_t, sin_t)


def _rope_tables(seq):
    half = HEAD_DIM // 2
    inv_freq = 1.0 / (ROPE_THETA ** (jnp.arange(half, dtype=F32) * 2.0 / HEAD_DIM))
    ang = jnp.arange(seq, dtype=F32)[:, None] * inv_freq[None, :]
    cos = jnp.cos(ang)
    sin = jnp.sin(ang)
    reps = LANES // HEAD_DIM
    cos_t = jnp.tile(jnp.concatenate([cos, cos], axis=1), (1, reps))
    sin_t = jnp.tile(jnp.concatenate([-sin, sin], axis=1), (1, reps))
    return cos_t, sin_t


def _dil_kernel(q_ref, k_ref, v_ref, o_ref, f32_scr, qp, kp, vp, operm, lperm, onat, lnat, *, seq):
    n_blocks = seq // BLOCK
    lane_q = lax.broadcasted_iota(jnp.int32, (BLOCK, LANES), 1)
    head_a_q = lane_q < HEAD_DIM
    zero_b = jnp.zeros((), BF16)

    def window_consts(n_kb):
        width = n_kb * BLOCK
        head_a_k = lax.broadcasted_iota(jnp.int32, (width, LANES), 1) < HEAD_DIM
        qi = lax.broadcasted_iota(jnp.int32, (BLOCK, width), 0)
        ki = lax.broadcasted_iota(jnp.int32, (BLOCK, width), 1)
        dist = qi + (width - BLOCK) - ki
        band = (dist >= 0) & (dist <= BLOCK)
        cur_only = ki >= width - BLOCK
        head_ones = jnp.concatenate([jnp.where(head_a_k, 1.0, 0.0), jnp.where(head_a_k, 0.0, 1.0)],
                                    axis=0).astype(BF16)
        return width, head_a_k, band, cur_only, head_ones

    f32_scr[0] = q_ref[0].astype(F32)
    f32_scr[1] = k_ref[0].astype(F32)
    f32_scr[2] = v_ref[0].astype(F32)
    kp[0:BLOCK, :] = jnp.zeros((BLOCK, LANES), BF16)
    vp[0:BLOCK, :] = jnp.zeros((BLOCK, LANES), BF16)

    for p, (window, dil) in enumerate(DIL_PATTERNS):
        assert window // dil == BLOCK
        length = seq // dil
        per_stream = length // BLOCK
        assert length % BLOCK == 0
        if dil == 1:
            qp[...] = q_ref[0]
            kp[BLOCK:, :] = k_ref[0]
            vp[BLOCK:, :] = v_ref[0]
        else:
            for r in range(dil):
                rows = pl.ds(r, length, stride=dil)
                qp[r * length:(r + 1) * length, :] = f32_scr[0, rows, :].astype(BF16)
                kp[BLOCK + r * length:BLOCK + (r + 1) * length, :] = f32_scr[1, rows, :].astype(BF16)
                vp[BLOCK + r * length:BLOCK + (r + 1) * length, :] = f32_scr[2, rows, :].astype(BF16)
        o_dst = onat.at[p] if dil == 1 else operm
        l_dst = lnat.at[p] if dil == 1 else lperm

        consts = window_consts(1 if per_stream == 1 else 2)

        def block_body(bi, carry, per_stream=per_stream, o_dst=o_dst, l_dst=l_dst, consts=consts):
            width, head_a_k, band, cur_only, head_ones = consts
            row0 = pl.multiple_of(bi * BLOCK, BLOCK)
            qb = qp[pl.ds(row0, BLOCK), :]
            win0 = pl.multiple_of(row0 + 2 * BLOCK - width, BLOCK)
            kw = kp[pl.ds(win0, width), :]
            vw = vp[pl.ds(win0, width), :]
            has_prev = (bi % per_stream) != 0
            valid = band & (cur_only | has_prev)
            kk = jnp.concatenate([jnp.where(head_a_k, kw, zero_b), jnp.where(head_a_k, zero_b, kw)], axis=0)
            s2 = lax.dot_general(qb, kk, (((1,), (1,)), ((), ())), preferred_element_type=F32)
            ps, ms = [], []
            for h in range(2):
                s = jnp.where(valid, s2[:, h * width:(h + 1) * width], -jnp.inf)
                m = jnp.max(s, axis=-1, keepdims=True)
                ps.append(jnp.exp2(s - m).astype(BF16))
                ms.append(m)
            vv = jnp.concatenate([jnp.where(head_a_k, vw, zero_b), jnp.where(head_a_k, zero_b, vw)], axis=0)
            o_den = jnp.dot(jnp.concatenate(ps, axis=1), jnp.concatenate([vv, head_ones], axis=1),
                            preferred_element_type=F32)
            o = o_den[:, :LANES]
            den = o_den[:, LANES:]
            o_dst[pl.ds(row0, BLOCK), :] = o / den
            l_dst[pl.ds(row0, BLOCK), :] = jnp.where(head_a_q, ms[0], ms[1]) + jnp.log2(den)
            return carry

        lax.fori_loop(0, n_blocks, block_body, 0, unroll=DIL_UNROLL)
        if dil != 1:
            for r in range(dil):
                rows = pl.ds(r, length, stride=dil)
                onat[p, rows, :] = operm[r * length:(r + 1) * length, :]
                lnat[p, rows, :] = lperm[r * length:(r + 1) * length, :]

    def mix_body(c, carry):
        rows = pl.ds(pl.multiple_of(c * BLOCK, BLOCK), BLOCK)
        ls = [lnat[p, rows, :] for p in range(len(DIL_PATTERNS))]
        m = functools.reduce(jnp.maximum, ls)
        es = [jnp.exp2(l - m) for l in ls]
        num = functools.reduce(lambda a, b: a + b, [e * onat[p, rows, :] for p, e in enumerate(es)])
        o_ref[0, rows, :] = num / functools.reduce(lambda a, b: a + b, es)
        return carry

    lax.fori_loop(0, n_blocks, mix_body, 0, unroll=DIL_UNROLL)


def _dilated_attention(proj3, *, half_width):
    b, seq, _ = proj3.shape
    pairs = half_width // LANES
    n_pat = len(DIL_PATTERNS)
    spec = lambda off: pl.BlockSpec((1, seq, LANES), lambda bi, c, off=off: (bi, 0, off + c))
    return pl.pallas_call(
        functools.partial(_dil_kernel, seq=seq),
        grid=(b, pairs),
        in_specs=[spec(0), spec(pairs), spec(2 * pairs)],
        out_specs=pl.BlockSpec((1, seq, LANES), lambda bi, c: (bi, 0, c)),
        out_shape=jax.ShapeDtypeStruct((b, seq, half_width), F32),
        scratch_shapes=[
            pltpu.VMEM((3, seq, LANES), F32),
            pltpu.VMEM((seq, LANES), BF16),
            pltpu.VMEM((seq + BLOCK, LANES), BF16),
            pltpu.VMEM((seq + BLOCK, LANES), BF16),
            pltpu.VMEM((seq, LANES), F32),
            pltpu.VMEM((seq, LANES), F32),
            pltpu.VMEM((n_pat, seq, LANES), F32),
            pltpu.VMEM((n_pat, seq, LANES), F32),
        ],
        compiler_params=_cparams(("arbitrary", "arbitrary")),
        name="dilated_attention",
    )(proj3, proj3, proj3)


def _sb_kernel(q_ref, k_ref, v_ref, o_ref, kk_scr, vv_scr, tri_scr, acc_scr, carry_scr, *, seq, qt):
    n_blocks = seq // BLOCK
    per_tile = qt // BLOCK
    lane_k = lax.broadcasted_iota(jnp.int32, (BLOCK, LANES), 1)
    head_a = lane_k < HEAD_DIM
    zero_b = jnp.zeros((), BF16)

    def prep(jb, carry):
        rows = pl.ds(pl.multiple_of(jb * BLOCK, BLOCK), BLOCK)
        kb = k_ref[0, rows, :]
        vb = v_ref[0, rows, :]
        kk_scr[jb, 0:BLOCK, :] = jnp.where(head_a, kb, zero_b)
        kk_scr[jb, BLOCK:, :] = jnp.where(head_a, zero_b, kb)
        vv_scr[jb, 0:BLOCK, :] = jnp.where(head_a, vb, zero_b)
        vv_scr[jb, BLOCK:, :] = jnp.where(head_a, zero_b, vb)
        return carry

    lax.fori_loop(0, n_blocks, prep, 0, unroll=True)

    jj = lax.broadcasted_iota(jnp.int32, (2 * BLOCK, 2 * BLOCK), 0) % BLOCK
    ss = lax.broadcasted_iota(jnp.int32, (2 * BLOCK, 2 * BLOCK), 1)
    tri_scr[...] = jnp.where((ss >= BLOCK) | (jj > ss), 1.0, 0.0).astype(BF16)

    def key_block(ti, qb, jb, key_off):
        rs = slice(0 if key_off is None else key_off, qt)
        z = lax.dot_general(qb[rs], kk_scr[jb], (((1,), (1,)), ((), ())), preferred_element_type=F32)
        neg_abs = lax.bitcast_convert_type(lax.bitcast_convert_type(z, jnp.uint32) | jnp.uint32(0x80000000), F32)
        ls = jnp.minimum(z, 0.0) - jnp.log2(1.0 + jnp.exp2(neg_abs))
        lr = ls - z
        if key_off is not None:
            n = qt - key_off
            causal = (lax.broadcasted_iota(jnp.int32, (n, 2 * BLOCK), 1) % BLOCK
                      < lax.broadcasted_iota(jnp.int32, (n, 2 * BLOCK), 0))
            lr = jnp.where(causal, lr, 0.0)
        hi = lr.astype(BF16)
        lo = (lr - hi.astype(F32)).astype(BF16)
        tri = tri_scr[...]
        a_heads = []
        for h in range(2):
            sl = slice(h * BLOCK, (h + 1) * BLOCK)
            lhs = jnp.concatenate([hi[:, sl], lo[:, sl]], axis=1)
            cs_tot = jnp.dot(lhs, tri, preferred_element_type=F32)
            carry = carry_scr[ti, rs, sl]
            a = jnp.exp2(ls[:, sl] + cs_tot[:, :BLOCK] + carry)
            if key_off is not None:
                a = jnp.where(causal[:, :BLOCK], a, 0.0)
            carry_scr[ti, rs, sl] = carry + cs_tot[:, BLOCK:]
            a_heads.append(a.astype(BF16))
        acc_scr[ti, rs, :] += jnp.dot(jnp.concatenate(a_heads, axis=1), vv_scr[jb], preferred_element_type=F32)

    for i in range(seq // qt):
        rows = slice(i * qt, (i + 1) * qt)
        qb = q_ref[0, rows, :]
        acc_scr[i] = jnp.zeros(acc_scr.shape[1:], F32)
        carry_scr[i] = jnp.zeros(carry_scr.shape[1:], F32)
        for jb in reversed(range((i + 1) * per_tile)):
            in_tile = jb - i * per_tile
            key_block(i, qb, jb, in_tile * BLOCK if in_tile >= 0 else None)
        o_ref[0, rows, :] = acc_scr[i]


def _stickbreak_attention(proj3, *, half_width, qt):
    b, seq, _ = proj3.shape
    pairs = half_width // LANES
    n_blocks = seq // BLOCK
    assert seq % qt == 0 and qt % BLOCK == 0
    spec = lambda off: pl.BlockSpec((1, seq, LANES), lambda bi, c, off=off: (bi, 0, off + c))
    return pl.pallas_call(
        functools.partial(_sb_kernel, seq=seq, qt=qt),
        grid=(b, pairs),
        in_specs=[spec(3 * pairs), spec(4 * pairs), spec(5 * pairs)],
        out_specs=pl.BlockSpec((1, seq, LANES), lambda bi, c: (bi, 0, c)),
        out_shape=jax.ShapeDtypeStruct((b, seq, half_width), F32),
        scratch_shapes=[
            pltpu.VMEM((n_blocks, 2 * BLOCK, LANES), BF16),
            pltpu.VMEM((n_blocks, 2 * BLOCK, LANES), BF16),
            pltpu.VMEM((2 * BLOCK, 2 * BLOCK), BF16),
            pltpu.VMEM((seq // qt, qt, LANES), F32),
            pltpu.VMEM((seq // qt, qt, 2 * BLOCK), F32),
        ],
        compiler_params=_cparams(("arbitrary", "arbitrary")),
        name="stickbreak_attention",
    )(proj3, proj3, proj3)


def _split_bf16(x):
    hi = x.astype(BF16)
    return hi, (x - hi.astype(F32)).astype(BF16)


def _outproj_kernel(od_ref, os_ref, x_ref, w_ref, gd_ref, gs_ref, gf_ref, wr_ref, br_ref,
                    x2_ref, h2_ref, idx_ref, gate_ref, rank_ref, cnt_ref, cnt_scr, *, half_width):
    @pl.when(pl.program_id(0) == 0)
    def _():
        cnt_scr[...] = jnp.zeros_like(cnt_scr)

    def group(a_ref, g_ref, w_rows):
        a = a_ref[...]
        r = lax.rsqrt(jnp.mean(a * a, axis=-1, keepdims=True) + NORM_EPS)
        return jnp.dot((a * g_ref[...]).astype(BF16), w_ref[w_rows, :].astype(BF16), preferred_element_type=F32) * r

    def residual_norm_route():
        x2 = x_ref[...] + (group(od_ref, gd_ref, slice(0, half_width))
                           + group(os_ref, gs_ref, slice(half_width, 2 * half_width)))
        x2_ref[...] = x2
        h2 = (x2 * lax.rsqrt(jnp.mean(x2 * x2, axis=-1, keepdims=True) + NORM_EPS)) * gf_ref[...]
        packed = _pack_pairs(h2)
        n_chunk = packed.shape[1] // LANES
        for ch in range(n_chunk):
            h2_ref[pl.ds(ch, packed.shape[0], stride=n_chunk), :] = packed[:, ch * LANES:(ch + 1) * LANES]
        h_hi, h_lo = _split_bf16(h2)
        w_hi, w_lo = _split_bf16(wr_ref[...])
        hw = jnp.dot(h_hi, jnp.concatenate([w_hi, w_lo], axis=1), preferred_element_type=F32)
        logits = (hw[:, :N_EXPERTS] + hw[:, N_EXPERTS:]
                  + jnp.dot(h_lo, w_hi, preferred_element_type=F32)) + br_ref[...]
        tm = logits.shape[0]
        e_iota = lax.broadcasted_iota(jnp.int32, logits.shape, 1)
        out_lane = lax.broadcasted_iota(jnp.int32, (tm, LANES), 1)
        idx_out = jnp.zeros((tm, LANES), jnp.int32)
        val_out = jnp.full((tm, LANES), -jnp.inf, F32)
        top0 = None
        picks = []
        for kk in range(TOP_K):
            m = jnp.max(logits, axis=-1, keepdims=True)
            sel = jnp.min(jnp.where(logits == m, e_iota, N_EXPERTS), axis=-1, keepdims=True)
            idx_out = jnp.where(out_lane == kk, sel, idx_out)
            val_out = jnp.where(out_lane == kk, m, val_out)
            picks.append(e_iota == sel)
            logits = jnp.where(picks[-1], -jnp.inf, logits)
            if kk == 0:
                top0 = m
        ex = jnp.exp(val_out - top0)
        gate_ref[...] = ex / jnp.sum(ex, axis=-1, keepdims=True)
        idx_ref[...] = idx_out

        chosen = functools.reduce(jnp.logical_or, picks)
        t_row = lax.broadcasted_iota(jnp.int32, (tm, tm), 0)
        t_col = lax.broadcasted_iota(jnp.int32, (tm, tm), 1)
        before = jnp.where(t_col < t_row, 1.0, 0.0).astype(BF16)
        multi_hot = jnp.where(chosen, 1.0, 0.0)
        prior = jnp.dot(before, multi_hot.astype(BF16), preferred_element_type=F32) + cnt_scr[...]
        rank_out = jnp.zeros((tm, LANES), jnp.int32)
        for kk in range(TOP_K):
            r = jnp.sum(jnp.where(picks[kk], prior, 0.0), axis=-1, keepdims=True)
            rank_out = jnp.where(out_lane == kk, r.astype(jnp.int32), rank_out)
        rank_ref[...] = rank_out
        cnt_scr[...] += jnp.sum(multi_hot, axis=0, keepdims=True)
        cnt_ref[...] = cnt_scr[...].astype(jnp.int32)

    residual_norm_route()


def _outproj_router(o_d, o_s, x2d, w_out, g_d, g_s, g_f, w_router, b_router, *, tm):
    m, d = x2d.shape
    half_width = d // 2
    assert m % tm == 0
    return pl.pallas_call(
        functools.partial(_outproj_kernel, half_width=half_width),
        grid=(m // tm,),
        in_specs=[
            pl.BlockSpec((tm, half_width), lambda i: (i, 0)),
            pl.BlockSpec((tm, half_width), lambda i: (i, 0)),
            pl.BlockSpec((tm, d), lambda i: (i, 0)),
            pl.BlockSpec((d, d), lambda i: (0, 0), pipeline_mode=pl.Buffered(1)),
            pl.BlockSpec((1, half_width), lambda i: (0, 0)),
            pl.BlockSpec((1, half_width), lambda i: (0, 0)),
            pl.BlockSpec((1, d), lambda i: (0, 0)),
            pl.BlockSpec((d, N_EXPERTS), lambda i: (0, 0)),
            pl.BlockSpec((1, N_EXPERTS), lambda i: (0, 0)),
        ],
        out_specs=[
            pl.BlockSpec((tm, d), lambda i: (i, 0)),
            pl.BlockSpec((tm * (d // 2 // LANES), LANES), lambda i: (i, 0)),
            pl.BlockSpec((tm, LANES), lambda i: (i, 0)),
            pl.BlockSpec((tm, LANES), lambda i: (i, 0)),
            pl.BlockSpec((tm, LANES), lambda i: (i, 0)),
            pl.BlockSpec((1, N_EXPERTS), lambda i: (0, 0)),
        ],
        out_shape=[
            jax.ShapeDtypeStruct((m, d), F32),
            jax.ShapeDtypeStruct((m * (d // 2 // LANES), LANES), jnp.uint32),
            jax.ShapeDtypeStruct((m, LANES), jnp.int32),
            jax.ShapeDtypeStruct((m, LANES), F32),
            jax.ShapeDtypeStruct((m, LANES), jnp.int32),
            jax.ShapeDtypeStruct((1, N_EXPERTS), jnp.int32),
        ],
        scratch_shapes=[pltpu.VMEM((1, N_EXPERTS), F32)],
        compiler_params=_cparams(("arbitrary",)),
        name="outproj_router",
    )(o_d, o_s, x2d, w_out, g_d, g_s, g_f, w_router, b_router)


def _unpack_pairs(words):
    hi = lax.bitcast_convert_type(words & jnp.uint32(0xFFFF0000), F32)
    lo = lax.bitcast_convert_type(words << 16, F32)
    return hi.astype(BF16), lo.astype(BF16)


def _pack_pairs(x):
    c = x.shape[1] // 2
    as_bits = lambda v: lax.bitcast_convert_type(v.astype(BF16).astype(F32), jnp.uint32)
    return as_bits(x[:, :c]) | (as_bits(x[:, c:]) >> 16)


def _dispatch_kernel(row_tok, h_hbm, o_ref, h_vmem, sem, *, rows_blk, n_chunk):
    i = pl.program_id(0)

    @pl.when(i == 0)
    def _():
        cp = pltpu.make_async_copy(h_hbm, h_vmem, sem)
        cp.start()
        cp.wait()

    base = i * rows_blk

    def body(r, c):
        tok = row_tok[base + r]
        src = pl.ds(pl.multiple_of(tok * n_chunk, n_chunk), n_chunk)
        o_ref[pl.ds(pl.multiple_of(r * n_chunk, n_chunk), n_chunk), :] = h_vmem[src, :]
        return c

    lax.fori_loop(0, rows_blk, body, 0, unroll=8)


def _dispatch(hp, row_tok, *, rows_blk, n_chunk):
    n_rows = row_tok.shape[0]
    assert n_rows % rows_blk == 0
    grid_spec = pltpu.PrefetchScalarGridSpec(
        num_scalar_prefetch=1,
        grid=(n_rows // rows_blk,),
        in_specs=[pl.BlockSpec(memory_space=pl.ANY)],
        out_specs=pl.BlockSpec((rows_blk * n_chunk, LANES), lambda i, row_tok: (i, 0)),
        scratch_shapes=[pltpu.VMEM(hp.shape, hp.dtype), pltpu.SemaphoreType.DMA],
    )
    return pl.pallas_call(
        functools.partial(_dispatch_kernel, rows_blk=rows_blk, n_chunk=n_chunk),
        grid_spec=grid_spec,
        out_shape=jax.ShapeDtypeStruct((n_rows * n_chunk, LANES), hp.dtype),
        compiler_params=_cparams(("arbitrary",)),
        name="dispatch",
    )(row_tok, hp)


def _expert_kernel(item_e, item_r0, item_nb, n_items,
                   xs_hbm, wg_ref, bg_ref, wu_ref, bu_ref, wd_ref, bd_ref,
                   ys_hbm, stage, xb16, acc, pending, sem_in, sem_out, *, nj):
    i = pl.program_id(0)
    j = pl.program_id(1)
    n_live = n_items[0]
    active = i < n_live
    r0 = item_r0[i]
    n_align = item_nb[i]
    n_rows = n_align * ROW_ALIGN
    per_trip = ROW_BLOCK // ROW_ALIGN
    n_main = n_align // per_trip
    nxt = jnp.minimum(i + 1, pl.num_programs(0) - 1)
    r0_next = item_r0[nxt]
    blocks_next = jnp.where(i + 1 < n_live, item_nb[nxt], 0)
    d = acc.shape[1]
    n_chunk = d // 2 // LANES
    blk_words = ROW_ALIGN * n_chunk

    def in_copy(base_row, blk):
        src = xs_hbm.at[pl.ds(pl.multiple_of((base_row + blk * ROW_ALIGN) * n_chunk, blk_words), blk_words)]
        return pltpu.make_async_copy(src, stage.at[pl.ds(pl.multiple_of(blk * blk_words, blk_words), blk_words)],
                                     sem_in)

    def start_in(base_row, n_blk):
        def issue(blk, c):
            in_copy(base_row, blk).start()
            return c

        lax.fori_loop(0, n_blk, issue, 0)

    def out_copy(blk):
        src = acc.at[pl.ds(pl.multiple_of(blk * ROW_ALIGN, ROW_ALIGN), ROW_ALIGN)]
        dst = ys_hbm.at[pl.ds(pl.multiple_of(r0 + blk * ROW_ALIGN, ROW_ALIGN), ROW_ALIGN)]
        return pltpu.make_async_copy(src, dst, sem_out)

    def drain():
        @pl.when(pending[0] > 0)
        def _():
            rows = pl.ds(0, pending[0] * ROW_ALIGN)
            pltpu.make_async_copy(acc.at[rows], ys_hbm.at[rows], sem_out).wait()
            pending[0] = 0

    @pl.when((i == 0) & (j == 0))
    def _():
        pending[0] = 0
        start_in(r0, n_align)

    @pl.when(active & (j == 0))
    def _():
        all_rows = pl.ds(0, n_rows * n_chunk)
        pltpu.make_async_copy(xs_hbm.at[all_rows], stage.at[all_rows], sem_in).wait()
        drain()

        def unpack(rb, c):
            rows = pl.ds(pl.multiple_of(rb * ROW_ALIGN, ROW_ALIGN), ROW_ALIGN)
            for ch in range(n_chunk):
                words = stage[pl.ds(rb * (ROW_ALIGN * n_chunk) + ch, ROW_ALIGN, stride=n_chunk), :]
                hi, lo = _unpack_pairs(words)
                xb16[rows, ch * LANES:(ch + 1) * LANES] = hi
                xb16[rows, d // 2 + ch * LANES:d // 2 + (ch + 1) * LANES] = lo
            acc[rows, :] = jnp.broadcast_to(bd_ref[0], (ROW_ALIGN, d))
            return c

        lax.fori_loop(0, n_align, unpack, 0)
        start_in(r0_next, blocks_next)

    def trip(row0, n_blk):
        rows = pl.ds(row0, n_blk * ROW_ALIGN)
        xb = xb16[rows, :]
        g = jnp.dot(xb, wg_ref[0].astype(BF16), preferred_element_type=F32) + bg_ref[0]
        u = jnp.dot(xb, wu_ref[0].astype(BF16), preferred_element_type=F32) + bu_ref[0]
        g = jnp.minimum(g, SWIGLU_LIMIT)
        u = jnp.clip(u, -SWIGLU_LIMIT, SWIGLU_LIMIT)
        a = (u + 1.0) * (g * jax.nn.sigmoid(SWIGLU_ALPHA * g))
        acc[rows, :] += jnp.dot(a.astype(BF16), wd_ref[0].astype(BF16), preferred_element_type=F32)

        @pl.when(j == nj - 1)
        def _():
            for s in range(n_blk):
                out_copy(row0 // ROW_ALIGN + s).start()
            pending[0] = pending[0] + n_blk

    @pl.when(active)
    def _():
        def main_trip(rb, c):
            trip(pl.multiple_of(rb * ROW_BLOCK, ROW_BLOCK), per_trip)
            return c

        lax.fori_loop(0, n_main, main_trip, 0)

        def tail_trip(t, c):
            trip(pl.multiple_of((n_main * per_trip + t) * ROW_ALIGN, ROW_ALIGN), 1)
            return c

        lax.fori_loop(0, n_align - n_main * per_trip, tail_trip, 0)

    @pl.when((i == pl.num_programs(0) - 1) & (j == nj - 1))
    def _():
        drain()
        acc[0:ROW_ALIGN, :] = jnp.zeros((ROW_ALIGN, d), F32)
        used = n_items[1]

        def tail_copy(t):
            dst = ys_hbm.at[pl.ds(pl.multiple_of((used + t) * ROW_ALIGN, ROW_ALIGN), ROW_ALIGN)]
            return pltpu.make_async_copy(acc.at[pl.ds(0, ROW_ALIGN)], dst, sem_out)

        def issue(t, c):
            tail_copy(t).start()
            return c

        n_tail = ys_hbm.shape[0] // ROW_ALIGN - used
        lax.fori_loop(0, n_tail, issue, 0)

        def wait(t, c):
            tail_copy(t).wait()
            return c

        lax.fori_loop(0, n_tail, wait, 0)


def _experts(xs, w_gate, b_gate, w_up, b_up, w_down, b_down, meta, *, n_rows, th):
    item_e, item_r0, item_nb, n_items = meta
    n_exp, d, dh = w_gate.shape
    nj = dh // th
    n_item_slots = item_e.shape[0]

    def w_col(i, j, item_e, item_r0, item_nb, n_items):
        return (item_e[i], 0, jnp.where(i < n_items[0], j, nj - 1))

    def w_row(i, j, item_e, item_r0, item_nb, n_items):
        return (item_e[i], jnp.where(i < n_items[0], j, nj - 1), 0)

    def b_all(i, j, item_e, item_r0, item_nb, n_items):
        return (item_e[i], 0, 0)

    grid_spec = pltpu.PrefetchScalarGridSpec(
        num_scalar_prefetch=4,
        grid=(n_item_slots, nj),
        in_specs=[
            pl.BlockSpec(memory_space=pl.ANY),
            pl.BlockSpec((1, d, th), w_col),
            pl.BlockSpec((1, 1, th), w_col),
            pl.BlockSpec((1, d, th), w_col),
            pl.BlockSpec((1, 1, th), w_col),
            pl.BlockSpec((1, th, d), w_row),
            pl.BlockSpec((1, 1, d), b_all),
        ],
        out_specs=pl.BlockSpec(memory_space=pl.ANY),
        scratch_shapes=[
            pltpu.VMEM((ITEM_ROWS * (d // 2 // LANES), LANES), jnp.uint32),
            pltpu.VMEM((ITEM_ROWS, d), BF16),
            pltpu.VMEM((ITEM_ROWS, d), F32),
            pltpu.SMEM((1,), jnp.int32),
            pltpu.SemaphoreType.DMA,
            pltpu.SemaphoreType.DMA,
        ],
    )
    return pl.pallas_call(
        functools.partial(_expert_kernel, nj=nj),
        grid_spec=grid_spec,
        out_shape=jax.ShapeDtypeStruct((n_rows, d), F32),
        compiler_params=_cparams(("arbitrary", "arbitrary")),
        name="experts",
    )(item_e, item_r0, item_nb, n_items,
      xs, w_gate, b_gate.reshape(n_exp, 1, dh), w_up, b_up.reshape(n_exp, 1, dh), w_down,
      b_down.reshape(n_exp, 1, d))


def _routing_metadata(top_idx, top_rank, counts, n_tok):
    n_assign = n_tok * TOP_K
    flat_e = top_idx.reshape(n_assign)
    rank = top_rank.reshape(n_assign)
    padded = (counts + ROW_ALIGN - 1) // ROW_ALIGN * ROW_ALIGN
    pend = jnp.cumsum(padded)
    pstart = pend - padded
    dest = (pstart[flat_e] + rank).astype(jnp.int32)
    n_rows = (n_assign // ROW_ALIGN + N_EXPERTS) * ROW_ALIGN
    flat_tok = jnp.arange(n_assign, dtype=jnp.int32) // TOP_K
    row_tok = jnp.zeros((n_rows,), jnp.int32).at[dest].set(
        flat_tok, unique_indices=True, mode="promise_in_bounds")
    n_chunks = (padded + ITEM_ROWS - 1) // ITEM_ROWS
    cend = jnp.cumsum(n_chunks)
    cstart = cend - n_chunks
    n_slots = N_EXPERTS + n_assign // ITEM_ROWS + 1
    slot = jnp.arange(n_slots, dtype=jnp.int32)
    n_items = cend[-1]
    last_e = jnp.max(jnp.where(n_chunks > 0, jnp.arange(N_EXPERTS), 0))
    e_of = jnp.minimum(jnp.searchsorted(cend, slot, side='right'), N_EXPERTS - 1)
    live = slot < n_items
    item_e = jnp.where(live, e_of, last_e).astype(jnp.int32)
    chunk = slot - cstart[e_of]
    item_r0 = jnp.where(live, pstart[e_of] + chunk * ITEM_ROWS, 0).astype(jnp.int32)
    item_rows = jnp.clip(padded[e_of] - chunk * ITEM_ROWS, 0, ITEM_ROWS)
    item_nb = jnp.where(live, item_rows // ROW_ALIGN, 0).astype(jnp.int32)
    counts_blk = jnp.stack([n_items, pend[-1] // ROW_ALIGN]).astype(jnp.int32)
    meta = (item_e, item_r0, item_nb, counts_blk)
    return meta, row_tok, dest, n_rows


def _combine_kernel(dest, ys_hbm, x2_ref, gate_ref, g_ref, o_ref, buf, sem, *, tm):
    i = pl.program_id(0)
    n = pl.num_programs(0)

    def row_copy(tile, slot, t, kk):
        r = dest[(tile * tm + t) * TOP_K + kk]
        return pltpu.make_async_copy(ys_hbm.at[pl.ds(r, 1)], buf.at[slot, kk, pl.ds(t, 1)], sem.at[slot])

    def wait_tile(slot):
        for kk in range(TOP_K):
            pltpu.make_async_copy(ys_hbm.at[pl.ds(0, tm)], buf.at[slot, kk], sem.at[slot]).wait()

    slot = i % 2

    @pl.when(i == 0)
    def _():
        def issue(t, c):
            for kk in range(TOP_K):
                row_copy(0, 0, t, kk).start()
            return c

        lax.fori_loop(0, tm, issue, 0, unroll=8)

    wait_tile(slot)
    nxt = jnp.minimum(i + 1, n - 1)
    for t in range(tm):
        for kk in range(TOP_K):
            row_copy(nxt, 1 - slot, t, kk).start()

    gates = gate_ref[...]
    y = x2_ref[...]
    for kk in range(TOP_K):
        y = y + gates[:, kk:kk + 1] * buf[slot, kk]
    o_ref[...] = (y * lax.rsqrt(jnp.mean(y * y, axis=-1, keepdims=True) + NORM_EPS)) * g_ref[...]

    @pl.when(i == n - 1)
    def _():
        wait_tile(1 - slot)


def _combine(dest, ys, x2, gates, g_final, *, tm):
    m, d = x2.shape
    grid_spec = pltpu.PrefetchScalarGridSpec(
        num_scalar_prefetch=1,
        grid=(m // tm,),
        in_specs=[
            pl.BlockSpec(memory_space=pl.ANY),
            pl.BlockSpec((tm, d), lambda i, dest: (i, 0)),
            pl.BlockSpec((tm, LANES), lambda i, dest: (i, 0)),
            pl.BlockSpec((1, d), lambda i, dest: (0, 0)),
        ],
        out_specs=pl.BlockSpec((tm, d), lambda i, dest: (i, 0)),
        scratch_shapes=[pltpu.VMEM((2, TOP_K, tm, d), F32), pltpu.SemaphoreType.DMA((2,))],
    )
    return pl.pallas_call(
        functools.partial(_combine_kernel, tm=tm),
        grid_spec=grid_spec,
        out_shape=jax.ShapeDtypeStruct((m, d), F32),
        compiler_params=_cparams(("arbitrary",)),
        name="combine",
    )(dest, ys, x2, gates, g_final)


def _layer(x, attn_norm, w_in, dil_out_norm, sb_out_norm, w_out, ffn_norm,
           w_router, b_router, w_gate, b_gate, w_up, b_up, w_down, b_down, *, tiles):
    b, seq, d = x.shape
    m = b * seq
    half_width = d // 2
    x2d = x.reshape(m, d)
    cos_t, sin_t = _rope_tables(seq)
    proj = _inproj(x2d, attn_norm.reshape(1, d), w_in, cos_t, sin_t,
                   seq=seq, tm=tiles["in_tm"], tn=tiles["in_tn"])
    proj3 = proj.reshape(b, seq, 3 * d)
    o_d = _dilated_attention(proj3, half_width=half_width).reshape(m, half_width)
    o_s = _stickbreak_attention(proj3, half_width=half_width, qt=tiles["sb_qt"]).reshape(m, half_width)
    x2, h2, idx_l, gate_l, rank_l, counts = _outproj_router(
        o_d, o_s, x2d, w_out, dil_out_norm.reshape(1, half_width), sb_out_norm.reshape(1, half_width),
        ffn_norm.reshape(1, d), w_router, b_router.reshape(1, N_EXPERTS),
        tm=tiles["out_tm"])
    meta, row_tok, dest, n_rows = _routing_metadata(idx_l[:, :TOP_K], rank_l[:, :TOP_K], counts[0], m)
    xs = _dispatch(h2, row_tok, rows_blk=tiles["disp_rows"], n_chunk=d // 2 // LANES)
    ys = _experts(xs, w_gate, b_gate, w_up, b_up, w_down, b_down, meta, n_rows=n_rows, th=tiles["moe_th"])
    return x2, ys, dest, gate_l


_TILES = dict(in_tm=1024, in_tn=512, sb_qt=512, out_tm=256, disp_rows=1024, moe_th=512, cmb_tm=128)


def kernel(x, attn_norm, w_in, dil_out_norm, sb_out_norm, w_out, ffn_norm, w_router, b_router,
           w_gate, b_gate, w_up, b_up, w_down, b_down, final_norm):
    depth = attn_norm.shape[0]
    b, seq, d = x.shape
    for l in range(depth):
        x2, ys, dest, gate_l = _layer(
            x, attn_norm[l], w_in[l], dil_out_norm[l], sb_out_norm[l], w_out[l], ffn_norm[l],
            w_router[l], b_router[l], w_gate[l], b_gate[l], w_up[l], b_up[l], w_down[l], b_down[l],
            tiles=_TILES)
        assert depth == 1
        x = _combine(dest, ys, x2, gate_l, final_norm.reshape(1, d), tm=_TILES["cmb_tm"])
    return x.reshape(b, seq, d)
```
